```python
import math
import jax
import jax.numpy as jnp
from jax import lax
import numpy as np

D_MODEL = 1024
BATCH = 8
SEQ = 4096
DEPTH = 4

SSD_HEADS = 16
SSD_HEAD_DIM = 64
SSD_INNER = SSD_HEADS * SSD_HEAD_DIM
SSD_GROUPS = 2
SSD_STATE = 128
SSD_CONV_DIM = SSD_INNER + 2 * SSD_GROUPS * SSD_STATE
CONV_WIDTH = 4
SSD_CHUNK = 128
ATT_HEADS = 8
ATT_KV_HEADS = 2
ATT_HEAD_DIM = 64
ATT_WIDTH = ATT_HEADS * ATT_HEAD_DIM
IDX_HEADS = 4
IDX_HEAD_DIM = 64
TOPK_MAX = 256
Q_BLOCK = 128
POOL_WINDOWS = (2, 4, 8, 16)
POOL_GROUPS = 4
POOL_GROUP_DIM = 128
POOL_WIDTH = POOL_GROUPS * POOL_GROUP_DIM
N_BRANCH = 3
FFN_HIDDEN = ((8 * D_MODEL // 3 + 255) // 256) * 256
EPS = 1e-6
IN_SIZES = (SSD_INNER, SSD_CONV_DIM, SSD_HEADS, ATT_WIDTH, ATT_KV_HEADS * ATT_HEAD_DIM, ATT_KV_HEADS * ATT_HEAD_DIM, IDX_HEADS * IDX_HEAD_DIM, IDX_HEAD_DIM, IDX_HEADS, POOL_WIDTH, N_BRANCH * D_MODEL)
IN_WIDTH = SSD_INNER + SSD_CONV_DIM + SSD_HEADS + ATT_WIDTH + 2 * ATT_KV_HEADS * ATT_HEAD_DIM + IDX_HEADS * IDX_HEAD_DIM + IDX_HEAD_DIM + IDX_HEADS + POOL_WIDTH + N_BRANCH * D_MODEL

kernel_name = "hybrid_ssd_dsa_pool_gated_block"


def _rmsnorm(x, w):
    xf = x.astype(jnp.float32)
    y = xf * lax.rsqrt(jnp.mean(xf * xf, axis=-1, keepdims=True) + EPS)
    return (y * w.astype(jnp.float32)).astype(x.dtype)


def _split_proj(p):
    offs, acc = [], 0
    for s in IN_SIZES[:-1]:
        acc += s
        offs.append(acc)
    return jnp.split(p, offs, axis=-1)


def _causal_dwconv(u, w, b):
    out = lax.conv_general_dilated(u, w[:, None, :].astype(u.dtype), window_strides=(1,), padding=[(CONV_WIDTH - 1, 0)], dimension_numbers=('NWC', 'WIO', 'NWC'), feature_group_count=u.shape[-1])
    return out + b.astype(u.dtype)


def _ssd_scan(xh, dt, A, Bm, Cm):
    out_dtype = xh.dtype
    b, T, g, j, p = xh.shape
    n = Bm.shape[-1]
    c = T // SSD_CHUNK
    f32 = jnp.float32
    x = (xh.astype(f32) * dt.astype(f32)[..., None]).reshape(b, c, SSD_CHUNK, g, j, p)
    a = (dt.astype(f32) * A.astype(f32)).reshape(b, c, SSD_CHUNK, g, j)
    Bc = Bm.astype(f32).reshape(b, c, SSD_CHUNK, g, n)
    Cc = Cm.astype(f32).reshape(b, c, SSD_CHUNK, g, n)
    a_cs = jnp.cumsum(a, axis=2)
    causal = jnp.tril(jnp.ones((SSD_CHUNK, SSD_CHUNK), dtype=bool))[None, None, :, :, None, None]
    seg = a_cs[:, :, :, None] - a_cs[:, :, None, :]
    lmat = jnp.exp(jnp.where(causal, seg, -jnp.inf))
    cb = jnp.einsum('bclgn,bcsgn->bclsg', Cc, Bc)
    y_diag = jnp.einsum('bclsg,bclsgj,bcsgjp->bclgjp', cb, lmat, x)
    decay_states = jnp.exp(a_cs[:, :, -1:] - a_cs)
    states = jnp.einsum('bclgn,bclgj,bclgjp->bcgjpn', Bc, decay_states, x)
    chunk_decay = jnp.exp(a_cs[:, :, -1])

    def step(h, inp):
        s_c, d_c = inp
        return h * d_c[..., None, None] + s_c, h

    h0 = jnp.zeros((b, g, j, p, n), f32)
    _, prev = lax.scan(step, h0, (jnp.swapaxes(states, 0, 1), jnp.swapaxes(chunk_decay, 0, 1)))
    prev = jnp.swapaxes(prev, 0, 1)
    y_off = jnp.einsum('bclgn,bcgjpn,bclgj->bclgjp', Cc, prev, jnp.exp(a_cs))
    return (y_diag + y_off).reshape(b, T, g, j, p).astype(out_dtype)


def _dsa_attention(q, k, v, qi, ki, wi):
    b, T = q.shape[0], q.shape[1]
    topk = min(TOPK_MAX, T // 4)
    nb = T // Q_BLOCK
    att_scale = ATT_HEAD_DIM ** -0.5
    idx_scale = (IDX_HEADS ** -0.5) * (IDX_HEAD_DIM ** -0.5)
    key_pos = jnp.arange(T)
    ki32 = ki.astype(jnp.float32)

    def blockify(a):
        return jnp.swapaxes(a.reshape((b, nb, Q_BLOCK) + a.shape[2:]), 0, 1)

    def one_block(args):
        qb, qib, wib, pos = args
        s = jnp.einsum('bqhd,bsd->bqhs', qib.astype(jnp.float32), ki32)
        score = jnp.einsum('bqhs,bqh->bqs', jax.nn.relu(s), wib.astype(jnp.float32) * idx_scale)
        admissible = key_pos[None, None, :] <= pos[None, :, None]
        score = jnp.where(admissible, score, -jnp.inf)
        _, idx = lax.top_k(score, topk)
        valid = idx <= pos[None, :, None]
        ksel = jax.vmap(lambda kb, ib: kb[ib])(k, idx)
        vsel = jax.vmap(lambda vb, ib: vb[ib])(v, idx)
        logits = jnp.einsum('bqhgd,bqkhd->bqhgk', qb, ksel).astype(jnp.float32) * att_scale
        logits = jnp.where(valid[:, :, None, None, :], logits, -jnp.inf)
        prob = jax.nn.softmax(logits, axis=-1)
        return jnp.einsum('bqhgk,bqkhd->bqhgd', prob.astype(vsel.dtype), vsel)

    pos_blocks = jnp.arange(T).reshape(nb, Q_BLOCK)
    out = lax.map(one_block, (blockify(q), blockify(qi), blockify(wi), pos_blocks))
    return jnp.swapaxes(out, 0, 1).reshape(b, T, ATT_WIDTH)


def _pool_mixer(u, w, scale):
    b, T, _ = u.shape
    ug = u.reshape(b, T, POOL_GROUPS, POOL_GROUP_DIM).astype(jnp.float32)
    cs = jnp.cumsum(ug, axis=1)
    t1 = jnp.arange(1, T + 1, dtype=jnp.float32)
    outs = []
    for gi, win in enumerate(POOL_WINDOWS):
        c = cs[:, :, gi]
        lag = jnp.pad(c, ((0, 0), (win, 0), (0, 0)))[:, :T]
        cnt = jnp.minimum(t1, float(win))[None, :, None]
        outs.append((c - lag) / cnt - ug[:, :, gi])
    pooled = jnp.stack(outs, axis=2).astype(u.dtype)
    mixed = jnp.einsum('btgc,gcd->btgd', pooled, w)
    return mixed.reshape(b, T, POOL_WIDTH) * scale


def _layer(h, norm1_w, w_in, conv_w, conv_b, dt_bias, a_log, d_skip, ssd_norm_w, pool_w, pool_scale, w_up_ssd, w_up_attn, w_up_pool, w_out, norm2_w, w_ffn_in, w_ffn_out):
    b, T, _ = h.shape
    xn = _rmsnorm(h, norm1_w)
    proj = xn @ w_in
    z, xbc, dt_raw, q, k, v, qi, ki, wi, u, gates = _split_proj(proj)
    xbc = jax.nn.silu(_causal_dwconv(xbc, conv_w, conv_b))
    xs, Bm, Cm = jnp.split(xbc, [SSD_INNER, SSD_INNER + SSD_GROUPS * SSD_STATE], axis=-1)
    dt = jax.nn.softplus(dt_raw + dt_bias)
    A = -jnp.exp(a_log)
    hpg = SSD_HEADS // SSD_GROUPS
    xh = xs.reshape(b, T, SSD_GROUPS, hpg, SSD_HEAD_DIM)
    y = _ssd_scan(xh, dt.reshape(b, T, SSD_GROUPS, hpg), A.reshape(SSD_GROUPS, hpg), Bm.reshape(b, T, SSD_GROUPS, SSD_STATE), Cm.reshape(b, T, SSD_GROUPS, SSD_STATE))
    y = y + d_skip.reshape(SSD_GROUPS, hpg)[..., None] * xh
    y_ssd = _rmsnorm(y.reshape(b, T, SSD_INNER) * jax.nn.silu(z), ssd_norm_w)
    qh = q.reshape(b, T, ATT_KV_HEADS, ATT_HEADS // ATT_KV_HEADS, ATT_HEAD_DIM)
    kh = k.reshape(b, T, ATT_KV_HEADS, ATT_HEAD_DIM)
    vh = v.reshape(b, T, ATT_KV_HEADS, ATT_HEAD_DIM)
    y_att = _dsa_attention(qh, kh, vh, qi.reshape(b, T, IDX_HEADS, IDX_HEAD_DIM), ki, wi)
    y_pool = _pool_mixer(u, pool_w, pool_scale)
    g = jax.nn.sigmoid(gates.reshape(b, T, N_BRANCH, D_MODEL))
    merged = g[:, :, 0] * (y_ssd @ w_up_ssd) + g[:, :, 1] * (y_att @ w_up_attn) + g[:, :, 2] * (y_pool @ w_up_pool)
    h = h + merged @ w_out
    hn = _rmsnorm(h, norm2_w)
    a_part, b_part = jnp.split(hn @ w_ffn_in, 2, axis=-1)
    return h + (jax.nn.silu(a_part) * b_part) @ w_ffn_out


def setup_inputs(seed: int = 0) -> dict:
    key = jax.random.key(seed)
    ks = jax.random.split(key, 20)
    f32 = jnp.float32
    L = DEPTH

    def nrm(k, shape, fan_in):
        return jax.random.normal(k, shape, f32) * (fan_in ** -0.5)

    def gain(k, shape):
        return 1.0 + 0.02 * jax.random.normal(k, shape, f32)

    u_dt = jax.random.uniform(ks[5], (L, SSD_HEADS), f32)
    dt0 = jnp.exp(u_dt * (math.log(0.1) - math.log(0.001)) + math.log(0.001))
    dt_bias = dt0 + jnp.log(-jnp.expm1(-dt0))
    return {
        "x": jax.random.normal(ks[0], (BATCH, SEQ, D_MODEL), f32),
        "norm1_w": gain(ks[1], (L, D_MODEL)),
        "w_in": nrm(ks[2], (L, D_MODEL, IN_WIDTH), D_MODEL),
        "conv_w": nrm(ks[3], (L, CONV_WIDTH, SSD_CONV_DIM), CONV_WIDTH),
        "conv_b": 0.02 * jax.random.normal(ks[4], (L, SSD_CONV_DIM), f32),
        "dt_bias": dt_bias,
        "a_log": jnp.log(jax.random.uniform(ks[6], (L, SSD_HEADS), f32, minval=1.0, maxval=16.0)),
        "d_skip": gain(ks[7], (L, SSD_HEADS)),
        "ssd_norm_w": gain(ks[8], (L, SSD_INNER)),
        "pool_w": nrm(ks[9], (L, POOL_GROUPS, POOL_GROUP_DIM, POOL_GROUP_DIM), POOL_GROUP_DIM),
        "pool_scale": gain(ks[10], (L, POOL_WIDTH)),
        "w_up_ssd": nrm(ks[11], (L, SSD_INNER, D_MODEL), SSD_INNER),
        "w_up_attn": nrm(ks[12], (L, ATT_WIDTH, D_MODEL), ATT_WIDTH),
        "w_up_pool": nrm(ks[13], (L, POOL_WIDTH, D_MODEL), POOL_WIDTH),
        "w_out": nrm(ks[14], (L, D_MODEL, D_MODEL), D_MODEL),
        "norm2_w": gain(ks[15], (L, D_MODEL)),
        "w_ffn_in": nrm(ks[16], (L, D_MODEL, 2 * FFN_HIDDEN), D_MODEL),
        "w_ffn_out": nrm(ks[17], (L, FFN_HIDDEN, D_MODEL), FFN_HIDDEN),
        "final_norm_w": gain(ks[18], (D_MODEL,)),
    }


def reference(x, norm1_w, w_in, conv_w, conv_b, dt_bias, a_log, d_skip, ssd_norm_w, pool_w, pool_scale, w_up_ssd, w_up_attn, w_up_pool, w_out, norm2_w, w_ffn_in, w_ffn_out, final_norm_w):
    h = x
    for l in range(DEPTH):
        h = _layer(h, norm1_w[l], w_in[l], conv_w[l], conv_b[l], dt_bias[l], a_log[l], d_skip[l], ssd_norm_w[l], pool_w[l], pool_scale[l], w_up_ssd[l], w_up_attn[l], w_up_pool[l], w_out[l], norm2_w[l], w_ffn_in[l], w_ffn_out[l])
    return _rmsnorm(h, final_norm_w)
```

```python
import functools

import jax
import jax.numpy as jnp
from jax import lax
from jax.experimental import pallas as pl
from jax.experimental.pallas import tpu as pltpu

D_MODEL = 1024
DEPTH = 4
SSD_HEADS = 16
SSD_HEAD_DIM = 64
SSD_INNER = SSD_HEADS * SSD_HEAD_DIM
SSD_GROUPS = 2
SSD_STATE = 128
SSD_CONV_DIM = SSD_INNER + 2 * SSD_GROUPS * SSD_STATE
CONV_WIDTH = 4
SSD_CHUNK = 128
ATT_HEADS = 8
ATT_KV_HEADS = 2
ATT_HEAD_DIM = 64
ATT_WIDTH = ATT_HEADS * ATT_HEAD_DIM
IDX_HEADS = 4
IDX_HEAD_DIM = 64
TOPK_MAX = 256
POOL_WINDOWS = (2, 4, 8, 16)
POOL_GROUPS = 4
POOL_GROUP_DIM = 128
POOL_WIDTH = POOL_GROUPS * POOL_GROUP_DIM
N_BRANCH = 3
FFN_HIDDEN = 2816
EPS = 1e-6
ATT_SCALE = ATT_HEAD_DIM ** -0.5
IDX_SCALE = (IDX_HEADS ** -0.5) * (IDX_HEAD_DIM ** -0.5)

LANES = 128
SUBLANES = 8
VMEM_BUDGET_BYTES = 60000 * 1024

ROW_TILE = 512
Q_TILE = LANES
KEY_CHUNK = 512
POOL_HALO = 16
FFN_SPLIT = 2

SEG_Z = SSD_INNER
SEG_XBC = SSD_CONV_DIM
SEG_Q = ATT_WIDTH
SEG_KV = 2 * ATT_KV_HEADS * 2 * ATT_HEAD_DIM
SEG_QI = IDX_HEADS * IDX_HEAD_DIM
SEG_KI = 2 * IDX_HEAD_DIM
SEG_MISC = LANES
SEG_U = POOL_WIDTH
SEG_G = N_BRANCH * D_MODEL
SEGS = (SEG_Z, SEG_XBC, SEG_Q, SEG_KV, SEG_QI, SEG_KI, SEG_MISC, SEG_U, SEG_G)
W_CAT = sum(SEGS)
MISC_DT = 0
MISC_WI = SSD_HEADS

INT_MIN = -(2 ** 31)
NEG_BIG = -1e30
IDX_BIG = 2 ** 30

_NT = (((1,), (1,)), ((), ()))


def _bf(x):
    return x.astype(jnp.bfloat16)


def _sigmoid(x):
    return 1.0 / (1.0 + jnp.exp(-x))


def _rms(x, w):
    return x * lax.rsqrt(jnp.mean(x * x, axis=-1, keepdims=True) + EPS) * w


def _params(sem, est_bytes):
    return pltpu.CompilerParams(dimension_semantics=sem,
                                vmem_limit_bytes=int(min(VMEM_BUDGET_BYTES, est_bytes)))


def _layer_spec(shape, l):
    nd = len(shape)
    return pl.BlockSpec((None,) + tuple(shape), lambda *_: (l,) + (0,) * nd, pipeline_mode=pl.Buffered(1))


def _inproj_kernel(h_ref, nw_ref, w_ref, *out_refs):
    xn = _bf(_rms(h_ref[...], nw_ref[...]))
    off = 0
    for ref, width in zip(out_refs, SEGS):
        r = jnp.dot(xn, w_ref[:, off:off + width], preferred_element_type=jnp.float32)
        ref[...] = r.astype(ref.dtype)
        off += width


def _inproj(h, norm_w, wcat, l):
    n = h.shape[0]
    tm = ROW_TILE
    dts = (jnp.float32, jnp.float32, jnp.bfloat16, jnp.bfloat16, jnp.bfloat16, jnp.bfloat16,
           jnp.float32, jnp.float32, jnp.float32)
    out_shape = tuple(jax.ShapeDtypeStruct((n, w), dt) for w, dt in zip(SEGS, dts))
    out_specs = tuple(pl.BlockSpec((tm, w), lambda i: (i, 0)) for w in SEGS)
    out_bytes = sum(tm * w * jnp.dtype(dt).itemsize for w, dt in zip(SEGS, dts))
    est = D_MODEL * W_CAT * 2 + 2 * tm * D_MODEL * 4 + 2 * out_bytes + 3 * tm * SEG_G * 4
    return pl.pallas_call(
        _inproj_kernel,
        grid=(n // tm,),
        in_specs=[pl.BlockSpec((tm, D_MODEL), lambda i: (i, 0)),
                  _layer_spec((1, D_MODEL), l),
                  _layer_spec((D_MODEL, W_CAT), l)],
        out_specs=out_specs,
        out_shape=out_shape,
        compiler_params=_params(("parallel",), est),
        name="inproj",
    )(h, norm_w, wcat)


def _col(a, j, width=LANES):
    return jnp.broadcast_to(a[:, j:j + 1], (a.shape[0], width))


def _expand_heads(a, left):
    return jnp.concatenate(
        [jnp.where(left, _col(a, 2 * m), _col(a, 2 * m + 1)) for m in range(SSD_HEADS // 2)], axis=1)


def _ssd_kernel(xbc_ref, z_ref, misc_ref, cw_ref, cb_ref, dtb_ref, alog_ref, dskip_ref, nw_ref, o_ref,
                cbuf, state):
    L = SSD_CHUNK
    f32 = jnp.float32

    @pl.when(pl.program_id(1) == 0)
    def _():
        cbuf[0:SUBLANES, :] = jnp.zeros((SUBLANES, SSD_CONV_DIM), f32)
        state[...] = jnp.zeros(state.shape, f32)

    cbuf[SUBLANES:SUBLANES + L, :] = xbc_ref[...]
    conv = cb_ref[...]
    for k in range(CONV_WIDTH):
        start = SUBLANES - (CONV_WIDTH - 1) + k
        conv = conv + cw_ref[k:k + 1, :] * cbuf[start:start + L, :]
    cbuf[0:SUBLANES, :] = cbuf[L:L + SUBLANES, :]
    xbc = conv * _sigmoid(conv)
    xs = xbc[:, :SSD_INNER]

    lane = lax.broadcasted_iota(jnp.int32, (L, LANES), 1)
    row = lax.broadcasted_iota(jnp.int32, (L, LANES), 0)
    left = lane < SSD_HEAD_DIM
    causal = lane <= row
    tri = jnp.where(causal, 1.0, 0.0)

    dtr = misc_ref[...] + dtb_ref[...]
    dt = jnp.maximum(dtr, 0.0) + jnp.log1p(jnp.exp(-jnp.abs(dtr)))
    a = dt * (-jnp.exp(alog_ref[...]))
    a_cs = jnp.dot(tri, a, preferred_element_type=f32, precision=lax.Precision.HIGHEST)
    a_cs_t = a_cs.T
    ea = jnp.exp(a_cs)
    decay = jnp.exp(a_cs[L - 1:L, :] - a_cs)

    x_all = xs * _expand_heads(dt, left)
    ea_x = _expand_heads(ea, left)
    xd_all = x_all * _expand_heads(decay, left)

    gw = SSD_INNER // SSD_GROUPS
    y_parts = []
    for g in range(SSD_GROUPS):
        b_g = xbc[:, SSD_INNER + g * SSD_STATE:SSD_INNER + (g + 1) * SSD_STATE]
        c_off = SSD_INNER + SSD_GROUPS * SSD_STATE
        c_g = xbc[:, c_off + g * SSD_STATE:c_off + (g + 1) * SSD_STATE]
        cb = lax.dot_general(_bf(c_g), _bf(b_g), _NT, preferred_element_type=f32)
        st = state[g]
        y_off = jnp.dot(_bf(c_g), _bf(st), preferred_element_type=f32)
        new_st = jnp.dot(_bf(b_g.T), _bf(xd_all[:, g * gw:(g + 1) * gw]), preferred_element_type=f32)
        state[g] = st * ea_x[L - 1:L, g * gw:(g + 1) * gw] + new_st
        pairs = []
        for mm in range(SSD_HEADS // SSD_GROUPS // 2):
            m = g * (SSD_HEADS // SSD_GROUPS // 2) + mm
            lhs = []
            for hd in (2 * m, 2 * m + 1):
                seg = _col(a_cs, hd) - a_cs_t[hd:hd + 1, :]
                lhs.append(cb * jnp.where(causal, jnp.exp(seg), 0.0))
            xp = x_all[:, m * LANES:(m + 1) * LANES]
            rhs = jnp.concatenate([jnp.where(left, xp, 0.0), jnp.where(left, 0.0, xp)], axis=0)
            pairs.append(jnp.dot(_bf(jnp.concatenate(lhs, axis=1)), _bf(rhs), preferred_element_type=f32))
        y_parts.append(jnp.concatenate(pairs, axis=1) + y_off * ea_x[:, g * gw:(g + 1) * gw])
    y = jnp.concatenate(y_parts, axis=1) + dskip_ref[...] * xs
    z = z_ref[...]
    o_ref[...] = _bf(_rms(y * (z * _sigmoid(z)), nw_ref[...]))


def _ssd(xbc, z, misc, conv_w, conv_b, dtb, alog, dskip_x, norm_w, l, batch, seq):
    L = SSD_CHUNK
    nc = seq // L
    row_spec = lambda w: pl.BlockSpec((L, w), lambda b, c: (b * nc + c, 0))
    est = (2 * L * (SSD_CONV_DIM + SSD_INNER + LANES) * 4 + 2 * L * SSD_INNER * 2
           + (L + 2 * SUBLANES) * SSD_CONV_DIM * 4 + SSD_GROUPS * SSD_STATE * SSD_INNER // SSD_GROUPS * 4
           + 40 * L * SSD_INNER * 4)
    return pl.pallas_call(
        _ssd_kernel,
        grid=(batch, nc),
        in_specs=[row_spec(SSD_CONV_DIM), row_spec(SSD_INNER), row_spec(LANES),
                  _layer_spec((CONV_WIDTH, SSD_CONV_DIM), l), _layer_spec((1, SSD_CONV_DIM), l),
                  _layer_spec((1, LANES), l), _layer_spec((1, LANES), l),
                  _layer_spec((1, SSD_INNER), l), _layer_spec((1, SSD_INNER), l)],
        out_specs=row_spec(SSD_INNER),
        out_shape=jax.ShapeDtypeStruct((batch * seq, SSD_INNER), jnp.bfloat16),
        scratch_shapes=[pltpu.VMEM((L + 2 * SUBLANES, SSD_CONV_DIM), jnp.float32),
                        pltpu.VMEM((SSD_GROUPS, SSD_STATE, SSD_INNER // SSD_GROUPS), jnp.float32)],
        compiler_params=_params(("parallel", "arbitrary"), est),
        name="ssd",
    )(xbc, z, misc, conv_w, conv_b, dtb, alog, dskip_x, norm_w)


def _dsa_kernel(q_ref, qi_ref, misc_ref, kv_ref, ki_ref, o_ref, key_ref, m_ref, l_ref, acc_ref, *, topk, idx_bits):
    f32 = jnp.float32
    i32 = jnp.int32
    KC = KEY_CHUNK
    i = pl.program_id(1)
    nch = (i * Q_TILE + Q_TILE + KC - 1) // KC
    lane = lax.broadcasted_iota(i32, (1, LANES), 1)
    qpos = i * Q_TILE + lane
    left = lane < ATT_HEAD_DIM
    sub_iota = lax.broadcasted_iota(i32, (KC, LANES), 0)

    misc_t = misc_ref[...].T
    qi = qi_ref[...].astype(f32)
    qi_heads, w_heads = [], []
    for h in range(IDX_HEADS):
        qp = qi[:, (h // 2) * LANES:(h // 2 + 1) * LANES]
        qi_heads.append(_bf(jnp.where(left, qp, 0.0) if h % 2 == 0 else jnp.where(left, 0.0, qp)))
        w_heads.append(misc_t[MISC_WI + h:MISC_WI + h + 1, :] * IDX_SCALE)

    def score_body(c, carry):
        row0 = pl.multiple_of(c * KC, KC)
        ki_c = ki_ref[pl.ds(row0, KC), :]
        sc = jnp.zeros((KC, LANES), f32)
        for h in range(IDX_HEADS):
            s = lax.dot_general(ki_c, qi_heads[h], _NT, preferred_element_type=f32)
            sc = sc + jnp.maximum(s, 0.0) * w_heads[h]
        sc = jnp.where(sc == 0.0, 0.0, sc)
        bits = lax.bitcast_convert_type(sc, i32)
        key = bits ^ ((bits >> 31) & 0x7FFFFFFF)
        key_ref[c] = jnp.where(row0 + sub_iota <= qpos, key, INT_MIN)
        return carry

    lax.fori_loop(0, nch, score_body, 0)

    def count(pred):
        def body(c, acc):
            ind = jnp.where(pred(key_ref[c]), 1.0, 0.0)
            return acc + jnp.sum(ind.reshape(KC // SUBLANES, SUBLANES, LANES), axis=0)
        acc = lax.fori_loop(0, nch, body, jnp.zeros((SUBLANES, LANES), f32))
        return jnp.sum(acc, axis=0, keepdims=True)

    kf = float(topk)

    def bit_body(it, t):
        cand = t + lax.shift_left(jnp.int32(1), 31 - it)
        return jnp.where(count(lambda k: k >= cand) >= kf, cand, t)

    t = lax.fori_loop(0, 32, bit_body, jnp.full((1, LANES), INT_MIN, i32))

    def tie_body(c, carry):
        k = key_ref[c]
        key_ref[c] = jnp.where(k > t, -1, jnp.where(k == t, c * KC + sub_iota, IDX_BIG))
        return carry

    lax.fori_loop(0, nch, tie_body, 0)

    def idx_body(it, j0):
        cand = j0 + lax.shift_left(jnp.int32(1), idx_bits - 1 - it)
        return jnp.where(count(lambda k: k < cand) < kf, cand, j0)

    j0 = lax.fori_loop(0, idx_bits, idx_body, jnp.zeros((1, LANES), i32))
    j0 = jnp.where(t == INT_MIN, -1, j0)

    q = q_ref[...].astype(f32)
    hpg = ATT_HEADS // ATT_KV_HEADS
    q_stacks = []
    for g in range(ATT_KV_HEADS):
        parts = []
        for hh in range(hpg):
            h = g * hpg + hh
            qp = q[:, (h // 2) * LANES:(h // 2 + 1) * LANES]
            parts.append(jnp.where(left, qp, 0.0) if h % 2 == 0 else jnp.where(left, 0.0, qp))
        q_stacks.append(_bf(jnp.concatenate(parts, axis=0) * ATT_SCALE))
    eye = _bf(jnp.where(lax.broadcasted_iota(i32, (LANES, LANES), 0) == lax.broadcasted_iota(i32, (LANES, LANES), 1),
                        1.0, 0.0))
    m_ref[...] = jnp.full(m_ref.shape, NEG_BIG, f32)
    l_ref[...] = jnp.zeros(l_ref.shape, f32)
    acc_ref[...] = jnp.zeros(acc_ref.shape, f32)

    def att_body(c, carry):
        row0 = pl.multiple_of(c * KC, KC)
        sel_t = _bf(jnp.where(key_ref[c] <= j0, 1.0, 0.0))
        sel = lax.dot_general(eye, sel_t, _NT, preferred_element_type=f32)
        bias = pltpu.repeat((sel - 1.0) * (-NEG_BIG), hpg, axis=0)
        kv_c = kv_ref[pl.ds(row0, KC), :]
        for g in range(ATT_KV_HEADS):
            k2 = kv_c[:, g * LANES:(g + 1) * LANES]
            v2 = kv_c[:, (ATT_KV_HEADS + g) * LANES:(ATT_KV_HEADS + g + 1) * LANES]
            s = lax.dot_general(q_stacks[g], k2, _NT, preferred_element_type=f32) + bias
            m_prev = m_ref[g]
            m_new = jnp.maximum(m_prev, jnp.max(s, axis=1, keepdims=True))
            alpha = jnp.exp(m_prev - m_new)
            p = jnp.exp(s - pltpu.repeat(m_new, KC // LANES, axis=1))
            l_ref[g] = alpha * l_ref[g] + jnp.sum(p, axis=1, keepdims=True)
            acc_ref[g] = alpha * acc_ref[g] + jnp.dot(_bf(p), v2, preferred_element_type=f32)
            m_ref[g] = m_new
        return carry

    lax.fori_loop(0, nch, att_body, 0)

    outs = []
    for g in range(ATT_KV_HEADS):
        o = acc_ref[g] / l_ref[g]
        for pp in range(hpg // 2):
            outs.append(jnp.where(left, o[(2 * pp) * Q_TILE:(2 * pp + 1) * Q_TILE],
                                  o[(2 * pp + 1) * Q_TILE:(2 * pp + 2) * Q_TILE]))
    o_ref[...] = _bf(jnp.concatenate(outs, axis=1))


def _dsa(q, qi, misc, kv, ki, batch, seq):
    nq = seq // Q_TILE
    nkc = seq // KEY_CHUNK
    topk = min(TOPK_MAX, seq // 4)
    idx_bits = max(1, (seq - 1).bit_length())
    hpg = ATT_HEADS // ATT_KV_HEADS
    qrow = lambda w: pl.BlockSpec((Q_TILE, w), lambda b, i: (b * nq + i, 0))
    seq_spec = lambda w: pl.BlockSpec((seq, w), lambda b, i: (b, 0))
    est = (2 * seq * (SEG_KV + SEG_KI) * 2 + nkc * KEY_CHUNK * LANES * 4
           + 3 * ATT_KV_HEADS * hpg * Q_TILE * LANES * 4 + 16 * hpg * Q_TILE * KEY_CHUNK * 4)
    return pl.pallas_call(
        functools.partial(_dsa_kernel, topk=topk, idx_bits=idx_bits),
        grid=(batch, nq),
        in_specs=[qrow(SEG_Q), qrow(SEG_QI), qrow(SEG_MISC), seq_spec(SEG_KV), seq_spec(SEG_KI)],
        out_specs=qrow(ATT_WIDTH),
        out_shape=jax.ShapeDtypeStruct((batch * seq, ATT_WIDTH), jnp.bfloat16),
        scratch_shapes=[pltpu.VMEM((nkc, KEY_CHUNK, LANES), jnp.int32),
                        pltpu.VMEM((ATT_KV_HEADS, hpg * Q_TILE, LANES), jnp.float32),
                        pltpu.VMEM((ATT_KV_HEADS, hpg * Q_TILE, LANES), jnp.float32),
                        pltpu.VMEM((ATT_KV_HEADS, hpg * Q_TILE, LANES), jnp.float32)],
        compiler_params=_params(("parallel", "arbitrary"), est),
        name="dsa",
    )(q, qi, misc, kv, ki)


def _merge_kernel(yssd_ref, yatt_ref, u_ref, uh_ref, g_ref, h_ref, pw_ref, ps_ref, wus_ref, wua_ref, wup_ref,
                  wo_ref, o_ref, ubuf, *, tiles_per_seq):
    f32 = jnp.float32
    tm = u_ref.shape[0]
    it = pl.program_id(0) % tiles_per_seq
    ubuf[POOL_HALO:POOL_HALO + tm, :] = u_ref[...]
    ubuf[0:POOL_HALO, :] = jnp.where(it == 0, 0.0, uh_ref[...])
    t1 = (it * tm + 1 + lax.broadcasted_iota(jnp.int32, (tm, POOL_GROUP_DIM), 0)).astype(f32)
    mixed = []
    for gi, win in enumerate(POOL_WINDOWS):
        cols = slice(gi * POOL_GROUP_DIM, (gi + 1) * POOL_GROUP_DIM)
        cur = ubuf[POOL_HALO:POOL_HALO + tm, cols]
        s = cur
        for d in range(1, win):
            s = s + ubuf[POOL_HALO - d:POOL_HALO - d + tm, cols]
        pooled = s / jnp.minimum(t1, float(win)) - cur
        mixed.append(jnp.dot(_bf(pooled), pw_ref[gi], preferred_element_type=f32))
    ypool = _bf(jnp.concatenate(mixed, axis=1) * ps_ref[...])

    g = g_ref[...]
    merged = (_sigmoid(g[:, :D_MODEL]) * jnp.dot(yssd_ref[...], wus_ref[...], preferred_element_type=f32)
              + _sigmoid(g[:, D_MODEL:2 * D_MODEL]) * jnp.dot(yatt_ref[...], wua_ref[...], preferred_element_type=f32)
              + _sigmoid(g[:, 2 * D_MODEL:]) * jnp.dot(ypool, wup_ref[...], preferred_element_type=f32))
    o_ref[...] = h_ref[...] + jnp.dot(_bf(merged), wo_ref[...], preferred_element_type=f32)


def _merge(y_ssd, y_att, u, gates, h, pool_w, pool_scale, w_up_ssd, w_up_attn, w_up_pool, w_out, l, seq):
    n = h.shape[0]
    tm = ROW_TILE
    tiles_per_seq = seq // tm
    row = lambda w: pl.BlockSpec((tm, w), lambda r: (r, 0))
    halo = pl.BlockSpec((POOL_HALO, POOL_WIDTH), lambda r: (jnp.maximum(r * (tm // POOL_HALO) - 1, 0), 0))
    w_bytes = (POOL_GROUPS * POOL_GROUP_DIM ** 2 + (SSD_INNER + ATT_WIDTH + POOL_WIDTH + D_MODEL) * D_MODEL) * 2
    est = (w_bytes + 2 * tm * ((SSD_INNER + ATT_WIDTH) * 2 + (POOL_WIDTH + SEG_G + 2 * D_MODEL) * 4)
           + (tm + POOL_HALO) * POOL_WIDTH * 4 + 12 * tm * D_MODEL * 4)
    return pl.pallas_call(
        functools.partial(_merge_kernel, tiles_per_seq=tiles_per_seq),
        grid=(n // tm,),
        in_specs=[row(SSD_INNER), row(ATT_WIDTH), row(POOL_WIDTH), halo, row(SEG_G), row(D_MODEL),
                  _layer_spec((POOL_GROUPS, POOL_GROUP_DIM, POOL_GROUP_DIM), l), _layer_spec((1, POOL_WIDTH), l),
                  _layer_spec((SSD_INNER, D_MODEL), l), _layer_spec((ATT_WIDTH, D_MODEL), l),
                  _layer_spec((POOL_WIDTH, D_MODEL), l), _layer_spec((D_MODEL, D_MODEL), l)],
        out_specs=row(D_MODEL),
        out_shape=jax.ShapeDtypeStruct((n, D_MODEL), jnp.float32),
        scratch_shapes=[pltpu.VMEM((tm + POOL_HALO, POOL_WIDTH), jnp.float32)],
        compiler_params=_params(("parallel",), est),
        name="merge",
    )(y_ssd, y_att, u, u, gates, h, pool_w, pool_scale, w_up_ssd, w_up_attn, w_up_pool, w_out)


def _ffn_kernel(h_ref, nw_ref, wi_ref, wo_ref, fw_ref, o_ref, *, final):
    f32 = jnp.float32
    h = h_ref[...]
    hn = _bf(_rms(h, nw_ref[...]))
    blk = FFN_HIDDEN // FFN_SPLIT
    out = h
    for j in range(FFN_SPLIT):
        a = jnp.dot(hn, wi_ref[:, j * blk:(j + 1) * blk], preferred_element_type=f32)
        b = jnp.dot(hn, wi_ref[:, FFN_HIDDEN + j * blk:FFN_HIDDEN + (j + 1) * blk], preferred_element_type=f32)
        out = out + jnp.dot(_bf(a * _sigmoid(a) * b), wo_ref[j * blk:(j + 1) * blk, :], preferred_element_type=f32)
    if final:
        out = _rms(out, fw_ref[...])
    o_ref[...] = out


def _ffn(h, norm_w, w_ffn_in, w_ffn_out, final_w, l, final):
    n = h.shape[0]
    tm = ROW_TILE
    row = pl.BlockSpec((tm, D_MODEL), lambda r: (r, 0))
    est = (3 * D_MODEL * FFN_HIDDEN * 2 + 4 * tm * D_MODEL * 4 + 6 * tm * (FFN_HIDDEN // FFN_SPLIT) * 4
           + 4 * tm * D_MODEL * 4)
    return pl.pallas_call(
        functools.partial(_ffn_kernel, final=final),
        grid=(n // tm,),
        in_specs=[row, _layer_spec((1, D_MODEL), l), _layer_spec((D_MODEL, 2 * FFN_HIDDEN), l),
                  _layer_spec((FFN_HIDDEN, D_MODEL), l),
                  pl.BlockSpec((1, D_MODEL), lambda r: (0, 0))],
        out_specs=row,
        out_shape=jax.ShapeDtypeStruct((n, D_MODEL), jnp.float32),
        compiler_params=_params(("parallel",), est),
        name="ffn",
    )(h, norm_w, w_ffn_in, w_ffn_out, final_w)


def _pack_w_in(w_in):
    offs, acc = [], 0
    for s in (SSD_INNER, SSD_CONV_DIM, SSD_HEADS, ATT_WIDTH, ATT_KV_HEADS * ATT_HEAD_DIM,
              ATT_KV_HEADS * ATT_HEAD_DIM, IDX_HEADS * IDX_HEAD_DIM, IDX_HEAD_DIM, IDX_HEADS, POOL_WIDTH):
        acc += s
        offs.append(acc)
    z, xbc, dt, q, k, v, qi, ki, wi, u, gates = jnp.split(w_in, offs, axis=-1)
    dh = ATT_HEAD_DIM
    kv = [t[..., g * dh:(g + 1) * dh] for t in (k, v) for g in range(ATT_KV_HEADS) for _ in range(2)]
    pad = jnp.zeros(w_in.shape[:-1] + (SEG_MISC - SSD_HEADS - IDX_HEADS,), w_in.dtype)
    return _bf(jnp.concatenate([z, xbc, q] + kv + [qi, ki, ki, dt, wi, pad, u, gates], axis=-1))


def _pad_lanes(a):
    depth, w = a.shape
    return jnp.pad(a, ((0, 0), (0, LANES - w))).reshape(depth, 1, LANES)


def kernel(x, norm1_w, w_in, conv_w, conv_b, dt_bias, a_log, d_skip, ssd_norm_w, pool_w, pool_scale, w_up_ssd,
           w_up_attn, w_up_pool, w_out, norm2_w, w_ffn_in, w_ffn_out, final_norm_w):
    batch, seq, d = x.shape
    assert d == D_MODEL and seq % KEY_CHUNK == 0 and seq % ROW_TILE == 0 and (batch * seq) % ROW_TILE == 0
    depth = w_in.shape[0]
    row3 = lambda a: a.reshape(depth, 1, a.shape[-1])

    wcat = _pack_w_in(w_in)
    norm1 = row3(norm1_w)
    conv_b3 = row3(conv_b)
    dtb = _pad_lanes(dt_bias)
    alog = _pad_lanes(a_log)
    dskip_x = row3(jnp.repeat(d_skip, SSD_HEAD_DIM, axis=-1))
    ssd_nw = row3(ssd_norm_w)
    pool_s = row3(pool_scale)
    norm2 = row3(norm2_w)
    final_w = final_norm_w.reshape(1, D_MODEL)
    pool_wb, wus, wua, wup, wo, wfi, wfo = map(_bf, (pool_w, w_up_ssd, w_up_attn, w_up_pool, w_out, w_ffn_in,
                                                     w_ffn_out))

    h = x.reshape(batch * seq, D_MODEL)
    for l in range(depth):
        z, xbc, q, kv, qi, ki, misc, u, gates = _inproj(h, norm1, wcat, l)
        y_ssd = _ssd(xbc, z, misc, conv_w, conv_b3, dtb, alog, dskip_x, ssd_nw, l, batch, seq)
        y_att = _dsa(q, qi, misc, kv, ki, batch, seq)
        h = _merge(y_ssd, y_att, u, gates, h, pool_wb, pool_s, wus, wua, wup, wo, l, seq)
        h = _ffn(h, norm2, wfi, wfo, final_w, l, final=(l == depth - 1))
    return h.reshape(batch, seq, D_MODEL)
```

```python
import functools

import jax
import jax.numpy as jnp
from jax import lax
from jax.experimental import pallas as pl
from jax.experimental.pallas import tpu as pltpu

D_MODEL = 1024
DEPTH = 4
SSD_HEADS = 16
SSD_HEAD_DIM = 64
SSD_INNER = SSD_HEADS * SSD_HEAD_DIM
SSD_GROUPS = 2
SSD_STATE = 128
SSD_CONV_DIM = SSD_INNER + 2 * SSD_GROUPS * SSD_STATE
CONV_WIDTH = 4
SSD_CHUNK = 128
ATT_HEADS = 8
ATT_KV_HEADS = 2
ATT_HEAD_DIM = 64
ATT_WIDTH = ATT_HEADS * ATT_HEAD_DIM
IDX_HEADS = 4
IDX_HEAD_DIM = 64
TOPK_MAX = 256
POOL_WINDOWS = (2, 4, 8, 16)
POOL_GROUPS = 4
POOL_GROUP_DIM = 128
POOL_WIDTH = POOL_GROUPS * POOL_GROUP_DIM
N_BRANCH = 3
FFN_HIDDEN = 2816
EPS = 1e-6
ATT_SCALE = ATT_HEAD_DIM ** -0.5
IDX_SCALE = (IDX_HEADS ** -0.5) * (IDX_HEAD_DIM ** -0.5)

LANES = 128
SUBLANES = 8
VMEM_BUDGET_BYTES = 60000 * 1024

ROW_TILE = 512
Q_TILE = LANES
KEY_CHUNK = 512
POOL_HALO = 16
FFN_SPLIT = 2
COUNT_ROWS = 128

SEG_Z = SSD_INNER
SEG_XBC = SSD_CONV_DIM
SEG_Q = ATT_WIDTH
SEG_KV = 2 * ATT_KV_HEADS * 2 * ATT_HEAD_DIM
SEG_QI = IDX_HEADS * IDX_HEAD_DIM
SEG_KI = 2 * IDX_HEAD_DIM
SEG_MISC = LANES
SEG_U = POOL_WIDTH
SEG_G = N_BRANCH * D_MODEL
SEGS = (SEG_Z, SEG_XBC, SEG_Q, SEG_KV, SEG_QI, SEG_KI, SEG_MISC, SEG_U, SEG_G)
W_CAT = sum(SEGS)
MISC_DT = 0
MISC_WI = SSD_HEADS

INT_MIN = -(2 ** 31)
I16_MIN = -(2 ** 15)
I16_MAX = 2 ** 15 - 1
NEG_BIG = -1e30

_NT = (((1,), (1,)), ((), ()))


def _bf(x):
    return x.astype(jnp.bfloat16)


def _sigmoid(x):
    return 1.0 / (1.0 + jnp.exp(-x))


def _rms(x, w):
    return x * lax.rsqrt(jnp.mean(x * x, axis=-1, keepdims=True) + EPS) * w


def _params(sem, est_bytes):
    return pltpu.CompilerParams(dimension_semantics=sem,
                                vmem_limit_bytes=int(min(VMEM_BUDGET_BYTES, est_bytes)))


def _layer_spec(shape, l):
    nd = len(shape)
    return pl.BlockSpec((None,) + tuple(shape), lambda *_: (l,) + (0,) * nd, pipeline_mode=pl.Buffered(1))


def _inproj_kernel(h_ref, nw_ref, w_ref, *out_refs):
    xn = _bf(_rms(h_ref[...], nw_ref[...]))
    off = 0
    for ref, width in zip(out_refs, SEGS):
        r = jnp.dot(xn, w_ref[:, off:off + width], preferred_element_type=jnp.float32)
        ref[...] = r.astype(ref.dtype)
        off += width


def _inproj(h, norm_w, wcat, l):
    n = h.shape[0]
    tm = ROW_TILE
    dts = (jnp.float32, jnp.float32, jnp.bfloat16, jnp.bfloat16, jnp.bfloat16, jnp.bfloat16,
           jnp.float32, jnp.float32, jnp.float32)
    out_shape = tuple(jax.ShapeDtypeStruct((n, w), dt) for w, dt in zip(SEGS, dts))
    out_specs = tuple(pl.BlockSpec((tm, w), lambda i: (i, 0)) for w in SEGS)
    out_bytes = sum(tm * w * jnp.dtype(dt).itemsize for w, dt in zip(SEGS, dts))
    est = D_MODEL * W_CAT * 2 + 2 * tm * D_MODEL * 4 + 2 * out_bytes + 3 * tm * SEG_G * 4
    return pl.pallas_call(
        _inproj_kernel,
        grid=(n // tm,),
        in_specs=[pl.BlockSpec((tm, D_MODEL), lambda i: (i, 0)),
                  _layer_spec((1, D_MODEL), l),
                  _layer_spec((D_MODEL, W_CAT), l)],
        out_specs=out_specs,
        out_shape=out_shape,
        compiler_params=_params(("parallel",), est),
        name="inproj",
    )(h, norm_w, wcat)


def _col(a, j, width=LANES):
    return jnp.broadcast_to(a[:, j:j + 1], (a.shape[0], width))


def _expand_heads(a, left):
    return jnp.concatenate(
        [jnp.where(left, _col(a, 2 * m), _col(a, 2 * m + 1)) for m in range(SSD_HEADS // 2)], axis=1)


def _ssd_kernel(xbc_ref, z_ref, misc_ref, cw_ref, cb_ref, dtb_ref, alog_ref, dskip_ref, nw_ref, o_ref,
                cbuf, state):
    L = SSD_CHUNK
    f32 = jnp.float32

    @pl.when(pl.program_id(1) == 0)
    def _():
        cbuf[0:SUBLANES, :] = jnp.zeros((SUBLANES, SSD_CONV_DIM), f32)
        state[...] = jnp.zeros(state.shape, f32)

    cbuf[SUBLANES:SUBLANES + L, :] = xbc_ref[...]
    conv = cb_ref[...]
    for k in range(CONV_WIDTH):
        start = SUBLANES - (CONV_WIDTH - 1) + k
        conv = conv + cw_ref[k:k + 1, :] * cbuf[start:start + L, :]
    cbuf[0:SUBLANES, :] = cbuf[L:L + SUBLANES, :]
    xbc = conv * _sigmoid(conv)
    xs = xbc[:, :SSD_INNER]

    lane = lax.broadcasted_iota(jnp.int32, (L, LANES), 1)
    row = lax.broadcasted_iota(jnp.int32, (L, LANES), 0)
    left = lane < SSD_HEAD_DIM
    causal = lane <= row
    tri = jnp.where(causal, 1.0, 0.0)

    dtr = misc_ref[...] + dtb_ref[...]
    dt = jnp.maximum(dtr, 0.0) + jnp.log1p(jnp.exp(-jnp.abs(dtr)))
    a = dt * (-jnp.exp(alog_ref[...]))
    a_cs = jnp.dot(tri, a, preferred_element_type=f32, precision=lax.Precision.HIGHEST)
    a_cs_t = a_cs.T
    ea = jnp.exp(a_cs)
    decay = jnp.exp(a_cs[L - 1:L, :] - a_cs)

    x_all = xs * _expand_heads(dt, left)
    ea_x = _expand_heads(ea, left)
    xd_all = x_all * _expand_heads(decay, left)

    gw = SSD_INNER // SSD_GROUPS
    y_parts = []
    for g in range(SSD_GROUPS):
        b_g = xbc[:, SSD_INNER + g * SSD_STATE:SSD_INNER + (g + 1) * SSD_STATE]
        c_off = SSD_INNER + SSD_GROUPS * SSD_STATE
        c_g = xbc[:, c_off + g * SSD_STATE:c_off + (g + 1) * SSD_STATE]
        cb = lax.dot_general(_bf(c_g), _bf(b_g), _NT, preferred_element_type=f32)
        st = state[g]
        y_off = jnp.dot(_bf(c_g), _bf(st), preferred_element_type=f32)
        new_st = jnp.dot(_bf(b_g.T), _bf(xd_all[:, g * gw:(g + 1) * gw]), preferred_element_type=f32)
        state[g] = st * ea_x[L - 1:L, g * gw:(g + 1) * gw] + new_st
        pairs = []
        for mm in range(SSD_HEADS // SSD_GROUPS // 2):
            m = g * (SSD_HEADS // SSD_GROUPS // 2) + mm
            lhs = []
            for hd in (2 * m, 2 * m + 1):
                seg = _col(a_cs, hd) - a_cs_t[hd:hd + 1, :]
                lhs.append(cb * jnp.where(causal, jnp.exp(seg), 0.0))
            xp = x_all[:, m * LANES:(m + 1) * LANES]
            rhs = jnp.concatenate([jnp.where(left, xp, 0.0), jnp.where(left, 0.0, xp)], axis=0)
            pairs.append(jnp.dot(_bf(jnp.concatenate(lhs, axis=1)), _bf(rhs), preferred_element_type=f32))
        y_parts.append(jnp.concatenate(pairs, axis=1) + y_off * ea_x[:, g * gw:(g + 1) * gw])
    y = jnp.concatenate(y_parts, axis=1) + dskip_ref[...] * xs
    z = z_ref[...]
    o_ref[...] = _bf(_rms(y * (z * _sigmoid(z)), nw_ref[...]))


def _ssd(xbc, z, misc, conv_w, conv_b, dtb, alog, dskip_x, norm_w, l, batch, seq):
    L = SSD_CHUNK
    nc = seq // L
    row_spec = lambda w: pl.BlockSpec((L, w), lambda b, c: (b * nc + c, 0))
    est = (2 * L * (SSD_CONV_DIM + SSD_INNER + LANES) * 4 + 2 * L * SSD_INNER * 2
           + (L + 2 * SUBLANES) * SSD_CONV_DIM * 4 + SSD_GROUPS * SSD_STATE * SSD_INNER // SSD_GROUPS * 4
           + 40 * L * SSD_INNER * 4)
    return pl.pallas_call(
        _ssd_kernel,
        grid=(batch, nc),
        in_specs=[row_spec(SSD_CONV_DIM), row_spec(SSD_INNER), row_spec(LANES),
                  _layer_spec((CONV_WIDTH, SSD_CONV_DIM), l), _layer_spec((1, SSD_CONV_DIM), l),
                  _layer_spec((1, LANES), l), _layer_spec((1, LANES), l),
                  _layer_spec((1, SSD_INNER), l), _layer_spec((1, SSD_INNER), l)],
        out_specs=row_spec(SSD_INNER),
        out_shape=jax.ShapeDtypeStruct((batch * seq, SSD_INNER), jnp.bfloat16),
        scratch_shapes=[pltpu.VMEM((L + 2 * SUBLANES, SSD_CONV_DIM), jnp.float32),
                        pltpu.VMEM((SSD_GROUPS, SSD_STATE, SSD_INNER // SSD_GROUPS), jnp.float32)],
        compiler_params=_params(("parallel", "arbitrary"), est),
        name="ssd",
    )(xbc, z, misc, conv_w, conv_b, dtb, alog, dskip_x, norm_w)


def _dsa_kernel(q_ref, qi_ref, misc_ref, kv_ref, ki_ref, o_ref, hi_ref, lo_ref, m_ref, acc_ref, *, topk, idx_bits):
    f32 = jnp.float32
    i32 = jnp.int32
    i16 = jnp.int16
    KC = KEY_CHUNK
    i = pl.program_id(1)
    nch = (i * Q_TILE + Q_TILE + KC - 1) // KC
    lane = lax.broadcasted_iota(i32, (1, LANES), 1)
    qpos = i * Q_TILE + lane
    left = lane < ATT_HEAD_DIM
    sub_iota = lax.broadcasted_iota(i32, (KC, LANES), 0)

    misc_t = misc_ref[...].T
    qi = qi_ref[...].astype(f32)
    qi_heads, w_heads = [], []
    for h in range(IDX_HEADS):
        qp = qi[:, (h // 2) * LANES:(h // 2 + 1) * LANES]
        qi_heads.append(_bf(jnp.where(left, qp, 0.0) if h % 2 == 0 else jnp.where(left, 0.0, qp)))
        w_heads.append(misc_t[MISC_WI + h:MISC_WI + h + 1, :] * IDX_SCALE)

    def score_body(c, carry):
        row0 = pl.multiple_of(c * KC, KC)
        ki_c = ki_ref[pl.ds(row0, KC), :]
        sc = jnp.zeros((KC, LANES), f32)
        for h in range(IDX_HEADS):
            s = lax.dot_general(ki_c, qi_heads[h], _NT, preferred_element_type=f32)
            sc = sc + jnp.maximum(s, 0.0) * w_heads[h]
        sc = jnp.where(sc == 0.0, 0.0, sc)
        bits = lax.bitcast_convert_type(sc, i32)
        key = bits ^ ((bits >> 31) & 0x7FFFFFFF)
        key = jnp.where(row0 + sub_iota <= qpos, key, INT_MIN)
        hi_ref[c] = (key >> 16).astype(i16)
        lo_ref[c] = ((key & 0xFFFF) + I16_MIN).astype(i16)
        return carry

    lax.fori_loop(0, nch, score_body, 0)

    one_bf = jnp.ones((), jnp.bfloat16)
    zero_bf = jnp.zeros((), jnp.bfloat16)

    def count(ref, pred):
        def body(c, acc):
            ind = jnp.where(pred(ref[c]), one_bf, zero_bf)
            parts = [ind[r:r + COUNT_ROWS] for r in range(0, KC, COUNT_ROWS)]
            while len(parts) > 1:
                parts = [a + b for a, b in zip(parts[::2], parts[1::2])]
            return acc + parts[0]
        acc = lax.fori_loop(0, nch, body, jnp.zeros((COUNT_ROWS, LANES), jnp.bfloat16))
        return jnp.sum(acc.astype(f32), axis=0, keepdims=True)

    kf = float(topk)

    def search16(ref, base):
        def body(it, t):
            cand = t + lax.shift_left(jnp.int32(1), 15 - it)
            cand16 = cand.astype(i16)
            return jnp.where(base + count(ref, lambda k: k >= cand16) >= kf, cand, t)
        return lax.fori_loop(0, 16, body, jnp.full((1, LANES), I16_MIN, i32))

    t_hi = search16(hi_ref, 0.0)
    t_hi16 = t_hi.astype(i16)
    above = count(hi_ref, lambda k: k > t_hi16)

    def lo_mask_body(c, carry):
        lo_ref[c] = jnp.where(hi_ref[c] == t_hi16, lo_ref[c], I16_MIN)
        return carry

    lax.fori_loop(0, nch, lo_mask_body, 0)
    t_lo = search16(lo_ref, above)
    t_lo16 = t_lo.astype(i16)

    def tie_body(c, carry):
        hi = hi_ref[c]
        lo = lo_ref[c]
        sidx = (c * KC + sub_iota).astype(i16)
        tie = jnp.where(hi == t_hi16, jnp.where(lo == t_lo16, sidx, I16_MAX), I16_MAX)
        lo_ref[c] = jnp.where(hi > t_hi16, -1, jnp.where(lo > t_lo16, -1, tie))
        return carry

    lax.fori_loop(0, nch, tie_body, 0)

    def idx_body(it, j0):
        cand = j0 + lax.shift_left(jnp.int32(1), idx_bits - 1 - it)
        cand16 = cand.astype(i16)
        return jnp.where(count(lo_ref, lambda k: k < cand16) < kf, cand, j0)

    j0 = lax.fori_loop(0, idx_bits, idx_body, jnp.zeros((1, LANES), i32))
    j0 = jnp.where(jnp.where(t_hi == I16_MIN, t_lo, 0) == I16_MIN, -1, j0)
    j0_16 = j0.astype(i16)

    q = q_ref[...].astype(f32)
    hpg = ATT_HEADS // ATT_KV_HEADS
    q_stacks = []
    for g in range(ATT_KV_HEADS):
        parts = []
        for hh in range(hpg):
            h = g * hpg + hh
            qp = q[:, (h // 2) * LANES:(h // 2 + 1) * LANES]
            parts.append(jnp.where(left, qp, 0.0) if h % 2 == 0 else jnp.where(left, 0.0, qp))
        q_stacks.append(_bf(jnp.concatenate(parts, axis=0) * ATT_SCALE))
    eye = _bf(jnp.where(lax.broadcasted_iota(i32, (LANES, LANES), 0) == lax.broadcasted_iota(i32, (LANES, LANES), 1),
                        1.0, 0.0))
    ones_hi = _bf(jnp.where(left, 0.0, 1.0))
    m_ref[...] = jnp.full(m_ref.shape, NEG_BIG, f32)
    acc_ref[...] = jnp.zeros(acc_ref.shape, f32)

    def att_body(c, carry):
        row0 = pl.multiple_of(c * KC, KC)
        sel_t = jnp.where(lo_ref[c] <= j0_16, one_bf, zero_bf)
        sel = lax.dot_general(eye, sel_t, _NT, preferred_element_type=f32)
        bias = pltpu.repeat((sel - 1.0) * (-NEG_BIG), hpg, axis=0)
        kv_c = kv_ref[pl.ds(row0, KC), :]
        for g in range(ATT_KV_HEADS):
            k2 = kv_c[:, g * LANES:(g + 1) * LANES]
            v1 = kv_c[:, (ATT_KV_HEADS + g) * LANES:(ATT_KV_HEADS + g + 1) * LANES] + ones_hi
            s = lax.dot_general(q_stacks[g], k2, _NT, preferred_element_type=f32) + bias
            m_prev = m_ref[g]
            m_new = jnp.maximum(m_prev, jnp.max(s, axis=1, keepdims=True))
            alpha = jnp.exp(m_prev - m_new)
            p = jnp.exp(s - pltpu.repeat(m_new, KC // LANES, axis=1))
            acc_ref[g] = alpha * acc_ref[g] + jnp.dot(_bf(p), v1, preferred_element_type=f32)
            m_ref[g] = m_new
        return carry

    lax.fori_loop(0, nch, att_body, 0)

    outs = []
    for g in range(ATT_KV_HEADS):
        acc = acc_ref[g]
        o = acc / pltpu.roll(acc, ATT_HEAD_DIM, axis=1)
        for pp in range(hpg // 2):
            odd = pltpu.roll(o[(2 * pp + 1) * Q_TILE:(2 * pp + 2) * Q_TILE], ATT_HEAD_DIM, axis=1)
            outs.append(jnp.where(left, o[(2 * pp) * Q_TILE:(2 * pp + 1) * Q_TILE], odd))
    o_ref[...] = _bf(jnp.concatenate(outs, axis=1))


def _dsa(q, qi, misc, kv, ki, batch, seq):
    nq = seq // Q_TILE
    nkc = seq // KEY_CHUNK
    topk = min(TOPK_MAX, seq // 4)
    idx_bits = max(1, (seq - 1).bit_length())
    hpg = ATT_HEADS // ATT_KV_HEADS
    qrow = lambda w: pl.BlockSpec((Q_TILE, w), lambda b, i: (b * nq + i, 0))
    seq_spec = lambda w: pl.BlockSpec((seq, w), lambda b, i: (b, 0))
    assert seq <= I16_MAX and nkc * (KEY_CHUNK // COUNT_ROWS) <= 256
    est = (2 * seq * (SEG_KV + SEG_KI) * 2 + 2 * nkc * KEY_CHUNK * LANES * 2
           + 2 * ATT_KV_HEADS * hpg * Q_TILE * LANES * 4 + 16 * hpg * Q_TILE * KEY_CHUNK * 4)
    return pl.pallas_call(
        functools.partial(_dsa_kernel, topk=topk, idx_bits=idx_bits),
        grid=(batch, nq),
        in_specs=[qrow(SEG_Q), qrow(SEG_QI), qrow(SEG_MISC), seq_spec(SEG_KV), seq_spec(SEG_KI)],
        out_specs=qrow(ATT_WIDTH),
        out_shape=jax.ShapeDtypeStruct((batch * seq, ATT_WIDTH), jnp.bfloat16),
        scratch_shapes=[pltpu.VMEM((nkc, KEY_CHUNK, LANES), jnp.int16),
                        pltpu.VMEM((nkc, KEY_CHUNK, LANES), jnp.int16),
                        pltpu.VMEM((ATT_KV_HEADS, hpg * Q_TILE, LANES), jnp.float32),
                        pltpu.VMEM((ATT_KV_HEADS, hpg * Q_TILE, LANES), jnp.float32)],
        compiler_params=_params(("parallel", "arbitrary"), est),
        name="dsa",
    )(q, qi, misc, kv, ki)


def _merge_kernel(yssd_ref, yatt_ref, u_ref, uh_ref, g_ref, h_ref, pw_ref, ps_ref, wus_ref, wua_ref, wup_ref,
                  wo_ref, o_ref, ubuf, *, tiles_per_seq):
    f32 = jnp.float32
    tm = u_ref.shape[0]
    it = pl.program_id(0) % tiles_per_seq
    ubuf[POOL_HALO:POOL_HALO + tm, :] = u_ref[...]
    ubuf[0:POOL_HALO, :] = jnp.where(it == 0, 0.0, uh_ref[...])
    t1 = (it * tm + 1 + lax.broadcasted_iota(jnp.int32, (tm, POOL_GROUP_DIM), 0)).astype(f32)
    mixed = []
    for gi, win in enumerate(POOL_WINDOWS):
        cols = slice(gi * POOL_GROUP_DIM, (gi + 1) * POOL_GROUP_DIM)
        cur = ubuf[POOL_HALO:POOL_HALO + tm, cols]
        s = cur
        for d in range(1, win):
            s = s + ubuf[POOL_HALO - d:POOL_HALO - d + tm, cols]
        pooled = s / jnp.minimum(t1, float(win)) - cur
        mixed.append(jnp.dot(_bf(pooled), pw_ref[gi], preferred_element_type=f32))
    ypool = _bf(jnp.concatenate(mixed, axis=1) * ps_ref[...])

    g = g_ref[...]
    merged = (_sigmoid(g[:, :D_MODEL]) * jnp.dot(yssd_ref[...], wus_ref[...], preferred_element_type=f32)
              + _sigmoid(g[:, D_MODEL:2 * D_MODEL]) * jnp.dot(yatt_ref[...], wua_ref[...], preferred_element_type=f32)
              + _sigmoid(g[:, 2 * D_MODEL:]) * jnp.dot(ypool, wup_ref[...], preferred_element_type=f32))
    o_ref[...] = h_ref[...] + jnp.dot(_bf(merged), wo_ref[...], preferred_element_type=f32)


def _merge(y_ssd, y_att, u, gates, h, pool_w, pool_scale, w_up_ssd, w_up_attn, w_up_pool, w_out, l, seq):
    n = h.shape[0]
    tm = ROW_TILE
    tiles_per_seq = seq // tm
    row = lambda w: pl.BlockSpec((tm, w), lambda r: (r, 0))
    halo = pl.BlockSpec((POOL_HALO, POOL_WIDTH), lambda r: (jnp.maximum(r * (tm // POOL_HALO) - 1, 0), 0))
    w_bytes = (POOL_GROUPS * POOL_GROUP_DIM ** 2 + (SSD_INNER + ATT_WIDTH + POOL_WIDTH + D_MODEL) * D_MODEL) * 2
    est = (w_bytes + 2 * tm * ((SSD_INNER + ATT_WIDTH) * 2 + (POOL_WIDTH + SEG_G + 2 * D_MODEL) * 4)
           + (tm + POOL_HALO) * POOL_WIDTH * 4 + 12 * tm * D_MODEL * 4)
    return pl.pallas_call(
        functools.partial(_merge_kernel, tiles_per_seq=tiles_per_seq),
        grid=(n // tm,),
        in_specs=[row(SSD_INNER), row(ATT_WIDTH), row(POOL_WIDTH), halo, row(SEG_G), row(D_MODEL),
                  _layer_spec((POOL_GROUPS, POOL_GROUP_DIM, POOL_GROUP_DIM), l), _layer_spec((1, POOL_WIDTH), l),
                  _layer_spec((SSD_INNER, D_MODEL), l), _layer_spec((ATT_WIDTH, D_MODEL), l),
                  _layer_spec((POOL_WIDTH, D_MODEL), l), _layer_spec((D_MODEL, D_MODEL), l)],
        out_specs=row(D_MODEL),
        out_shape=jax.ShapeDtypeStruct((n, D_MODEL), jnp.float32),
        scratch_shapes=[pltpu.VMEM((tm + POOL_HALO, POOL_WIDTH), jnp.float32)],
        compiler_params=_params(("parallel",), est),
        name="merge",
    )(y_ssd, y_att, u, u, gates, h, pool_w, pool_scale, w_up_ssd, w_up_attn, w_up_pool, w_out)


def _ffn_kernel(h_ref, nw_ref, wi_ref, wo_ref, fw_ref, o_ref, *, final):
    f32 = jnp.float32
    h = h_ref[...]
    hn = _bf(_rms(h, nw_ref[...]))
    blk = FFN_HIDDEN // FFN_SPLIT
    out = h
    for j in range(FFN_SPLIT):
        a = jnp.dot(hn, wi_ref[:, j * blk:(j + 1) * blk], preferred_element_type=f32)
        b = jnp.dot(hn, wi_ref[:, FFN_HIDDEN + j * blk:FFN_HIDDEN + (j + 1) * blk], preferred_element_type=f32)
        out = out + jnp.dot(_bf(a * _sigmoid(a) * b), wo_ref[j * blk:(j + 1) * blk, :], preferred_element_type=f32)
    if final:
        out = _rms(out, fw_ref[...])
    o_ref[...] = out


def _ffn(h, norm_w, w_ffn_in, w_ffn_out, final_w, l, final):
    n = h.shape[0]
    tm = ROW_TILE
    row = pl.BlockSpec((tm, D_MODEL), lambda r: (r, 0))
    est = (3 * D_MODEL * FFN_HIDDEN * 2 + 4 * tm * D_MODEL * 4 + 6 * tm * (FFN_HIDDEN // FFN_SPLIT) * 4
           + 4 * tm * D_MODEL * 4)
    return pl.pallas_call(
        functools.partial(_ffn_kernel, final=final),
        grid=(n // tm,),
        in_specs=[row, _layer_spec((1, D_MODEL), l), _layer_spec((D_MODEL, 2 * FFN_HIDDEN), l),
                  _layer_spec((FFN_HIDDEN, D_MODEL), l),
                  pl.BlockSpec((1, D_MODEL), lambda r: (0, 0))],
        out_specs=row,
        out_shape=jax.ShapeDtypeStruct((n, D_MODEL), jnp.float32),
        compiler_params=_params(("parallel",), est),
        name="ffn",
    )(h, norm_w, w_ffn_in, w_ffn_out, final_w)


def _pack_w_in(w_in):
    offs, acc = [], 0
    for s in (SSD_INNER, SSD_CONV_DIM, SSD_HEADS, ATT_WIDTH, ATT_KV_HEADS * ATT_HEAD_DIM,
              ATT_KV_HEADS * ATT_HEAD_DIM, IDX_HEADS * IDX_HEAD_DIM, IDX_HEAD_DIM, IDX_HEADS, POOL_WIDTH):
        acc += s
        offs.append(acc)
    z, xbc, dt, q, k, v, qi, ki, wi, u, gates = jnp.split(w_in, offs, axis=-1)
    dh = ATT_HEAD_DIM
    kv = [k[..., g * dh:(g + 1) * dh] for g in range(ATT_KV_HEADS) for _ in range(2)]
    for g in range(ATT_KV_HEADS):
        kv += [v[..., g * dh:(g + 1) * dh], jnp.zeros_like(v[..., :dh])]
    pad = jnp.zeros(w_in.shape[:-1] + (SEG_MISC - SSD_HEADS - IDX_HEADS,), w_in.dtype)
    return _bf(jnp.concatenate([z, xbc, q] + kv + [qi, ki, ki, dt, wi, pad, u, gates], axis=-1))


def _pad_lanes(a):
    depth, w = a.shape
    return jnp.pad(a, ((0, 0), (0, LANES - w))).reshape(depth, 1, LANES)


def kernel(x, norm1_w, w_in, conv_w, conv_b, dt_bias, a_log, d_skip, ssd_norm_w, pool_w, pool_scale, w_up_ssd,
           w_up_attn, w_up_pool, w_out, norm2_w, w_ffn_in, w_ffn_out, final_norm_w):
    batch, seq, d = x.shape
    assert d == D_MODEL and seq % KEY_CHUNK == 0 and seq % ROW_TILE == 0 and (batch * seq) % ROW_TILE == 0
    depth = w_in.shape[0]
    row3 = lambda a: a.reshape(depth, 1, a.shape[-1])

    wcat = _pack_w_in(w_in)
    norm1 = row3(norm1_w)
    conv_b3 = row3(conv_b)
    dtb = _pad_lanes(dt_bias)
    alog = _pad_lanes(a_log)
    dskip_x = row3(jnp.repeat(d_skip, SSD_HEAD_DIM, axis=-1))
    ssd_nw = row3(ssd_norm_w)
    pool_s = row3(pool_scale)
    norm2 = row3(norm2_w)
    final_w = final_norm_w.reshape(1, D_MODEL)
    pool_wb, wus, wua, wup, wo, wfi, wfo = map(_bf, (pool_w, w_up_ssd, w_up_attn, w_up_pool, w_out, w_ffn_in,
                                                     w_ffn_out))

    h = x.reshape(batch * seq, D_MODEL)
    for l in range(depth):
        z, xbc, q, kv, qi, ki, misc, u, gates = _inproj(h, norm1, wcat, l)
        y_ssd = _ssd(xbc, z, misc, conv_w, conv_b3, dtb, alog, dskip_x, ssd_nw, l, batch, seq)
        y_att = _dsa(q, qi, misc, kv, ki, batch, seq)
        h = _merge(y_ssd, y_att, u, gates, h, pool_wb, pool_s, wus, wua, wup, wo, l, seq)
        h = _ffn(h, norm2, wfi, wfo, final_w, l, final=(l == depth - 1))
    return h.reshape(batch, seq, D_MODEL)
```

```python
import functools

import jax
import jax.numpy as jnp
from jax import lax
from jax.experimental import pallas as pl
from jax.experimental.pallas import tpu as pltpu

D_MODEL = 1024
DEPTH = 4
SSD_HEADS = 16
SSD_HEAD_DIM = 64
SSD_INNER = SSD_HEADS * SSD_HEAD_DIM
SSD_GROUPS = 2
SSD_STATE = 128
SSD_CONV_DIM = SSD_INNER + 2 * SSD_GROUPS * SSD_STATE
CONV_WIDTH = 4
SSD_CHUNK = 128
ATT_HEADS = 8
ATT_KV_HEADS = 2
ATT_HEAD_DIM = 64
ATT_WIDTH = ATT_HEADS * ATT_HEAD_DIM
IDX_HEADS = 4
IDX_HEAD_DIM = 64
TOPK_MAX = 256
POOL_WINDOWS = (2, 4, 8, 16)
POOL_GROUPS = 4
POOL_GROUP_DIM = 128
POOL_WIDTH = POOL_GROUPS * POOL_GROUP_DIM
N_BRANCH = 3
FFN_HIDDEN = 2816
EPS = 1e-6
ATT_SCALE = ATT_HEAD_DIM ** -0.5
IDX_SCALE = (IDX_HEADS ** -0.5) * (IDX_HEAD_DIM ** -0.5)

LANES = 128
SUBLANES = 8
VMEM_BUDGET_BYTES = 60000 * 1024

ROW_TILE = 512
Q_TILE = LANES
KEY_CHUNK = 512
POOL_HALO = 16
FFN_SPLIT = 2
COUNT_WAYS = 8
TIE_BLOCK = 256

SEG_Z = SSD_INNER
SEG_XBC = SSD_CONV_DIM
SEG_Q = ATT_WIDTH
SEG_KV = 2 * ATT_KV_HEADS * 2 * ATT_HEAD_DIM
SEG_QI = IDX_HEADS * IDX_HEAD_DIM
SEG_KI = 2 * IDX_HEAD_DIM
SEG_MISC = LANES
SEG_U = POOL_WIDTH
SEG_G = N_BRANCH * D_MODEL
SEGS = (SEG_Z, SEG_XBC, SEG_Q, SEG_KV, SEG_QI, SEG_KI, SEG_MISC, SEG_U, SEG_G)
W_CAT = sum(SEGS)
MISC_DT = 0
MISC_WI = SSD_HEADS

INT_MIN = -(2 ** 31)
NEG_BIG = -1e30

_NT = (((1,), (1,)), ((), ()))


def _bf(x):
    return x.astype(jnp.bfloat16)


def _sigmoid(x):
    return 1.0 / (1.0 + jnp.exp(-x))


def _rms(x, w):
    return x * lax.rsqrt(jnp.mean(x * x, axis=-1, keepdims=True) + EPS) * w


def _params(sem, est_bytes):
    return pltpu.CompilerParams(dimension_semantics=sem,
                                vmem_limit_bytes=int(min(VMEM_BUDGET_BYTES, est_bytes)))


def _layer_spec(shape, l):
    nd = len(shape)
    return pl.BlockSpec((None,) + tuple(shape), lambda *_: (l,) + (0,) * nd, pipeline_mode=pl.Buffered(1))


def _inproj_kernel(h_ref, nw_ref, w_ref, *out_refs):
    xn = _bf(_rms(h_ref[...], nw_ref[...]))
    off = 0
    for ref, width in zip(out_refs, SEGS):
        r = jnp.dot(xn, w_ref[:, off:off + width], preferred_element_type=jnp.float32)
        ref[...] = r.astype(ref.dtype)
        off += width


def _inproj(h, norm_w, wcat, l):
    n = h.shape[0]
    tm = ROW_TILE
    dts = (jnp.float32, jnp.float32, jnp.bfloat16, jnp.bfloat16, jnp.bfloat16, jnp.bfloat16,
           jnp.float32, jnp.float32, jnp.float32)
    out_shape = tuple(jax.ShapeDtypeStruct((n, w), dt) for w, dt in zip(SEGS, dts))
    out_specs = tuple(pl.BlockSpec((tm, w), lambda i: (i, 0)) for w in SEGS)
    out_bytes = sum(tm * w * jnp.dtype(dt).itemsize for w, dt in zip(SEGS, dts))
    est = D_MODEL * W_CAT * 2 + 2 * tm * D_MODEL * 4 + 2 * out_bytes + 3 * tm * SEG_G * 4
    return pl.pallas_call(
        _inproj_kernel,
        grid=(n // tm,),
        in_specs=[pl.BlockSpec((tm, D_MODEL), lambda i: (i, 0)),
                  _layer_spec((1, D_MODEL), l),
                  _layer_spec((D_MODEL, W_CAT), l)],
        out_specs=out_specs,
        out_shape=out_shape,
        compiler_params=_params(("parallel",), est),
        name="inproj",
    )(h, norm_w, wcat)


def _col(a, j, width=LANES):
    return jnp.broadcast_to(a[:, j:j + 1], (a.shape[0], width))


def _expand_heads(a, left):
    return jnp.concatenate(
        [jnp.where(left, _col(a, 2 * m), _col(a, 2 * m + 1)) for m in range(SSD_HEADS // 2)], axis=1)


def _ssd_kernel(xbc_ref, z_ref, misc_ref, cw_ref, cb_ref, dtb_ref, alog_ref, dskip_ref, nw_ref, o_ref,
                cbuf, state):
    L = SSD_CHUNK
    f32 = jnp.float32

    @pl.when(pl.program_id(1) == 0)
    def _():
        cbuf[0:SUBLANES, :] = jnp.zeros((SUBLANES, SSD_CONV_DIM), f32)
        state[...] = jnp.zeros(state.shape, f32)

    cbuf[SUBLANES:SUBLANES + L, :] = xbc_ref[...]
    conv = cb_ref[...]
    for k in range(CONV_WIDTH):
        start = SUBLANES - (CONV_WIDTH - 1) + k
        conv = conv + cw_ref[k:k + 1, :] * cbuf[start:start + L, :]
    cbuf[0:SUBLANES, :] = cbuf[L:L + SUBLANES, :]
    xbc = conv * _sigmoid(conv)
    xs = xbc[:, :SSD_INNER]

    lane = lax.broadcasted_iota(jnp.int32, (L, LANES), 1)
    row = lax.broadcasted_iota(jnp.int32, (L, LANES), 0)
    left = lane < SSD_HEAD_DIM
    causal = lane <= row
    tri = jnp.where(causal, 1.0, 0.0)

    dtr = misc_ref[...] + dtb_ref[...]
    dt = jnp.maximum(dtr, 0.0) + jnp.log1p(jnp.exp(-jnp.abs(dtr)))
    a = dt * (-jnp.exp(alog_ref[...]))
    a_cs = jnp.dot(tri, a, preferred_element_type=f32, precision=lax.Precision.HIGHEST)
    a_cs_t = a_cs.T
    ea = jnp.exp(a_cs)
    decay = jnp.exp(a_cs[L - 1:L, :] - a_cs)

    x_all = xs * _expand_heads(dt, left)
    ea_x = _expand_heads(ea, left)
    xd_all = x_all * _expand_heads(decay, left)

    gw = SSD_INNER // SSD_GROUPS
    y_parts = []
    for g in range(SSD_GROUPS):
        b_g = xbc[:, SSD_INNER + g * SSD_STATE:SSD_INNER + (g + 1) * SSD_STATE]
        c_off = SSD_INNER + SSD_GROUPS * SSD_STATE
        c_g = xbc[:, c_off + g * SSD_STATE:c_off + (g + 1) * SSD_STATE]
        cb = lax.dot_general(_bf(c_g), _bf(b_g), _NT, preferred_element_type=f32)
        st = state[g]
        y_off = jnp.dot(_bf(c_g), _bf(st), preferred_element_type=f32)
        new_st = jnp.dot(_bf(b_g.T), _bf(xd_all[:, g * gw:(g + 1) * gw]), preferred_element_type=f32)
        state[g] = st * ea_x[L - 1:L, g * gw:(g + 1) * gw] + new_st
        pairs = []
        for mm in range(SSD_HEADS // SSD_GROUPS // 2):
            m = g * (SSD_HEADS // SSD_GROUPS // 2) + mm
            lhs = []
            for hd in (2 * m, 2 * m + 1):
                seg = _col(a_cs, hd) - a_cs_t[hd:hd + 1, :]
                lhs.append(cb * jnp.where(causal, jnp.exp(seg), 0.0))
            xp = x_all[:, m * LANES:(m + 1) * LANES]
            rhs = jnp.concatenate([jnp.where(left, xp, 0.0), jnp.where(left, 0.0, xp)], axis=0)
            pairs.append(jnp.dot(_bf(jnp.concatenate(lhs, axis=1)), _bf(rhs), preferred_element_type=f32))
        y_parts.append(jnp.concatenate(pairs, axis=1) + y_off * ea_x[:, g * gw:(g + 1) * gw])
    y = jnp.concatenate(y_parts, axis=1) + dskip_ref[...] * xs
    z = z_ref[...]
    o_ref[...] = _bf(_rms(y * (z * _sigmoid(z)), nw_ref[...]))


def _ssd(xbc, z, misc, conv_w, conv_b, dtb, alog, dskip_x, norm_w, l, batch, seq):
    L = SSD_CHUNK
    nc = seq // L
    row_spec = lambda w: pl.BlockSpec((L, w), lambda b, c: (b * nc + c, 0))
    est = (2 * L * (SSD_CONV_DIM + SSD_INNER + LANES) * 4 + 2 * L * SSD_INNER * 2
           + (L + 2 * SUBLANES) * SSD_CONV_DIM * 4 + SSD_GROUPS * SSD_STATE * SSD_INNER // SSD_GROUPS * 4
           + 40 * L * SSD_INNER * 4)
    return pl.pallas_call(
        _ssd_kernel,
        grid=(batch, nc),
        in_specs=[row_spec(SSD_CONV_DIM), row_spec(SSD_INNER), row_spec(LANES),
                  _layer_spec((CONV_WIDTH, SSD_CONV_DIM), l), _layer_spec((1, SSD_CONV_DIM), l),
                  _layer_spec((1, LANES), l), _layer_spec((1, LANES), l),
                  _layer_spec((1, SSD_INNER), l), _layer_spec((1, SSD_INNER), l)],
        out_specs=row_spec(SSD_INNER),
        out_shape=jax.ShapeDtypeStruct((batch * seq, SSD_INNER), jnp.bfloat16),
        scratch_shapes=[pltpu.VMEM((L + 2 * SUBLANES, SSD_CONV_DIM), jnp.float32),
                        pltpu.VMEM((SSD_GROUPS, SSD_STATE, SSD_INNER // SSD_GROUPS), jnp.float32)],
        compiler_params=_params(("parallel", "arbitrary"), est),
        name="ssd",
    )(xbc, z, misc, conv_w, conv_b, dtb, alog, dskip_x, norm_w)


def _dsa_kernel(q_ref, qi_ref, misc_ref, kv_ref, ki_ref, tri_ref, o_ref, key_ref, sel_ref, s_ref, m_ref, acc_ref, *,
                topk):
    f32 = jnp.float32
    i32 = jnp.int32
    KC = KEY_CHUNK
    i = pl.program_id(1)
    nch = (i * Q_TILE + Q_TILE + KC - 1) // KC
    lane = lax.broadcasted_iota(i32, (1, LANES), 1)
    qpos = i * Q_TILE + lane
    left = lane < ATT_HEAD_DIM
    sub_iota = lax.broadcasted_iota(i32, (KC, LANES), 0)

    misc_t = misc_ref[...].T
    qi = qi_ref[...].astype(f32)
    qi_heads, w_heads = [], []
    for h in range(IDX_HEADS):
        qp = qi[:, (h // 2) * LANES:(h // 2 + 1) * LANES]
        qi_heads.append(_bf(jnp.where(left, qp, 0.0) if h % 2 == 0 else jnp.where(left, 0.0, qp)))
        w_heads.append(misc_t[MISC_WI + h:MISC_WI + h + 1, :] * IDX_SCALE)

    def score_body(c, carry):
        row0 = pl.multiple_of(c * KC, KC)
        ki_c = ki_ref[pl.ds(row0, KC), :]
        sc = jnp.zeros((KC, LANES), f32)
        for h in range(IDX_HEADS):
            s = lax.dot_general(ki_c, qi_heads[h], _NT, preferred_element_type=f32)
            sc = sc + jnp.maximum(s, 0.0) * w_heads[h]
        sc = jnp.where(sc == 0.0, 0.0, sc)
        bits = lax.bitcast_convert_type(sc, i32)
        key = bits ^ ((bits >> 31) & 0x7FFFFFFF)
        key_ref[c] = jnp.where(row0 + sub_iota <= qpos, key, INT_MIN)
        return carry

    lax.fori_loop(0, nch, score_body, 0)

    def count(pred):
        rows = COUNT_WAYS * SUBLANES

        def body(c, acc):
            ind = jnp.where(pred(key_ref[c]), 1.0, 0.0)
            return acc + jnp.sum(ind.reshape(KC // rows, rows, LANES), axis=0)
        acc = lax.fori_loop(0, nch, body, jnp.zeros((rows, LANES), f32))
        return jnp.sum(acc, axis=0, keepdims=True)

    kf = float(topk)

    def bit_body(it, t):
        cand = t + lax.shift_left(jnp.int32(1), 31 - it)
        return jnp.where(count(lambda k: k >= cand) >= kf, cand, t)

    t = lax.fori_loop(0, 32, bit_body, jnp.full((1, LANES), INT_MIN, i32))

    need = kf - count(lambda k: k > t)
    need = jnp.where(t == INT_MIN, 0.0, need)
    tri = tri_ref[...]
    tb = tri.shape[0]

    def tie_body(c, seen):
        k = key_ref[c]
        tie = jnp.where(k == t, 1.0, 0.0)
        picks = []
        for r in range(0, KC, tb):
            rank = seen + jnp.dot(tri, _bf(tie[r:r + tb]), preferred_element_type=f32)
            picks.append(jnp.where(rank <= need, tie[r:r + tb], 0.0))
            seen = rank[tb - 1:tb]
        sel_ref[c] = _bf(jnp.where(k > t, 1.0, jnp.concatenate(picks, axis=0)))
        return seen

    lax.fori_loop(0, nch, tie_body, jnp.zeros((1, LANES), f32))

    q = q_ref[...].astype(f32)
    hpg = ATT_HEADS // ATT_KV_HEADS
    q_stacks = []
    for g in range(ATT_KV_HEADS):
        parts = []
        for hh in range(hpg):
            h = g * hpg + hh
            qp = q[:, (h // 2) * LANES:(h // 2 + 1) * LANES]
            parts.append(jnp.where(left, qp, 0.0) if h % 2 == 0 else jnp.where(left, 0.0, qp))
        q_stacks.append(_bf(jnp.concatenate(parts, axis=0) * ATT_SCALE))
    eye = _bf(jnp.where(lax.broadcasted_iota(i32, (LANES, LANES), 0) == lax.broadcasted_iota(i32, (LANES, LANES), 1),
                        1.0, 0.0))
    ones_hi = _bf(jnp.where(left, 0.0, 1.0))
    m_ref[...] = jnp.full(m_ref.shape, NEG_BIG, f32)
    acc_ref[...] = jnp.zeros(acc_ref.shape, f32)

    def chunk_start(c):
        return c * KC if isinstance(c, int) else pl.multiple_of(c * KC, KC)

    def logits(c, slot):
        row0 = chunk_start(c)
        sel = lax.dot_general(eye, sel_ref[c], _NT, preferred_element_type=f32)
        bias = jnp.concatenate([(sel - 1.0) * (-NEG_BIG)] * hpg, axis=0)
        for g in range(ATT_KV_HEADS):
            k2 = kv_ref[pl.ds(row0, KC), g * LANES:(g + 1) * LANES]
            s_ref[slot, g] = lax.dot_general(q_stacks[g], k2, _NT, preferred_element_type=f32) + bias

    def softmax_pv(c, slot):
        row0 = chunk_start(c)
        for g in range(ATT_KV_HEADS):
            v_lo = (ATT_KV_HEADS + g) * LANES
            v1 = kv_ref[pl.ds(row0, KC), v_lo:v_lo + LANES] + ones_hi
            s = s_ref[slot, g]
            m_prev = m_ref[g]
            m_new = jnp.maximum(m_prev, jnp.max(s, axis=1, keepdims=True))
            alpha = jnp.exp(m_prev - m_new)
            p = jnp.exp(s - jnp.concatenate([m_new] * (KC // LANES), axis=1))
            acc_ref[g] = alpha * acc_ref[g] + jnp.dot(_bf(p), v1, preferred_element_type=f32)
            m_ref[g] = m_new

    def step(c, slot):
        logits(c + 1, 1 - slot)
        softmax_pv(c, slot)

    def pair_body(j, carry):
        step(2 * j, 0)
        step(2 * j + 1, 1)
        return carry

    logits(0, 0)
    npairs = (nch - 1) // 2
    lax.fori_loop(0, npairs, pair_body, 0)

    @pl.when(nch - 1 > 2 * npairs)
    def _():
        step(nch - 2, 0)
        softmax_pv(nch - 1, 1)

    @pl.when(nch - 1 == 2 * npairs)
    def _():
        softmax_pv(nch - 1, 0)

    outs = []
    for g in range(ATT_KV_HEADS):
        acc = acc_ref[g]
        o = acc / pltpu.roll(acc, ATT_HEAD_DIM, axis=1)
        for pp in range(hpg // 2):
            odd = pltpu.roll(o[(2 * pp + 1) * Q_TILE:(2 * pp + 2) * Q_TILE], ATT_HEAD_DIM, axis=1)
            outs.append(jnp.where(left, o[(2 * pp) * Q_TILE:(2 * pp + 1) * Q_TILE], odd))
    o_ref[...] = _bf(jnp.concatenate(outs, axis=1))


def _dsa(q, qi, misc, kv, ki, batch, seq):
    nq = seq // Q_TILE
    nkc = seq // KEY_CHUNK
    topk = min(TOPK_MAX, seq // 4)
    hpg = ATT_HEADS // ATT_KV_HEADS
    qrow = lambda w: pl.BlockSpec((Q_TILE, w), lambda b, i: (b * nq + i, 0))
    seq_spec = lambda w: pl.BlockSpec((seq, w), lambda b, i: (b, 0))
    tri = _bf(jnp.tril(jnp.ones((TIE_BLOCK, TIE_BLOCK), jnp.float32)))
    s_bytes = 2 * ATT_KV_HEADS * hpg * Q_TILE * KEY_CHUNK * 4
    est = (2 * seq * (SEG_KV + SEG_KI) * 2 + nkc * KEY_CHUNK * LANES * (4 + 2) + s_bytes
           + 2 * ATT_KV_HEADS * hpg * Q_TILE * LANES * 4 + 12 * hpg * Q_TILE * KEY_CHUNK * 4)
    return pl.pallas_call(
        functools.partial(_dsa_kernel, topk=topk),
        grid=(batch, nq),
        in_specs=[qrow(SEG_Q), qrow(SEG_QI), qrow(SEG_MISC), seq_spec(SEG_KV), seq_spec(SEG_KI),
                  pl.BlockSpec((TIE_BLOCK, TIE_BLOCK), lambda b, i: (0, 0))],
        out_specs=qrow(ATT_WIDTH),
        out_shape=jax.ShapeDtypeStruct((batch * seq, ATT_WIDTH), jnp.bfloat16),
        scratch_shapes=[pltpu.VMEM((nkc, KEY_CHUNK, LANES), jnp.int32),
                        pltpu.VMEM((nkc, KEY_CHUNK, LANES), jnp.bfloat16),
                        pltpu.VMEM((2, ATT_KV_HEADS, hpg * Q_TILE, KEY_CHUNK), jnp.float32),
                        pltpu.VMEM((ATT_KV_HEADS, hpg * Q_TILE, LANES), jnp.float32),
                        pltpu.VMEM((ATT_KV_HEADS, hpg * Q_TILE, LANES), jnp.float32)],
        compiler_params=_params(("parallel", "arbitrary"), est),
        name="dsa",
    )(q, qi, misc, kv, ki, tri)


def _merge_kernel(yssd_ref, yatt_ref, u_ref, uh_ref, g_ref, h_ref, pw_ref, ps_ref, wus_ref, wua_ref, wup_ref,
                  wo_ref, o_ref, ubuf, *, tiles_per_seq):
    f32 = jnp.float32
    tm = u_ref.shape[0]
    it = pl.program_id(0) % tiles_per_seq
    ubuf[POOL_HALO:POOL_HALO + tm, :] = u_ref[...]
    ubuf[0:POOL_HALO, :] = jnp.where(it == 0, 0.0, uh_ref[...])
    t1 = (it * tm + 1 + lax.broadcasted_iota(jnp.int32, (tm, POOL_GROUP_DIM), 0)).astype(f32)
    mixed = []
    for gi, win in enumerate(POOL_WINDOWS):
        cols = slice(gi * POOL_GROUP_DIM, (gi + 1) * POOL_GROUP_DIM)
        cur = ubuf[POOL_HALO:POOL_HALO + tm, cols]
        s = cur
        for d in range(1, win):
            s = s + ubuf[POOL_HALO - d:POOL_HALO - d + tm, cols]
        pooled = s / jnp.minimum(t1, float(win)) - cur
        mixed.append(jnp.dot(_bf(pooled), pw_ref[gi], preferred_element_type=f32))
    ypool = _bf(jnp.concatenate(mixed, axis=1) * ps_ref[...])

    g = g_ref[...]
    merged = (_sigmoid(g[:, :D_MODEL]) * jnp.dot(yssd_ref[...], wus_ref[...], preferred_element_type=f32)
              + _sigmoid(g[:, D_MODEL:2 * D_MODEL]) * jnp.dot(yatt_ref[...], wua_ref[...], preferred_element_type=f32)
              + _sigmoid(g[:, 2 * D_MODEL:]) * jnp.dot(ypool, wup_ref[...], preferred_element_type=f32))
    o_ref[...] = h_ref[...] + jnp.dot(_bf(merged), wo_ref[...], preferred_element_type=f32)


def _merge(y_ssd, y_att, u, gates, h, pool_w, pool_scale, w_up_ssd, w_up_attn, w_up_pool, w_out, l, seq):
    n = h.shape[0]
    tm = ROW_TILE
    tiles_per_seq = seq // tm
    row = lambda w: pl.BlockSpec((tm, w), lambda r: (r, 0))
    halo = pl.BlockSpec((POOL_HALO, POOL_WIDTH), lambda r: (jnp.maximum(r * (tm // POOL_HALO) - 1, 0), 0))
    w_bytes = (POOL_GROUPS * POOL_GROUP_DIM ** 2 + (SSD_INNER + ATT_WIDTH + POOL_WIDTH + D_MODEL) * D_MODEL) * 2
    est = (w_bytes + 2 * tm * ((SSD_INNER + ATT_WIDTH) * 2 + (POOL_WIDTH + SEG_G + 2 * D_MODEL) * 4)
           + (tm + POOL_HALO) * POOL_WIDTH * 4 + 12 * tm * D_MODEL * 4)
    return pl.pallas_call(
        functools.partial(_merge_kernel, tiles_per_seq=tiles_per_seq),
        grid=(n // tm,),
        in_specs=[row(SSD_INNER), row(ATT_WIDTH), row(POOL_WIDTH), halo, row(SEG_G), row(D_MODEL),
                  _layer_spec((POOL_GROUPS, POOL_GROUP_DIM, POOL_GROUP_DIM), l), _layer_spec((1, POOL_WIDTH), l),
                  _layer_spec((SSD_INNER, D_MODEL), l), _layer_spec((ATT_WIDTH, D_MODEL), l),
                  _layer_spec((POOL_WIDTH, D_MODEL), l), _layer_spec((D_MODEL, D_MODEL), l)],
        out_specs=row(D_MODEL),
        out_shape=jax.ShapeDtypeStruct((n, D_MODEL), jnp.float32),
        scratch_shapes=[pltpu.VMEM((tm + POOL_HALO, POOL_WIDTH), jnp.float32)],
        compiler_params=_params(("parallel",), est),
        name="merge",
    )(y_ssd, y_att, u, u, gates, h, pool_w, pool_scale, w_up_ssd, w_up_attn, w_up_pool, w_out)


def _ffn_kernel(h_ref, nw_ref, wi_ref, wo_ref, fw_ref, o_ref, *, final):
    f32 = jnp.float32
    h = h_ref[...]
    hn = _bf(_rms(h, nw_ref[...]))
    blk = FFN_HIDDEN // FFN_SPLIT
    out = h
    for j in range(FFN_SPLIT):
        a = jnp.dot(hn, wi_ref[:, j * blk:(j + 1) * blk], preferred_element_type=f32)
        b = jnp.dot(hn, wi_ref[:, FFN_HIDDEN + j * blk:FFN_HIDDEN + (j + 1) * blk], preferred_element_type=f32)
        out = out + jnp.dot(_bf(a * _sigmoid(a) * b), wo_ref[j * blk:(j + 1) * blk, :], preferred_element_type=f32)
    if final:
        out = _rms(out, fw_ref[...])
    o_ref[...] = out


def _ffn(h, norm_w, w_ffn_in, w_ffn_out, final_w, l, final):
    n = h.shape[0]
    tm = ROW_TILE
    row = pl.BlockSpec((tm, D_MODEL), lambda r: (r, 0))
    est = (3 * D_MODEL * FFN_HIDDEN * 2 + 4 * tm * D_MODEL * 4 + 6 * tm * (FFN_HIDDEN // FFN_SPLIT) * 4
           + 4 * tm * D_MODEL * 4)
    return pl.pallas_call(
        functools.partial(_ffn_kernel, final=final),
        grid=(n // tm,),
        in_specs=[row, _layer_spec((1, D_MODEL), l), _layer_spec((D_MODEL, 2 * FFN_HIDDEN), l),
                  _layer_spec((FFN_HIDDEN, D_MODEL), l),
                  pl.BlockSpec((1, D_MODEL), lambda r: (0, 0))],
        out_specs=row,
        out_shape=jax.ShapeDtypeStruct((n, D_MODEL), jnp.float32),
        compiler_params=_params(("parallel",), est),
        name="ffn",
    )(h, norm_w, w_ffn_in, w_ffn_out, final_w)


def _pack_w_in(w_in):
    offs, acc = [], 0
    for s in (SSD_INNER, SSD_CONV_DIM, SSD_HEADS, ATT_WIDTH, ATT_KV_HEADS * ATT_HEAD_DIM,
              ATT_KV_HEADS * ATT_HEAD_DIM, IDX_HEADS * IDX_HEAD_DIM, IDX_HEAD_DIM, IDX_HEADS, POOL_WIDTH):
        acc += s
        offs.append(acc)
    z, xbc, dt, q, k, v, qi, ki, wi, u, gates = jnp.split(w_in, offs, axis=-1)
    dh = ATT_HEAD_DIM
    kv = [k[..., g * dh:(g + 1) * dh] for g in range(ATT_KV_HEADS) for _ in range(2)]
    for g in range(ATT_KV_HEADS):
        kv += [v[..., g * dh:(g + 1) * dh], jnp.zeros_like(v[..., :dh])]
    pad = jnp.zeros(w_in.shape[:-1] + (SEG_MISC - SSD_HEADS - IDX_HEADS,), w_in.dtype)
    return _bf(jnp.concatenate([z, xbc, q] + kv + [qi, ki, ki, dt, wi, pad, u, gates], axis=-1))


def _pad_lanes(a):
    depth, w = a.shape
    return jnp.pad(a, ((0, 0), (0, LANES - w))).reshape(depth, 1, LANES)


def kernel(x, norm1_w, w_in, conv_w, conv_b, dt_bias, a_log, d_skip, ssd_norm_w, pool_w, pool_scale, w_up_ssd,
           w_up_attn, w_up_pool, w_out, norm2_w, w_ffn_in, w_ffn_out, final_norm_w):
    batch, seq, d = x.shape
    assert d == D_MODEL and seq % KEY_CHUNK == 0 and seq % ROW_TILE == 0 and (batch * seq) % ROW_TILE == 0
    depth = w_in.shape[0]
    row3 = lambda a: a.reshape(depth, 1, a.shape[-1])

    wcat = _pack_w_in(w_in)
    norm1 = row3(norm1_w)
    conv_b3 = row3(conv_b)
    dtb = _pad_lanes(dt_bias)
    alog = _pad_lanes(a_log)
    dskip_x = row3(jnp.repeat(d_skip, SSD_HEAD_DIM, axis=-1))
    ssd_nw = row3(ssd_norm_w)
    pool_s = row3(pool_scale)
    norm2 = row3(norm2_w)
    final_w = final_norm_w.reshape(1, D_MODEL)
    pool_wb, wus, wua, wup, wo, wfi, wfo = map(_bf, (pool_w, w_up_ssd, w_up_attn, w_up_pool, w_out, w_ffn_in,
                                                     w_ffn_out))

    h = x.reshape(batch * seq, D_MODEL)
    for l in range(depth):
        z, xbc, q, kv, qi, ki, misc, u, gates = _inproj(h, norm1, wcat, l)
        y_ssd = _ssd(xbc, z, misc, conv_w, conv_b3, dtb, alog, dskip_x, ssd_nw, l, batch, seq)
        y_att = _dsa(q, qi, misc, kv, ki, batch, seq)
        h = _merge(y_ssd, y_att, u, gates, h, pool_wb, pool_s, wus, wua, wup, wo, l, seq)
        h = _ffn(h, norm2, wfi, wfo, final_w, l, final=(l == depth - 1))
    return h.reshape(batch, seq, D_MODEL)
```

```python
import functools

import jax
import jax.numpy as jnp
from jax import lax
from jax.experimental import pallas as pl
from jax.experimental.pallas import tpu as pltpu

D_MODEL = 1024
DEPTH = 4
SSD_HEADS = 16
SSD_HEAD_DIM = 64
SSD_INNER = SSD_HEADS * SSD_HEAD_DIM
SSD_GROUPS = 2
SSD_STATE = 128
SSD_CONV_DIM = SSD_INNER + 2 * SSD_GROUPS * SSD_STATE
CONV_WIDTH = 4
SSD_CHUNK = 128
ATT_HEADS = 8
ATT_KV_HEADS = 2
ATT_HEAD_DIM = 64
ATT_WIDTH = ATT_HEADS * ATT_HEAD_DIM
IDX_HEADS = 4
IDX_HEAD_DIM = 64
TOPK_MAX = 256
POOL_WINDOWS = (2, 4, 8, 16)
POOL_GROUPS = 4
POOL_GROUP_DIM = 128
POOL_WIDTH = POOL_GROUPS * POOL_GROUP_DIM
N_BRANCH = 3
FFN_HIDDEN = 2816
EPS = 1e-6
ATT_SCALE = ATT_HEAD_DIM ** -0.5
LOG2_E = 1.4426950408889634
IDX_SCALE = (IDX_HEADS ** -0.5) * (IDX_HEAD_DIM ** -0.5)

LANES = 128
SUBLANES = 8
VMEM_BUDGET_BYTES = 60000 * 1024

ROW_TILE = 512
Q_TILE = LANES
KEY_CHUNK = 512
POOL_HALO = 16
FFN_SPLIT = 2
COUNT_WAYS = 8
TIE_BLOCK = 256

SEG_Z = SSD_INNER
SEG_XBC = SSD_CONV_DIM
SEG_Q = ATT_WIDTH
SEG_KV = 2 * ATT_KV_HEADS * 2 * ATT_HEAD_DIM
SEG_QI = IDX_HEADS * IDX_HEAD_DIM
SEG_KI = 2 * IDX_HEAD_DIM
SEG_MISC = LANES
SEG_U = POOL_WIDTH
SEG_G = N_BRANCH * D_MODEL
SEGS = (SEG_Z, SEG_XBC, SEG_Q, SEG_KV, SEG_QI, SEG_KI, SEG_MISC, SEG_U, SEG_G)
W_CAT = sum(SEGS)
MISC_DT = 0
MISC_WI = SSD_HEADS

INT_MIN = -(2 ** 31)
NEG_BIG = -1e30

_NT = (((1,), (1,)), ((), ()))


def _bf(x):
    return x.astype(jnp.bfloat16)


def _sigmoid(x):
    return 1.0 / (1.0 + jnp.exp(-x))


def _rms(x, w):
    return x * lax.rsqrt(jnp.mean(x * x, axis=-1, keepdims=True) + EPS) * w


def _params(sem, est_bytes):
    return pltpu.CompilerParams(dimension_semantics=sem,
                                vmem_limit_bytes=int(min(VMEM_BUDGET_BYTES, est_bytes)))


def _layer_spec(shape, l):
    nd = len(shape)
    return pl.BlockSpec((None,) + tuple(shape), lambda *_: (l,) + (0,) * nd, pipeline_mode=pl.Buffered(1))


def _inproj_kernel(h_ref, nw_ref, w_ref, *out_refs):
    xn = _bf(_rms(h_ref[...], nw_ref[...]))
    off = 0
    for ref, width in zip(out_refs, SEGS):
        r = jnp.dot(xn, w_ref[:, off:off + width], preferred_element_type=jnp.float32)
        ref[...] = r.astype(ref.dtype)
        off += width


def _inproj(h, norm_w, wcat, l):
    n = h.shape[0]
    tm = ROW_TILE
    dts = (jnp.float32, jnp.float32, jnp.bfloat16, jnp.bfloat16, jnp.bfloat16, jnp.bfloat16,
           jnp.float32, jnp.float32, jnp.float32)
    out_shape = tuple(jax.ShapeDtypeStruct((n, w), dt) for w, dt in zip(SEGS, dts))
    out_specs = tuple(pl.BlockSpec((tm, w), lambda i: (i, 0)) for w in SEGS)
    out_bytes = sum(tm * w * jnp.dtype(dt).itemsize for w, dt in zip(SEGS, dts))
    est = D_MODEL * W_CAT * 2 + 2 * tm * D_MODEL * 4 + 2 * out_bytes + 3 * tm * SEG_G * 4
    return pl.pallas_call(
        _inproj_kernel,
        grid=(n // tm,),
        in_specs=[pl.BlockSpec((tm, D_MODEL), lambda i: (i, 0)),
                  _layer_spec((1, D_MODEL), l),
                  _layer_spec((D_MODEL, W_CAT), l)],
        out_specs=out_specs,
        out_shape=out_shape,
        compiler_params=_params(("parallel",), est),
        name="inproj",
    )(h, norm_w, wcat)


def _col(a, j, width=LANES):
    return jnp.broadcast_to(a[:, j:j + 1], (a.shape[0], width))


def _expand_heads(a, spread):
    hi = _bf(a)
    lo = _bf(a - hi.astype(jnp.float32))
    return (jnp.dot(hi, spread, preferred_element_type=jnp.float32)
            + jnp.dot(lo, spread, preferred_element_type=jnp.float32))


def _silu(x):
    h = 0.5 * x
    return h + h * jnp.tanh(h)


def _ssd_kernel(xbc_ref, z_ref, misc_ref, cw_ref, cb_ref, dtb_ref, alog_ref, dskip_ref, nw_ref, spread_ref, o_ref,
                cbuf, state):
    L = SSD_CHUNK
    f32 = jnp.float32

    @pl.when(pl.program_id(1) == 0)
    def _():
        cbuf[0:SUBLANES, :] = jnp.zeros((SUBLANES, SSD_CONV_DIM), f32)
        state[...] = jnp.zeros(state.shape, f32)

    cbuf[SUBLANES:SUBLANES + L, :] = xbc_ref[...]
    conv = cb_ref[...]
    for k in range(CONV_WIDTH):
        start = SUBLANES - (CONV_WIDTH - 1) + k
        conv = conv + cw_ref[k:k + 1, :] * cbuf[start:start + L, :]
    cbuf[0:SUBLANES, :] = cbuf[L:L + SUBLANES, :]
    xbc = _silu(conv)
    xs = xbc[:, :SSD_INNER]

    lane = lax.broadcasted_iota(jnp.int32, (L, LANES), 1)
    row = lax.broadcasted_iota(jnp.int32, (L, LANES), 0)
    left = lane < SSD_HEAD_DIM
    causal = lane <= row
    tri = jnp.where(causal, 1.0, 0.0)

    dtr = misc_ref[...] + dtb_ref[...]
    dt = jnp.maximum(dtr, 0.0) + jnp.log1p(jnp.exp(-jnp.abs(dtr)))
    a = dt * (-jnp.exp(alog_ref[...]))
    a_cs = jnp.dot(tri, a, preferred_element_type=f32, precision=lax.Precision.HIGHEST)
    a_cs_t = a_cs.T
    ea = jnp.exp(a_cs)
    decay = jnp.exp(a_cs[L - 1:L, :] - a_cs)

    spread = spread_ref[...]
    x_all = xs * _expand_heads(dt, spread)
    ea_x = _expand_heads(ea, spread)
    xd_all = x_all * _expand_heads(decay, spread)

    gw = SSD_INNER // SSD_GROUPS
    y_parts = []
    for g in range(SSD_GROUPS):
        b_g = xbc[:, SSD_INNER + g * SSD_STATE:SSD_INNER + (g + 1) * SSD_STATE]
        c_off = SSD_INNER + SSD_GROUPS * SSD_STATE
        c_g = xbc[:, c_off + g * SSD_STATE:c_off + (g + 1) * SSD_STATE]
        cb = lax.dot_general(_bf(c_g), _bf(b_g), _NT, preferred_element_type=f32)
        st = state[g]
        y_off = jnp.dot(_bf(c_g), _bf(st), preferred_element_type=f32)
        new_st = jnp.dot(_bf(b_g.T), _bf(xd_all[:, g * gw:(g + 1) * gw]), preferred_element_type=f32)
        state[g] = st * ea_x[L - 1:L, g * gw:(g + 1) * gw] + new_st
        pairs = []
        for mm in range(SSD_HEADS // SSD_GROUPS // 2):
            m = g * (SSD_HEADS // SSD_GROUPS // 2) + mm
            lhs = []
            for hd in (2 * m, 2 * m + 1):
                seg = _col(a_cs, hd) - a_cs_t[hd:hd + 1, :]
                lhs.append(cb * jnp.where(causal, jnp.exp(seg), 0.0))
            xp = x_all[:, m * LANES:(m + 1) * LANES]
            rhs = jnp.concatenate([jnp.where(left, xp, 0.0), jnp.where(left, 0.0, xp)], axis=0)
            pairs.append(jnp.dot(_bf(jnp.concatenate(lhs, axis=1)), _bf(rhs), preferred_element_type=f32))
        y_parts.append(jnp.concatenate(pairs, axis=1) + y_off * ea_x[:, g * gw:(g + 1) * gw])
    y = jnp.concatenate(y_parts, axis=1) + dskip_ref[...] * xs
    o_ref[...] = _bf(_rms(y * _silu(z_ref[...]), nw_ref[...]))


def _ssd(xbc, z, misc, conv_w, conv_b, dtb, alog, dskip_x, norm_w, l, batch, seq):
    L = SSD_CHUNK
    nc = seq // L
    row_spec = lambda w: pl.BlockSpec((L, w), lambda b, c: (b * nc + c, 0))
    head_of_col = lax.broadcasted_iota(jnp.int32, (LANES, SSD_INNER), 1) // SSD_HEAD_DIM
    spread = _bf(head_of_col == lax.broadcasted_iota(jnp.int32, (LANES, SSD_INNER), 0))
    est = (2 * L * (SSD_CONV_DIM + SSD_INNER + LANES) * 4 + 2 * L * SSD_INNER * 2
           + (L + 2 * SUBLANES) * SSD_CONV_DIM * 4 + SSD_GROUPS * SSD_STATE * SSD_INNER // SSD_GROUPS * 4
           + 40 * L * SSD_INNER * 4)
    return pl.pallas_call(
        _ssd_kernel,
        grid=(batch, nc),
        in_specs=[row_spec(SSD_CONV_DIM), row_spec(SSD_INNER), row_spec(LANES),
                  _layer_spec((CONV_WIDTH, SSD_CONV_DIM), l), _layer_spec((1, SSD_CONV_DIM), l),
                  _layer_spec((1, LANES), l), _layer_spec((1, LANES), l),
                  _layer_spec((1, SSD_INNER), l), _layer_spec((1, SSD_INNER), l),
                  pl.BlockSpec((LANES, SSD_INNER), lambda b, c: (0, 0))],
        out_specs=row_spec(SSD_INNER),
        out_shape=jax.ShapeDtypeStruct((batch * seq, SSD_INNER), jnp.bfloat16),
        scratch_shapes=[pltpu.VMEM((L + 2 * SUBLANES, SSD_CONV_DIM), jnp.float32),
                        pltpu.VMEM((SSD_GROUPS, SSD_STATE, SSD_INNER // SSD_GROUPS), jnp.float32)],
        compiler_params=_params(("parallel", "arbitrary"), est),
        name="ssd",
    )(xbc, z, misc, conv_w, conv_b, dtb, alog, dskip_x, norm_w, spread)


def _dsa_kernel(q_ref, qi_ref, misc_ref, k_ref, vt_ref, ki_ref, tri_ref, o_ref, key_ref, s_ref, m_ref, acc_ref, *,
                topk):
    f32 = jnp.float32
    i32 = jnp.int32
    KC = KEY_CHUNK
    i = pl.program_id(1)
    nch = (i * Q_TILE + Q_TILE + KC - 1) // KC
    lane = lax.broadcasted_iota(i32, (1, LANES), 1)
    qpos = i * Q_TILE + lane
    left = lane < ATT_HEAD_DIM
    sub_iota = lax.broadcasted_iota(i32, (KC, LANES), 0)

    misc_t = misc_ref[...].T
    qi = qi_ref[...].astype(f32)
    qi_heads, w_heads = [], []
    for h in range(IDX_HEADS):
        qp = qi[:, (h // 2) * LANES:(h // 2 + 1) * LANES]
        qi_heads.append(_bf(jnp.where(left, qp, 0.0) if h % 2 == 0 else jnp.where(left, 0.0, qp)))
        w_heads.append(misc_t[MISC_WI + h:MISC_WI + h + 1, :] * IDX_SCALE)
    qi_pairs = [jnp.concatenate(qi_heads[h:h + 2], axis=0) for h in range(0, IDX_HEADS, 2)]

    def score_body(c, carry):
        row0 = pl.multiple_of(c * KC, KC)
        ki_c = ki_ref[pl.ds(row0, KC), :]
        sc = jnp.zeros((KC, LANES), f32)
        for hp, qi_pair in enumerate(qi_pairs):
            s2 = lax.dot_general(ki_c, qi_pair, _NT, preferred_element_type=f32)
            for j in range(2):
                s = s2[:, j * LANES:(j + 1) * LANES]
                sc = sc + jnp.maximum(s, 0.0) * w_heads[2 * hp + j]
        sc = jnp.where(sc == 0.0, 0.0, sc)
        bits = lax.bitcast_convert_type(sc, i32)
        key = bits ^ ((bits >> 31) & 0x7FFFFFFF)
        key_ref[c] = jnp.where(row0 + sub_iota <= qpos, key, INT_MIN)
        return carry

    lax.fori_loop(0, nch, score_body, 0)

    def count(pred):
        rows = COUNT_WAYS * SUBLANES

        def body(c, acc):
            ind = jnp.where(pred(key_ref[c]), 1.0, 0.0)
            return acc + jnp.sum(ind.reshape(KC // rows, rows, LANES), axis=0)
        acc = lax.fori_loop(0, nch, body, jnp.zeros((rows, LANES), f32))
        return jnp.sum(acc, axis=0, keepdims=True)

    kf = float(topk)

    def bit_body(it, t):
        cand = t + lax.shift_left(jnp.int32(1), 31 - it)
        return jnp.where(count(lambda k: k >= cand) >= kf, cand, t)

    t = lax.fori_loop(0, 32, bit_body, jnp.full((1, LANES), INT_MIN, i32))

    need = kf - count(lambda k: k > t)
    need = jnp.where(t == INT_MIN, 0.0, need)
    tri = tri_ref[...]
    tb = tri.shape[0]

    def select_bias(c, seen):
        k = key_ref[c]
        tie = jnp.where(k == t, 1.0, 0.0)
        picks = []
        for r in range(0, KC, tb):
            rank = seen + jnp.dot(tri, _bf(tie[r:r + tb]), preferred_element_type=f32)
            picks.append(jnp.where(rank <= need, tie[r:r + tb], 0.0))
            seen = rank[tb - 1:tb]
        sel = jnp.where(k > t, 1.0, jnp.concatenate(picks, axis=0))
        return _bf((sel - 1.0) * (-NEG_BIG)), seen

    q = q_ref[...].astype(f32)
    hpg = ATT_HEADS // ATT_KV_HEADS
    eye = jnp.where(lax.broadcasted_iota(i32, (LANES, LANES), 0) == lax.broadcasted_iota(i32, (LANES, LANES), 1),
                    1.0, 0.0)
    q_stacks = []
    for g in range(ATT_KV_HEADS):
        parts = []
        for hh in range(hpg):
            h = g * hpg + hh
            qp = q[:, (h // 2) * LANES:(h // 2 + 1) * LANES]
            parts.append(jnp.where(left, qp, 0.0) if h % 2 == 0 else jnp.where(left, 0.0, qp))
        q_stacks.append(_bf(jnp.concatenate([jnp.concatenate(parts, axis=0), jnp.concatenate([eye] * hpg, axis=0)],
                                            axis=1)))
    ones_lo = _bf(jnp.where(lax.broadcasted_iota(i32, (LANES, KC), 0) < ATT_HEAD_DIM, 0.0, 1.0))
    m_ref[...] = jnp.full(m_ref.shape, NEG_BIG, f32)
    acc_ref[...] = jnp.zeros(acc_ref.shape, f32)

    def chunk_start(c):
        return c * KC if isinstance(c, int) else pl.multiple_of(c * KC, KC)

    def logits(c, slot, seen):
        row0 = chunk_start(c)
        bias, seen = select_bias(c, seen)
        for g in range(ATT_KV_HEADS):
            k2 = jnp.concatenate([k_ref[pl.ds(row0, KC), g * LANES:(g + 1) * LANES], bias], axis=1)
            s_ref[slot, g] = lax.dot_general(k2, q_stacks[g], _NT, preferred_element_type=f32)
        return seen

    def softmax_pv(c, slot):
        for g in range(ATT_KV_HEADS):
            v1 = vt_ref[c, g * LANES:(g + 1) * LANES, :] + ones_lo
            s = s_ref[slot, g]
            m_prev = m_ref[g]
            m_new = jnp.maximum(m_prev, jnp.max(s, axis=0, keepdims=True))
            alpha = jnp.exp2(m_prev - m_new)
            p = jnp.exp2(s - m_new)
            acc_ref[g] = alpha * acc_ref[g] + jnp.dot(v1, _bf(p), preferred_element_type=f32)
            m_ref[g] = m_new

    def step(c, slot, seen):
        seen = logits(c + 1, 1 - slot, seen)
        softmax_pv(c, slot)
        return seen

    def pair_body(j, seen):
        return step(2 * j + 1, 1, step(2 * j, 0, seen))

    npairs = (nch - 1) // 2
    seen = lax.fori_loop(0, npairs, pair_body, logits(0, 0, jnp.zeros((1, LANES), f32)))

    @pl.when(nch - 1 > 2 * npairs)
    def _():
        step(nch - 2, 0, seen)
        softmax_pv(nch - 1, 1)

    @pl.when(nch - 1 == 2 * npairs)
    def _():
        softmax_pv(nch - 1, 0)

    outs = []
    for g in range(ATT_KV_HEADS):
        acc = acc_ref[g]
        heads = []
        for hh in range(hpg):
            blk = acc[:, hh * Q_TILE:(hh + 1) * Q_TILE].T
            heads.append(blk / pltpu.roll(blk, ATT_HEAD_DIM, axis=1))
        for pp in range(hpg // 2):
            outs.append(jnp.where(left, heads[2 * pp], pltpu.roll(heads[2 * pp + 1], ATT_HEAD_DIM, axis=1)))
    o_ref[...] = _bf(jnp.concatenate(outs, axis=1))


def _dsa(q, qi, misc, kv, ki, batch, seq):
    nq = seq // Q_TILE
    nkc = seq // KEY_CHUNK
    topk = min(TOPK_MAX, seq // 4)
    hpg = ATT_HEADS // ATT_KV_HEADS
    qrow = lambda w: pl.BlockSpec((Q_TILE, w), lambda b, i: (b * nq + i, 0))
    seq_spec = lambda w: pl.BlockSpec((seq, w), lambda b, i: (b, 0))
    tri = _bf(jnp.tril(jnp.ones((TIE_BLOCK, TIE_BLOCK), jnp.float32)))
    half = SEG_KV // 2
    vt = kv[:, half:].reshape(batch * nkc, KEY_CHUNK, half).transpose(0, 2, 1)
    s_bytes = 2 * ATT_KV_HEADS * hpg * Q_TILE * KEY_CHUNK * 4
    est = (2 * seq * (SEG_KV + SEG_KI) * 2 + nkc * KEY_CHUNK * LANES * 4 + s_bytes
           + ATT_KV_HEADS * (LANES + SUBLANES) * hpg * Q_TILE * 4 + 12 * hpg * Q_TILE * KEY_CHUNK * 4)
    return pl.pallas_call(
        functools.partial(_dsa_kernel, topk=topk),
        grid=(batch, nq),
        in_specs=[qrow(SEG_Q), qrow(SEG_QI), qrow(SEG_MISC), seq_spec(half),
                  pl.BlockSpec((nkc, half, KEY_CHUNK), lambda b, i: (b, 0, 0)), seq_spec(SEG_KI),
                  pl.BlockSpec((TIE_BLOCK, TIE_BLOCK), lambda b, i: (0, 0))],
        out_specs=qrow(ATT_WIDTH),
        out_shape=jax.ShapeDtypeStruct((batch * seq, ATT_WIDTH), jnp.bfloat16),
        scratch_shapes=[pltpu.VMEM((nkc, KEY_CHUNK, LANES), jnp.int32),
                        pltpu.VMEM((2, ATT_KV_HEADS, KEY_CHUNK, hpg * Q_TILE), jnp.float32),
                        pltpu.VMEM((ATT_KV_HEADS, 1, hpg * Q_TILE), jnp.float32),
                        pltpu.VMEM((ATT_KV_HEADS, LANES, hpg * Q_TILE), jnp.float32)],
        compiler_params=_params(("parallel", "arbitrary"), est),
        name="dsa",
    )(q, qi, misc, kv, vt, ki, tri)


def _merge_kernel(yssd_ref, yatt_ref, u_ref, uh_ref, g_ref, h_ref, pw_ref, ps_ref, wus_ref, wua_ref, wup_ref,
                  wo_ref, o_ref, ubuf, *, tiles_per_seq):
    f32 = jnp.float32
    tm = u_ref.shape[0]
    it = pl.program_id(0) % tiles_per_seq
    ubuf[POOL_HALO:POOL_HALO + tm, :] = u_ref[...]
    ubuf[0:POOL_HALO, :] = jnp.where(it == 0, 0.0, uh_ref[...])
    t1 = (it * tm + 1 + lax.broadcasted_iota(jnp.int32, (tm, POOL_GROUP_DIM), 0)).astype(f32)
    mixed = []
    for gi, win in enumerate(POOL_WINDOWS):
        cols = slice(gi * POOL_GROUP_DIM, (gi + 1) * POOL_GROUP_DIM)
        cur = ubuf[POOL_HALO:POOL_HALO + tm, cols]
        s = cur
        for d in range(1, win):
            s = s + ubuf[POOL_HALO - d:POOL_HALO - d + tm, cols]
        pooled = s / jnp.minimum(t1, float(win)) - cur
        mixed.append(jnp.dot(_bf(pooled), pw_ref[gi], preferred_element_type=f32))
    ypool = _bf(jnp.concatenate(mixed, axis=1) * ps_ref[...])

    g = g_ref[...]
    merged = (_sigmoid(g[:, :D_MODEL]) * jnp.dot(yssd_ref[...], wus_ref[...], preferred_element_type=f32)
              + _sigmoid(g[:, D_MODEL:2 * D_MODEL]) * jnp.dot(yatt_ref[...], wua_ref[...], preferred_element_type=f32)
              + _sigmoid(g[:, 2 * D_MODEL:]) * jnp.dot(ypool, wup_ref[...], preferred_element_type=f32))
    o_ref[...] = h_ref[...] + jnp.dot(_bf(merged), wo_ref[...], preferred_element_type=f32)


def _merge(y_ssd, y_att, u, gates, h, pool_w, pool_scale, w_up_ssd, w_up_attn, w_up_pool, w_out, l, seq):
    n = h.shape[0]
    tm = ROW_TILE
    tiles_per_seq = seq // tm
    row = lambda w: pl.BlockSpec((tm, w), lambda r: (r, 0))
    halo = pl.BlockSpec((POOL_HALO, POOL_WIDTH), lambda r: (jnp.maximum(r * (tm // POOL_HALO) - 1, 0), 0))
    w_bytes = (POOL_GROUPS * POOL_GROUP_DIM ** 2 + (SSD_INNER + ATT_WIDTH + POOL_WIDTH + D_MODEL) * D_MODEL) * 2
    est = (w_bytes + 2 * tm * ((SSD_INNER + ATT_WIDTH) * 2 + (POOL_WIDTH + SEG_G + 2 * D_MODEL) * 4)
           + (tm + POOL_HALO) * POOL_WIDTH * 4 + 12 * tm * D_MODEL * 4)
    return pl.pallas_call(
        functools.partial(_merge_kernel, tiles_per_seq=tiles_per_seq),
        grid=(n // tm,),
        in_specs=[row(SSD_INNER), row(ATT_WIDTH), row(POOL_WIDTH), halo, row(SEG_G), row(D_MODEL),
                  _layer_spec((POOL_GROUPS, POOL_GROUP_DIM, POOL_GROUP_DIM), l), _layer_spec((1, POOL_WIDTH), l),
                  _layer_spec((SSD_INNER, D_MODEL), l), _layer_spec((ATT_WIDTH, D_MODEL), l),
                  _layer_spec((POOL_WIDTH, D_MODEL), l), _layer_spec((D_MODEL, D_MODEL), l)],
        out_specs=row(D_MODEL),
        out_shape=jax.ShapeDtypeStruct((n, D_MODEL), jnp.float32),
        scratch_shapes=[pltpu.VMEM((tm + POOL_HALO, POOL_WIDTH), jnp.float32)],
        compiler_params=_params(("parallel",), est),
        name="merge",
    )(y_ssd, y_att, u, u, gates, h, pool_w, pool_scale, w_up_ssd, w_up_attn, w_up_pool, w_out)


def _ffn_kernel(h_ref, nw_ref, wi_ref, wo_ref, fw_ref, o_ref, *, final):
    f32 = jnp.float32
    h = h_ref[...]
    hn = _bf(_rms(h, nw_ref[...]))
    blk = FFN_HIDDEN // FFN_SPLIT
    out = h
    for j in range(FFN_SPLIT):
        a = jnp.dot(hn, wi_ref[:, j * blk:(j + 1) * blk], preferred_element_type=f32)
        b = jnp.dot(hn, wi_ref[:, FFN_HIDDEN + j * blk:FFN_HIDDEN + (j + 1) * blk], preferred_element_type=f32)
        out = out + jnp.dot(_bf(a * _sigmoid(a) * b), wo_ref[j * blk:(j + 1) * blk, :], preferred_element_type=f32)
    if final:
        out = _rms(out, fw_ref[...])
    o_ref[...] = out


def _ffn(h, norm_w, w_ffn_in, w_ffn_out, final_w, l, final):
    n = h.shape[0]
    tm = ROW_TILE
    row = pl.BlockSpec((tm, D_MODEL), lambda r: (r, 0))
    est = (3 * D_MODEL * FFN_HIDDEN * 2 + 4 * tm * D_MODEL * 4 + 6 * tm * (FFN_HIDDEN // FFN_SPLIT) * 4
           + 4 * tm * D_MODEL * 4)
    return pl.pallas_call(
        functools.partial(_ffn_kernel, final=final),
        grid=(n // tm,),
        in_specs=[row, _layer_spec((1, D_MODEL), l), _layer_spec((D_MODEL, 2 * FFN_HIDDEN), l),
                  _layer_spec((FFN_HIDDEN, D_MODEL), l),
                  pl.BlockSpec((1, D_MODEL), lambda r: (0, 0))],
        out_specs=row,
        out_shape=jax.ShapeDtypeStruct((n, D_MODEL), jnp.float32),
        compiler_params=_params(("parallel",), est),
        name="ffn",
    )(h, norm_w, w_ffn_in, w_ffn_out, final_w)


def _pack_w_in(w_in):
    offs, acc = [], 0
    for s in (SSD_INNER, SSD_CONV_DIM, SSD_HEADS, ATT_WIDTH, ATT_KV_HEADS * ATT_HEAD_DIM,
              ATT_KV_HEADS * ATT_HEAD_DIM, IDX_HEADS * IDX_HEAD_DIM, IDX_HEAD_DIM, IDX_HEADS, POOL_WIDTH):
        acc += s
        offs.append(acc)
    z, xbc, dt, q, k, v, qi, ki, wi, u, gates = jnp.split(w_in, offs, axis=-1)
    q = q * (ATT_SCALE * LOG2_E)
    dh = ATT_HEAD_DIM
    kv = [k[..., g * dh:(g + 1) * dh] for g in range(ATT_KV_HEADS) for _ in range(2)]
    for g in range(ATT_KV_HEADS):
        kv += [v[..., g * dh:(g + 1) * dh], jnp.zeros_like(v[..., :dh])]
    pad = jnp.zeros(w_in.shape[:-1] + (SEG_MISC - SSD_HEADS - IDX_HEADS,), w_in.dtype)
    return _bf(jnp.concatenate([z, xbc, q] + kv + [qi, ki, ki, dt, wi, pad, u, gates], axis=-1))


def _pad_lanes(a):
    depth, w = a.shape
    return jnp.pad(a, ((0, 0), (0, LANES - w))).reshape(depth, 1, LANES)


def kernel(x, norm1_w, w_in, conv_w, conv_b, dt_bias, a_log, d_skip, ssd_norm_w, pool_w, pool_scale, w_up_ssd,
           w_up_attn, w_up_pool, w_out, norm2_w, w_ffn_in, w_ffn_out, final_norm_w):
    batch, seq, d = x.shape
    assert d == D_MODEL and seq % KEY_CHUNK == 0 and seq % ROW_TILE == 0 and (batch * seq) % ROW_TILE == 0
    depth = w_in.shape[0]
    row3 = lambda a: a.reshape(depth, 1, a.shape[-1])

    wcat = _pack_w_in(w_in)
    norm1 = row3(norm1_w)
    conv_b3 = row3(conv_b)
    dtb = _pad_lanes(dt_bias)
    alog = _pad_lanes(a_log)
    dskip_x = row3(jnp.repeat(d_skip, SSD_HEAD_DIM, axis=-1))
    ssd_nw = row3(ssd_norm_w)
    pool_s = row3(pool_scale)
    norm2 = row3(norm2_w)
    final_w = final_norm_w.reshape(1, D_MODEL)
    pool_wb, wus, wua, wup, wo, wfi, wfo = map(_bf, (pool_w, w_up_ssd, w_up_attn, w_up_pool, w_out, w_ffn_in,
                                                     w_ffn_out))

    h = x.reshape(batch * seq, D_MODEL)
    for l in range(depth):
        z, xbc, q, kv, qi, ki, misc, u, gates = _inproj(h, norm1, wcat, l)
        y_ssd = _ssd(xbc, z, misc, conv_w, conv_b3, dtb, alog, dskip_x, ssd_nw, l, batch, seq)
        y_att = _dsa(q, qi, misc, kv, ki, batch, seq)
        h = _merge(y_ssd, y_att, u, gates, h, pool_wb, pool_s, wus, wua, wup, wo, l, seq)
        h = _ffn(h, norm2, wfi, wfo, final_w, l, final=(l == depth - 1))
    return h.reshape(batch, seq, D_MODEL)
```

```python
import functools

import jax
import jax.numpy as jnp
from jax import lax
from jax.experimental import pallas as pl
from jax.experimental.pallas import tpu as pltpu

D_MODEL = 1024
DEPTH = 4
SSD_HEADS = 16
SSD_HEAD_DIM = 64
SSD_INNER = SSD_HEADS * SSD_HEAD_DIM
SSD_GROUPS = 2
SSD_STATE = 128
SSD_CONV_DIM = SSD_INNER + 2 * SSD_GROUPS * SSD_STATE
CONV_WIDTH = 4
SSD_CHUNK = 128
ATT_HEADS = 8
ATT_KV_HEADS = 2
ATT_HEAD_DIM = 64
ATT_WIDTH = ATT_HEADS * ATT_HEAD_DIM
IDX_HEADS = 4
IDX_HEAD_DIM = 64
TOPK_MAX = 256
POOL_WINDOWS = (2, 4, 8, 16)
POOL_GROUPS = 4
POOL_GROUP_DIM = 128
POOL_WIDTH = POOL_GROUPS * POOL_GROUP_DIM
N_BRANCH = 3
FFN_HIDDEN = 2816
EPS = 1e-6
ATT_SCALE = ATT_HEAD_DIM ** -0.5
LOG2_E = 1.4426950408889634
IDX_SCALE = (IDX_HEADS ** -0.5) * (IDX_HEAD_DIM ** -0.5)

LANES = 128
SUBLANES = 8
VMEM_BUDGET_BYTES = 60000 * 1024

ROW_TILE = 512
Q_TILE = LANES
KEY_CHUNK = 512
POOL_HALO = 16
FFN_SPLIT = 2
PLANE_BLOCK = 32 * SUBLANES
BLOCKS_PER_CHUNK = KEY_CHUNK // PLANE_BLOCK
TIE_BLOCK = PLANE_BLOCK

SEG_Z = SSD_INNER
SEG_XBC = SSD_CONV_DIM
SEG_Q = ATT_WIDTH
SEG_KV = 2 * ATT_KV_HEADS * 2 * ATT_HEAD_DIM
SEG_QI = IDX_HEADS * IDX_HEAD_DIM
SEG_KI = 2 * IDX_HEAD_DIM
SEG_MISC = LANES
SEG_U = POOL_WIDTH
SEG_G = N_BRANCH * D_MODEL
SEGS = (SEG_Z, SEG_XBC, SEG_Q, SEG_KV, SEG_QI, SEG_KI, SEG_MISC, SEG_U, SEG_G)
W_CAT = sum(SEGS)
MISC_DT = 0
MISC_WI = SSD_HEADS

INT_MIN = -(2 ** 31)
NEG_BIG = -1e30

_NT = (((1,), (1,)), ((), ()))


def _bf(x):
    return x.astype(jnp.bfloat16)


def _sigmoid(x):
    return 1.0 / (1.0 + jnp.exp(-x))


def _rms(x, w):
    return x * lax.rsqrt(jnp.mean(x * x, axis=-1, keepdims=True) + EPS) * w


def _params(sem, est_bytes):
    return pltpu.CompilerParams(dimension_semantics=sem,
                                vmem_limit_bytes=int(min(VMEM_BUDGET_BYTES, est_bytes)))


def _layer_spec(shape, l):
    nd = len(shape)
    return pl.BlockSpec((None,) + tuple(shape), lambda *_: (l,) + (0,) * nd, pipeline_mode=pl.Buffered(1))


def _inproj_kernel(h_ref, nw_ref, w_ref, *out_refs):
    xn = _bf(_rms(h_ref[...], nw_ref[...]))
    off = 0
    for ref, width in zip(out_refs, SEGS):
        r = jnp.dot(xn, w_ref[:, off:off + width], preferred_element_type=jnp.float32)
        ref[...] = r.astype(ref.dtype)
        off += width


def _inproj(h, norm_w, wcat, l):
    n = h.shape[0]
    tm = ROW_TILE
    dts = (jnp.float32, jnp.float32, jnp.bfloat16, jnp.bfloat16, jnp.bfloat16, jnp.bfloat16,
           jnp.float32, jnp.float32, jnp.float32)
    out_shape = tuple(jax.ShapeDtypeStruct((n, w), dt) for w, dt in zip(SEGS, dts))
    out_specs = tuple(pl.BlockSpec((tm, w), lambda i: (i, 0)) for w in SEGS)
    out_bytes = sum(tm * w * jnp.dtype(dt).itemsize for w, dt in zip(SEGS, dts))
    est = D_MODEL * W_CAT * 2 + 2 * tm * D_MODEL * 4 + 2 * out_bytes + 3 * tm * SEG_G * 4
    return pl.pallas_call(
        _inproj_kernel,
        grid=(n // tm,),
        in_specs=[pl.BlockSpec((tm, D_MODEL), lambda i: (i, 0)),
                  _layer_spec((1, D_MODEL), l),
                  _layer_spec((D_MODEL, W_CAT), l)],
        out_specs=out_specs,
        out_shape=out_shape,
        compiler_params=_params(("parallel",), est),
        name="inproj",
    )(h, norm_w, wcat)


def _col(a, j, width=LANES):
    return jnp.broadcast_to(a[:, j:j + 1], (a.shape[0], width))


def _expand_heads(a, spread):
    hi = _bf(a)
    lo = _bf(a - hi.astype(jnp.float32))
    return (jnp.dot(hi, spread, preferred_element_type=jnp.float32)
            + jnp.dot(lo, spread, preferred_element_type=jnp.float32))


def _silu(x):
    h = 0.5 * x
    return h + h * jnp.tanh(h)


def _ssd_kernel(xbc_ref, z_ref, misc_ref, cw_ref, cb_ref, dtb_ref, alog_ref, dskip_ref, nw_ref, spread_ref, o_ref,
                cbuf, state):
    L = SSD_CHUNK
    f32 = jnp.float32

    @pl.when(pl.program_id(1) == 0)
    def _():
        cbuf[0:SUBLANES, :] = jnp.zeros((SUBLANES, SSD_CONV_DIM), f32)
        state[...] = jnp.zeros(state.shape, f32)

    cbuf[SUBLANES:SUBLANES + L, :] = xbc_ref[...]
    conv = cb_ref[...]
    for k in range(CONV_WIDTH):
        start = SUBLANES - (CONV_WIDTH - 1) + k
        conv = conv + cw_ref[k:k + 1, :] * cbuf[start:start + L, :]
    cbuf[0:SUBLANES, :] = cbuf[L:L + SUBLANES, :]
    xbc = _silu(conv)
    xs = xbc[:, :SSD_INNER]

    lane = lax.broadcasted_iota(jnp.int32, (L, LANES), 1)
    row = lax.broadcasted_iota(jnp.int32, (L, LANES), 0)
    left = lane < SSD_HEAD_DIM
    causal = lane <= row
    tri = jnp.where(causal, 1.0, 0.0)

    dtr = misc_ref[...] + dtb_ref[...]
    dt = jnp.maximum(dtr, 0.0) + jnp.log1p(jnp.exp(-jnp.abs(dtr)))
    a = dt * (-jnp.exp(alog_ref[...]))
    a_cs = jnp.dot(tri, a, preferred_element_type=f32, precision=lax.Precision.HIGHEST)
    a_cs_t = a_cs.T
    ea = jnp.exp(a_cs)
    decay = jnp.exp(a_cs[L - 1:L, :] - a_cs)

    spread = spread_ref[...]
    x_all = xs * _expand_heads(dt, spread)
    ea_x = _expand_heads(ea, spread)
    xd_all = x_all * _expand_heads(decay, spread)

    gw = SSD_INNER // SSD_GROUPS
    y_parts = []
    for g in range(SSD_GROUPS):
        b_g = xbc[:, SSD_INNER + g * SSD_STATE:SSD_INNER + (g + 1) * SSD_STATE]
        c_off = SSD_INNER + SSD_GROUPS * SSD_STATE
        c_g = xbc[:, c_off + g * SSD_STATE:c_off + (g + 1) * SSD_STATE]
        cb = lax.dot_general(_bf(c_g), _bf(b_g), _NT, preferred_element_type=f32)
        st = state[g]
        y_off = jnp.dot(_bf(c_g), _bf(st), preferred_element_type=f32)
        new_st = jnp.dot(_bf(b_g.T), _bf(xd_all[:, g * gw:(g + 1) * gw]), preferred_element_type=f32)
        state[g] = st * ea_x[L - 1:L, g * gw:(g + 1) * gw] + new_st
        pairs = []
        for mm in range(SSD_HEADS // SSD_GROUPS // 2):
            m = g * (SSD_HEADS // SSD_GROUPS // 2) + mm
            lhs = []
            for hd in (2 * m, 2 * m + 1):
                seg = _col(a_cs, hd) - a_cs_t[hd:hd + 1, :]
                lhs.append(cb * jnp.where(causal, jnp.exp(seg), 0.0))
            xp = x_all[:, m * LANES:(m + 1) * LANES]
            rhs = jnp.concatenate([jnp.where(left, xp, 0.0), jnp.where(left, 0.0, xp)], axis=0)
            pairs.append(jnp.dot(_bf(jnp.concatenate(lhs, axis=1)), _bf(rhs), preferred_element_type=f32))
        y_parts.append(jnp.concatenate(pairs, axis=1) + y_off * ea_x[:, g * gw:(g + 1) * gw])
    y = jnp.concatenate(y_parts, axis=1) + dskip_ref[...] * xs
    o_ref[...] = _bf(_rms(y * _silu(z_ref[...]), nw_ref[...]))


def _ssd(xbc, z, misc, conv_w, conv_b, dtb, alog, dskip_x, norm_w, l, batch, seq):
    L = SSD_CHUNK
    nc = seq // L
    row_spec = lambda w: pl.BlockSpec((L, w), lambda b, c: (b * nc + c, 0))
    head_of_col = lax.broadcasted_iota(jnp.int32, (LANES, SSD_INNER), 1) // SSD_HEAD_DIM
    spread = _bf(head_of_col == lax.broadcasted_iota(jnp.int32, (LANES, SSD_INNER), 0))
    est = (2 * L * (SSD_CONV_DIM + SSD_INNER + LANES) * 4 + 2 * L * SSD_INNER * 2
           + (L + 2 * SUBLANES) * SSD_CONV_DIM * 4 + SSD_GROUPS * SSD_STATE * SSD_INNER // SSD_GROUPS * 4
           + 40 * L * SSD_INNER * 4)
    return pl.pallas_call(
        _ssd_kernel,
        grid=(batch, nc),
        in_specs=[row_spec(SSD_CONV_DIM), row_spec(SSD_INNER), row_spec(LANES),
                  _layer_spec((CONV_WIDTH, SSD_CONV_DIM), l), _layer_spec((1, SSD_CONV_DIM), l),
                  _layer_spec((1, LANES), l), _layer_spec((1, LANES), l),
                  _layer_spec((1, SSD_INNER), l), _layer_spec((1, SSD_INNER), l),
                  pl.BlockSpec((LANES, SSD_INNER), lambda b, c: (0, 0))],
        out_specs=row_spec(SSD_INNER),
        out_shape=jax.ShapeDtypeStruct((batch * seq, SSD_INNER), jnp.bfloat16),
        scratch_shapes=[pltpu.VMEM((L + 2 * SUBLANES, SSD_CONV_DIM), jnp.float32),
                        pltpu.VMEM((SSD_GROUPS, SSD_STATE, SSD_INNER // SSD_GROUPS), jnp.float32)],
        compiler_params=_params(("parallel", "arbitrary"), est),
        name="ssd",
    )(xbc, z, misc, conv_w, conv_b, dtb, alog, dskip_x, norm_w, spread)


def _bit_planes(a):
    a = list(a)
    j, m = 16, 0x0000FFFF
    while j:
        for k in range(32):
            if not k & j:
                t = (a[k] ^ (a[k + j] >> j)) & m
                a[k] = a[k] ^ t
                a[k + j] = a[k + j] ^ (t << j)
        j >>= 1
        m ^= m << j
    return a


def _dsa_kernel(q_ref, qi_ref, misc_ref, k_ref, vt_ref, ki_ref, tri_ref, o_ref, plane_ref, bits_ref, s_ref, m_ref,
                acc_ref, *, topk):
    f32 = jnp.float32
    i32 = jnp.int32
    KC = KEY_CHUNK
    i = pl.program_id(1)
    nch = (i * Q_TILE + Q_TILE + KC - 1) // KC
    lane = lax.broadcasted_iota(i32, (1, LANES), 1)
    qpos = i * Q_TILE + lane
    left = lane < ATT_HEAD_DIM
    sub_iota = lax.broadcasted_iota(i32, (KC, LANES), 0)

    misc_t = misc_ref[...].T
    qi = qi_ref[...].astype(f32)
    qi_heads, w_heads = [], []
    for h in range(IDX_HEADS):
        qp = qi[:, (h // 2) * LANES:(h // 2 + 1) * LANES]
        qi_heads.append(_bf(jnp.where(left, qp, 0.0) if h % 2 == 0 else jnp.where(left, 0.0, qp)))
        w_heads.append(misc_t[MISC_WI + h:MISC_WI + h + 1, :] * IDX_SCALE)
    qi_pairs = [jnp.concatenate(qi_heads[h:h + 2], axis=0) for h in range(0, IDX_HEADS, 2)]

    def score_body(c, carry):
        row0 = pl.multiple_of(c * KC, KC)
        ki_c = ki_ref[pl.ds(row0, KC), :]
        sc = jnp.zeros((KC, LANES), f32)
        for hp, qi_pair in enumerate(qi_pairs):
            s2 = lax.dot_general(ki_c, qi_pair, _NT, preferred_element_type=f32)
            for j in range(2):
                s = s2[:, j * LANES:(j + 1) * LANES]
                sc = sc + jnp.maximum(s, 0.0) * w_heads[2 * hp + j]
        sc = jnp.where(sc == 0.0, 0.0, sc)
        bits = lax.bitcast_convert_type(sc, i32)
        key = bits ^ ((bits >> 31) & 0x7FFFFFFF)
        key = jnp.where(row0 + sub_iota <= qpos, key, INT_MIN)
        for blk in range(BLOCKS_PER_CHUNK):
            planes = _bit_planes([key[blk * PLANE_BLOCK + SUBLANES * j:blk * PLANE_BLOCK + SUBLANES * (j + 1)]
                                  for j in range(32)])
            planes[0] = planes[0] ^ -1
            for p in range(32):
                plane_ref[p, BLOCKS_PER_CHUNK * c + blk] = planes[p]
        return carry

    lax.fori_loop(0, nch, score_body, 0)

    nblk = plane_ref.shape[1]

    def clear_body(c, carry):
        for blk in range(BLOCKS_PER_CHUNK):
            for p in range(32):
                plane_ref[p, BLOCKS_PER_CHUNK * c + blk] = jnp.zeros((SUBLANES, LANES), i32)
        return carry

    lax.fori_loop(nch, nblk // BLOCKS_PER_CHUNK, clear_body, 0)

    def pass_body(p, carry):
        alive, above, remaining, took = carry
        plane = plane_ref[p]
        hit = alive & plane
        ones = jnp.sum(lax.population_count(hit), axis=0).astype(f32)
        ones = jnp.sum(ones, axis=0, keepdims=True)
        take = ones >= remaining
        drop = jnp.where(take, 0, -1)
        alive = alive & (plane ^ drop)
        above = above | (hit & drop)
        remaining = jnp.where(take, remaining, remaining - ones)
        return alive, above, remaining, jnp.where(take, 1.0, took)

    in_range = lax.broadcasted_iota(i32, (nblk, SUBLANES, LANES), 0) < BLOCKS_PER_CHUNK * nch
    alive, above, need, took = lax.fori_loop(
        0, 32, pass_body, (jnp.where(in_range, -1, 0), jnp.zeros((nblk, SUBLANES, LANES), i32),
                           jnp.full((1, LANES), float(topk), f32), jnp.zeros((1, LANES), f32)))
    need = jnp.where(took > 0.0, need, 0.0)
    bits_ref[0] = alive
    bits_ref[1] = above
    tri = tri_ref[...]

    def unpack(words):
        return jnp.concatenate([(words >> (31 - j)) & 1 for j in range(32)], axis=0).astype(f32)

    def select_bias(c, seen):
        parts = []
        for blk in range(BLOCKS_PER_CHUNK):
            tie = unpack(bits_ref[0, BLOCKS_PER_CHUNK * c + blk])
            rank = seen + jnp.dot(tri, _bf(tie), preferred_element_type=f32)
            seen = rank[PLANE_BLOCK - 1:PLANE_BLOCK]
            sel = unpack(bits_ref[1, BLOCKS_PER_CHUNK * c + blk]) + jnp.where(rank <= need, tie, 0.0)
            parts.append(_bf((sel - 1.0) * (-NEG_BIG)))
        return jnp.concatenate(parts, axis=0), seen

    q = q_ref[...].astype(f32)
    hpg = ATT_HEADS // ATT_KV_HEADS
    eye = jnp.where(lax.broadcasted_iota(i32, (LANES, LANES), 0) == lax.broadcasted_iota(i32, (LANES, LANES), 1),
                    1.0, 0.0)
    q_stacks = []
    for g in range(ATT_KV_HEADS):
        parts = []
        for hh in range(hpg):
            h = g * hpg + hh
            qp = q[:, (h // 2) * LANES:(h // 2 + 1) * LANES]
            parts.append(jnp.where(left, qp, 0.0) if h % 2 == 0 else jnp.where(left, 0.0, qp))
        q_stacks.append(_bf(jnp.concatenate([jnp.concatenate(parts, axis=0), jnp.concatenate([eye] * hpg, axis=0)],
                                            axis=1)))
    ones_lo = _bf(jnp.where(lax.broadcasted_iota(i32, (LANES, KC), 0) < ATT_HEAD_DIM, 0.0, 1.0))
    m_ref[...] = jnp.full(m_ref.shape, NEG_BIG, f32)
    acc_ref[...] = jnp.zeros(acc_ref.shape, f32)

    def chunk_start(c):
        return c * KC if isinstance(c, int) else pl.multiple_of(c * KC, KC)

    def logits(c, slot, seen):
        row0 = chunk_start(c)
        bias, seen = select_bias(c, seen)
        for g in range(ATT_KV_HEADS):
            k2 = jnp.concatenate([k_ref[pl.ds(row0, KC), g * LANES:(g + 1) * LANES], bias], axis=1)
            s_ref[slot, g] = lax.dot_general(k2, q_stacks[g], _NT, preferred_element_type=f32)
        return seen

    def softmax_pv(c, slot):
        for g in range(ATT_KV_HEADS):
            v1 = vt_ref[c, g * LANES:(g + 1) * LANES, :] + ones_lo
            s = s_ref[slot, g]
            m_prev = m_ref[g]
            m_new = jnp.maximum(m_prev, jnp.max(s, axis=0, keepdims=True))
            alpha = jnp.exp2(m_prev - m_new)
            p = jnp.exp2(s - m_new)
            acc_ref[g] = alpha * acc_ref[g] + jnp.dot(v1, _bf(p), preferred_element_type=f32)
            m_ref[g] = m_new

    def step(c, slot, seen):
        seen = logits(c + 1, 1 - slot, seen)
        softmax_pv(c, slot)
        return seen

    def pair_body(j, seen):
        return step(2 * j + 1, 1, step(2 * j, 0, seen))

    npairs = (nch - 1) // 2
    seen = lax.fori_loop(0, npairs, pair_body, logits(0, 0, jnp.zeros((1, LANES), f32)))

    @pl.when(nch - 1 > 2 * npairs)
    def _():
        step(nch - 2, 0, seen)
        softmax_pv(nch - 1, 1)

    @pl.when(nch - 1 == 2 * npairs)
    def _():
        softmax_pv(nch - 1, 0)

    outs = []
    for g in range(ATT_KV_HEADS):
        acc = acc_ref[g]
        heads = []
        for hh in range(hpg):
            blk = acc[:, hh * Q_TILE:(hh + 1) * Q_TILE].T
            heads.append(blk / pltpu.roll(blk, ATT_HEAD_DIM, axis=1))
        for pp in range(hpg // 2):
            outs.append(jnp.where(left, heads[2 * pp], pltpu.roll(heads[2 * pp + 1], ATT_HEAD_DIM, axis=1)))
    o_ref[...] = _bf(jnp.concatenate(outs, axis=1))


def _dsa(q, qi, misc, kv, ki, batch, seq):
    nq = seq // Q_TILE
    nkc = seq // KEY_CHUNK
    topk = min(TOPK_MAX, seq // 4)
    hpg = ATT_HEADS // ATT_KV_HEADS
    qrow = lambda w: pl.BlockSpec((Q_TILE, w), lambda b, i: (b * nq + i, 0))
    seq_spec = lambda w: pl.BlockSpec((seq, w), lambda b, i: (b, 0))
    tri = _bf(jnp.tril(jnp.ones((TIE_BLOCK, TIE_BLOCK), jnp.float32)))
    half = SEG_KV // 2
    vt = kv[:, half:].reshape(batch * nkc, KEY_CHUNK, half).transpose(0, 2, 1)
    s_bytes = 2 * ATT_KV_HEADS * hpg * Q_TILE * KEY_CHUNK * 4
    est = (2 * seq * (SEG_KV + SEG_KI) * 2 + nkc * KEY_CHUNK * LANES * 4 + s_bytes
           + ATT_KV_HEADS * (LANES + SUBLANES) * hpg * Q_TILE * 4 + 12 * hpg * Q_TILE * KEY_CHUNK * 4)
    return pl.pallas_call(
        functools.partial(_dsa_kernel, topk=topk),
        grid=(batch, nq),
        in_specs=[qrow(SEG_Q), qrow(SEG_QI), qrow(SEG_MISC), seq_spec(half),
                  pl.BlockSpec((nkc, half, KEY_CHUNK), lambda b, i: (b, 0, 0)), seq_spec(SEG_KI),
                  pl.BlockSpec((TIE_BLOCK, TIE_BLOCK), lambda b, i: (0, 0))],
        out_specs=qrow(ATT_WIDTH),
        out_shape=jax.ShapeDtypeStruct((batch * seq, ATT_WIDTH), jnp.bfloat16),
        scratch_shapes=[pltpu.VMEM((32, nkc * BLOCKS_PER_CHUNK, SUBLANES, LANES), jnp.int32),
                        pltpu.VMEM((2, nkc * BLOCKS_PER_CHUNK, SUBLANES, LANES), jnp.int32),
                        pltpu.VMEM((2, ATT_KV_HEADS, KEY_CHUNK, hpg * Q_TILE), jnp.float32),
                        pltpu.VMEM((ATT_KV_HEADS, 1, hpg * Q_TILE), jnp.float32),
                        pltpu.VMEM((ATT_KV_HEADS, LANES, hpg * Q_TILE), jnp.float32)],
        compiler_params=_params(("parallel", "arbitrary"), est),
        name="dsa",
    )(q, qi, misc, kv, vt, ki, tri)


def _merge_kernel(yssd_ref, yatt_ref, u_ref, uh_ref, g_ref, h_ref, pw_ref, ps_ref, wus_ref, wua_ref, wup_ref,
                  wo_ref, o_ref, ubuf, *, tiles_per_seq):
    f32 = jnp.float32
    tm = u_ref.shape[0]
    it = pl.program_id(0) % tiles_per_seq
    ubuf[POOL_HALO:POOL_HALO + tm, :] = u_ref[...]
    ubuf[0:POOL_HALO, :] = jnp.where(it == 0, 0.0, uh_ref[...])
    t1 = (it * tm + 1 + lax.broadcasted_iota(jnp.int32, (tm, POOL_GROUP_DIM), 0)).astype(f32)
    mixed = []
    for gi, win in enumerate(POOL_WINDOWS):
        cols = slice(gi * POOL_GROUP_DIM, (gi + 1) * POOL_GROUP_DIM)
        cur = ubuf[POOL_HALO:POOL_HALO + tm, cols]
        s = cur
        for d in range(1, win):
            s = s + ubuf[POOL_HALO - d:POOL_HALO - d + tm, cols]
        pooled = s / jnp.minimum(t1, float(win)) - cur
        mixed.append(jnp.dot(_bf(pooled), pw_ref[gi], preferred_element_type=f32))
    ypool = _bf(jnp.concatenate(mixed, axis=1) * ps_ref[...])

    g = g_ref[...]
    merged = (_sigmoid(g[:, :D_MODEL]) * jnp.dot(yssd_ref[...], wus_ref[...], preferred_element_type=f32)
              + _sigmoid(g[:, D_MODEL:2 * D_MODEL]) * jnp.dot(yatt_ref[...], wua_ref[...], preferred_element_type=f32)
              + _sigmoid(g[:, 2 * D_MODEL:]) * jnp.dot(ypool, wup_ref[...], preferred_element_type=f32))
    o_ref[...] = h_ref[...] + jnp.dot(_bf(merged), wo_ref[...], preferred_element_type=f32)


def _merge(y_ssd, y_att, u, gates, h, pool_w, pool_scale, w_up_ssd, w_up_attn, w_up_pool, w_out, l, seq):
    n = h.shape[0]
    tm = ROW_TILE
    tiles_per_seq = seq // tm
    row = lambda w: pl.BlockSpec((tm, w), lambda r: (r, 0))
    halo = pl.BlockSpec((POOL_HALO, POOL_WIDTH), lambda r: (jnp.maximum(r * (tm // POOL_HALO) - 1, 0), 0))
    w_bytes = (POOL_GROUPS * POOL_GROUP_DIM ** 2 + (SSD_INNER + ATT_WIDTH + POOL_WIDTH + D_MODEL) * D_MODEL) * 2
    est = (w_bytes + 2 * tm * ((SSD_INNER + ATT_WIDTH) * 2 + (POOL_WIDTH + SEG_G + 2 * D_MODEL) * 4)
           + (tm + POOL_HALO) * POOL_WIDTH * 4 + 12 * tm * D_MODEL * 4)
    return pl.pallas_call(
        functools.partial(_merge_kernel, tiles_per_seq=tiles_per_seq),
        grid=(n // tm,),
        in_specs=[row(SSD_INNER), row(ATT_WIDTH), row(POOL_WIDTH), halo, row(SEG_G), row(D_MODEL),
                  _layer_spec((POOL_GROUPS, POOL_GROUP_DIM, POOL_GROUP_DIM), l), _layer_spec((1, POOL_WIDTH), l),
                  _layer_spec((SSD_INNER, D_MODEL), l), _layer_spec((ATT_WIDTH, D_MODEL), l),
                  _layer_spec((POOL_WIDTH, D_MODEL), l), _layer_spec((D_MODEL, D_MODEL), l)],
        out_specs=row(D_MODEL),
        out_shape=jax.ShapeDtypeStruct((n, D_MODEL), jnp.float32),
        scratch_shapes=[pltpu.VMEM((tm + POOL_HALO, POOL_WIDTH), jnp.float32)],
        compiler_params=_params(("parallel",), est),
        name="merge",
    )(y_ssd, y_att, u, u, gates, h, pool_w, pool_scale, w_up_ssd, w_up_attn, w_up_pool, w_out)


def _ffn_kernel(h_ref, nw_ref, wi_ref, wo_ref, fw_ref, o_ref, *, final):
    f32 = jnp.float32
    h = h_ref[...]
    hn = _bf(_rms(h, nw_ref[...]))
    blk = FFN_HIDDEN // FFN_SPLIT
    out = h
    for j in range(FFN_SPLIT):
        a = jnp.dot(hn, wi_ref[:, j * blk:(j + 1) * blk], preferred_element_type=f32)
        b = jnp.dot(hn, wi_ref[:, FFN_HIDDEN + j * blk:FFN_HIDDEN + (j + 1) * blk], preferred_element_type=f32)
        out = out + jnp.dot(_bf(a * _sigmoid(a) * b), wo_ref[j * blk:(j + 1) * blk, :], preferred_element_type=f32)
    if final:
        out = _rms(out, fw_ref[...])
    o_ref[...] = out


def _ffn(h, norm_w, w_ffn_in, w_ffn_out, final_w, l, final):
    n = h.shape[0]
    tm = ROW_TILE
    row = pl.BlockSpec((tm, D_MODEL), lambda r: (r, 0))
    est = (3 * D_MODEL * FFN_HIDDEN * 2 + 4 * tm * D_MODEL * 4 + 6 * tm * (FFN_HIDDEN // FFN_SPLIT) * 4
           + 4 * tm * D_MODEL * 4)
    return pl.pallas_call(
        functools.partial(_ffn_kernel, final=final),
        grid=(n // tm,),
        in_specs=[row, _layer_spec((1, D_MODEL), l), _layer_spec((D_MODEL, 2 * FFN_HIDDEN), l),
                  _layer_spec((FFN_HIDDEN, D_MODEL), l),
                  pl.BlockSpec((1, D_MODEL), lambda r: (0, 0))],
        out_specs=row,
        out_shape=jax.ShapeDtypeStruct((n, D_MODEL), jnp.float32),
        compiler_params=_params(("parallel",), est),
        name="ffn",
    )(h, norm_w, w_ffn_in, w_ffn_out, final_w)


def _pack_w_in(w_in):
    offs, acc = [], 0
    for s in (SSD_INNER, SSD_CONV_DIM, SSD_HEADS, ATT_WIDTH, ATT_KV_HEADS * ATT_HEAD_DIM,
              ATT_KV_HEADS * ATT_HEAD_DIM, IDX_HEADS * IDX_HEAD_DIM, IDX_HEAD_DIM, IDX_HEADS, POOL_WIDTH):
        acc += s
        offs.append(acc)
    z, xbc, dt, q, k, v, qi, ki, wi, u, gates = jnp.split(w_in, offs, axis=-1)
    q = q * (ATT_SCALE * LOG2_E)
    dh = ATT_HEAD_DIM
    kv = [k[..., g * dh:(g + 1) * dh] for g in range(ATT_KV_HEADS) for _ in range(2)]
    for g in range(ATT_KV_HEADS):
        kv += [v[..., g * dh:(g + 1) * dh], jnp.zeros_like(v[..., :dh])]
    pad = jnp.zeros(w_in.shape[:-1] + (SEG_MISC - SSD_HEADS - IDX_HEADS,), w_in.dtype)
    return _bf(jnp.concatenate([z, xbc, q] + kv + [qi, ki, ki, dt, wi, pad, u, gates], axis=-1))


def _pad_lanes(a):
    depth, w = a.shape
    return jnp.pad(a, ((0, 0), (0, LANES - w))).reshape(depth, 1, LANES)


def kernel(x, norm1_w, w_in, conv_w, conv_b, dt_bias, a_log, d_skip, ssd_norm_w, pool_w, pool_scale, w_up_ssd,
           w_up_attn, w_up_pool, w_out, norm2_w, w_ffn_in, w_ffn_out, final_norm_w):
    batch, seq, d = x.shape
    assert d == D_MODEL and seq % KEY_CHUNK == 0 and seq % ROW_TILE == 0 and (batch * seq) % ROW_TILE == 0
    depth = w_in.shape[0]
    row3 = lambda a: a.reshape(depth, 1, a.shape[-1])

    wcat = _pack_w_in(w_in)
    norm1 = row3(norm1_w)
    conv_b3 = row3(conv_b)
    dtb = _pad_lanes(dt_bias)
    alog = _pad_lanes(a_log)
    dskip_x = row3(jnp.repeat(d_skip, SSD_HEAD_DIM, axis=-1))
    ssd_nw = row3(ssd_norm_w)
    pool_s = row3(pool_scale)
    norm2 = row3(norm2_w)
    final_w = final_norm_w.reshape(1, D_MODEL)
    pool_wb, wus, wua, wup, wo, wfi, wfo = map(_bf, (pool_w, w_up_ssd, w_up_attn, w_up_pool, w_out, w_ffn_in,
                                                     w_ffn_out))

    h = x.reshape(batch * seq, D_MODEL)
    for l in range(depth):
        z, xbc, q, kv, qi, ki, misc, u, gates = _inproj(h, norm1, wcat, l)
        y_ssd = _ssd(xbc, z, misc, conv_w, conv_b3, dtb, alog, dskip_x, ssd_nw, l, batch, seq)
        y_att = _dsa(q, qi, misc, kv, ki, batch, seq)
        h = _merge(y_ssd, y_att, u, gates, h, pool_wb, pool_s, wus, wua, wup, wo, l, seq)
        h = _ffn(h, norm2, wfi, wfo, final_w, l, final=(l == depth - 1))
    return h.reshape(batch, seq, D_MODEL)
```

```python
import functools

import jax
import jax.numpy as jnp
from jax import lax
from jax.experimental import pallas as pl
from jax.experimental.pallas import tpu as pltpu

D_MODEL = 1024
DEPTH = 4
SSD_HEADS = 16
SSD_HEAD_DIM = 64
SSD_INNER = SSD_HEADS * SSD_HEAD_DIM
SSD_GROUPS = 2
SSD_STATE = 128
SSD_CONV_DIM = SSD_INNER + 2 * SSD_GROUPS * SSD_STATE
CONV_WIDTH = 4
SSD_CHUNK = 128
ATT_HEADS = 8
ATT_KV_HEADS = 2
ATT_HEAD_DIM = 64
ATT_WIDTH = ATT_HEADS * ATT_HEAD_DIM
IDX_HEADS = 4
IDX_HEAD_DIM = 64
TOPK_MAX = 256
POOL_WINDOWS = (2, 4, 8, 16)
POOL_GROUPS = 4
POOL_GROUP_DIM = 128
POOL_WIDTH = POOL_GROUPS * POOL_GROUP_DIM
N_BRANCH = 3
FFN_HIDDEN = 2816
EPS = 1e-6
ATT_SCALE = ATT_HEAD_DIM ** -0.5
LOG2_E = 1.4426950408889634
IDX_SCALE = (IDX_HEADS ** -0.5) * (IDX_HEAD_DIM ** -0.5)

LANES = 128
SUBLANES = 8
VMEM_BUDGET_BYTES = 60000 * 1024

ROW_TILE = 512
Q_TILE = 2 * LANES
KEY_CHUNK = 512
POOL_HALO = 16
FFN_SPLIT = 2
PLANE_BLOCK = 32 * SUBLANES
BLOCKS_PER_CHUNK = KEY_CHUNK // PLANE_BLOCK
TIE_BLOCK = PLANE_BLOCK

SEG_Z = SSD_INNER
SEG_XBC = SSD_CONV_DIM
SEG_Q = ATT_WIDTH
SEG_KV = 2 * ATT_KV_HEADS * 2 * ATT_HEAD_DIM
SEG_QI = IDX_HEADS * IDX_HEAD_DIM
SEG_KI = 2 * IDX_HEAD_DIM
SEG_MISC = LANES
SEG_U = POOL_WIDTH
SEG_G = N_BRANCH * D_MODEL
SEGS = (SEG_Z, SEG_XBC, SEG_Q, SEG_KV, SEG_QI, SEG_KI, SEG_MISC, SEG_U, SEG_G)
W_CAT = sum(SEGS)
MISC_DT = 0
MISC_WI = SSD_HEADS

INT_MIN = -(2 ** 31)
NEG_BIG = -1e30

_NT = (((1,), (1,)), ((), ()))


def _bf(x):
    return x.astype(jnp.bfloat16)


def _sigmoid(x):
    return 1.0 / (1.0 + jnp.exp(-x))


def _rms(x, w):
    return x * lax.rsqrt(jnp.mean(x * x, axis=-1, keepdims=True) + EPS) * w


def _params(sem, est_bytes):
    return pltpu.CompilerParams(dimension_semantics=sem,
                                vmem_limit_bytes=int(min(VMEM_BUDGET_BYTES, est_bytes)))


def _layer_spec(shape, l):
    nd = len(shape)
    return pl.BlockSpec((None,) + tuple(shape), lambda *_: (l,) + (0,) * nd, pipeline_mode=pl.Buffered(1))


def _inproj_kernel(h_ref, nw_ref, w_ref, *out_refs):
    xn = _bf(_rms(h_ref[...], nw_ref[...]))
    off = 0
    for ref, width in zip(out_refs, SEGS):
        r = jnp.dot(xn, w_ref[:, off:off + width], preferred_element_type=jnp.float32)
        ref[...] = r.astype(ref.dtype)
        off += width


def _inproj(h, norm_w, wcat, l):
    n = h.shape[0]
    tm = ROW_TILE
    dts = (jnp.float32, jnp.float32, jnp.bfloat16, jnp.bfloat16, jnp.bfloat16, jnp.bfloat16,
           jnp.float32, jnp.float32, jnp.float32)
    out_shape = tuple(jax.ShapeDtypeStruct((n, w), dt) for w, dt in zip(SEGS, dts))
    out_specs = tuple(pl.BlockSpec((tm, w), lambda i: (i, 0)) for w in SEGS)
    out_bytes = sum(tm * w * jnp.dtype(dt).itemsize for w, dt in zip(SEGS, dts))
    est = D_MODEL * W_CAT * 2 + 2 * tm * D_MODEL * 4 + 2 * out_bytes + 3 * tm * SEG_G * 4
    return pl.pallas_call(
        _inproj_kernel,
        grid=(n // tm,),
        in_specs=[pl.BlockSpec((tm, D_MODEL), lambda i: (i, 0)),
                  _layer_spec((1, D_MODEL), l),
                  _layer_spec((D_MODEL, W_CAT), l)],
        out_specs=out_specs,
        out_shape=out_shape,
        compiler_params=_params(("parallel",), est),
        name="inproj",
    )(h, norm_w, wcat)


def _col(a, j, width=LANES):
    return jnp.broadcast_to(a[:, j:j + 1], (a.shape[0], width))


def _expand_heads(a, spread):
    hi = _bf(a)
    lo = _bf(a - hi.astype(jnp.float32))
    return (jnp.dot(hi, spread, preferred_element_type=jnp.float32)
            + jnp.dot(lo, spread, preferred_element_type=jnp.float32))


def _silu(x):
    h = 0.5 * x
    return h + h * jnp.tanh(h)


def _ssd_kernel(xbc_ref, z_ref, misc_ref, cw_ref, cb_ref, dtb_ref, alog_ref, dskip_ref, nw_ref, spread_ref, o_ref,
                cbuf, state):
    L = SSD_CHUNK
    f32 = jnp.float32

    @pl.when(pl.program_id(1) == 0)
    def _():
        cbuf[0:SUBLANES, :] = jnp.zeros((SUBLANES, SSD_CONV_DIM), f32)
        state[...] = jnp.zeros(state.shape, f32)

    cbuf[SUBLANES:SUBLANES + L, :] = xbc_ref[...]
    conv = cb_ref[...]
    for k in range(CONV_WIDTH):
        start = SUBLANES - (CONV_WIDTH - 1) + k
        conv = conv + cw_ref[k:k + 1, :] * cbuf[start:start + L, :]
    cbuf[0:SUBLANES, :] = cbuf[L:L + SUBLANES, :]
    xbc = _silu(conv)
    xs = xbc[:, :SSD_INNER]

    lane = lax.broadcasted_iota(jnp.int32, (L, LANES), 1)
    row = lax.broadcasted_iota(jnp.int32, (L, LANES), 0)
    left = lane < SSD_HEAD_DIM
    causal = lane <= row
    tri = jnp.where(causal, 1.0, 0.0)

    dtr = misc_ref[...] + dtb_ref[...]
    dt = jnp.maximum(dtr, 0.0) + jnp.log1p(jnp.exp(-jnp.abs(dtr)))
    a = dt * (-jnp.exp(alog_ref[...]))
    a_cs = jnp.dot(tri, a, preferred_element_type=f32, precision=lax.Precision.HIGHEST)
    a_cs_t = a_cs.T
    ea = jnp.exp(a_cs)
    decay = jnp.exp(a_cs[L - 1:L, :] - a_cs)

    spread = spread_ref[...]
    x_all = xs * _expand_heads(dt, spread)
    ea_x = _expand_heads(ea, spread)
    xd_all = x_all * _expand_heads(decay, spread)

    gw = SSD_INNER // SSD_GROUPS
    y_parts = []
    for g in range(SSD_GROUPS):
        b_g = xbc[:, SSD_INNER + g * SSD_STATE:SSD_INNER + (g + 1) * SSD_STATE]
        c_off = SSD_INNER + SSD_GROUPS * SSD_STATE
        c_g = xbc[:, c_off + g * SSD_STATE:c_off + (g + 1) * SSD_STATE]
        cb = lax.dot_general(_bf(c_g), _bf(b_g), _NT, preferred_element_type=f32)
        st = state[g]
        y_off = jnp.dot(_bf(c_g), _bf(st), preferred_element_type=f32)
        new_st = jnp.dot(_bf(b_g.T), _bf(xd_all[:, g * gw:(g + 1) * gw]), preferred_element_type=f32)
        state[g] = st * ea_x[L - 1:L, g * gw:(g + 1) * gw] + new_st
        pairs = []
        for mm in range(SSD_HEADS // SSD_GROUPS // 2):
            m = g * (SSD_HEADS // SSD_GROUPS // 2) + mm
            lhs = []
            for hd in (2 * m, 2 * m + 1):
                seg = _col(a_cs, hd) - a_cs_t[hd:hd + 1, :]
                lhs.append(cb * jnp.where(causal, jnp.exp(seg), 0.0))
            xp = x_all[:, m * LANES:(m + 1) * LANES]
            rhs = jnp.concatenate([jnp.where(left, xp, 0.0), jnp.where(left, 0.0, xp)], axis=0)
            pairs.append(jnp.dot(_bf(jnp.concatenate(lhs, axis=1)), _bf(rhs), preferred_element_type=f32))
        y_parts.append(jnp.concatenate(pairs, axis=1) + y_off * ea_x[:, g * gw:(g + 1) * gw])
    y = jnp.concatenate(y_parts, axis=1) + dskip_ref[...] * xs
    o_ref[...] = _bf(_rms(y * _silu(z_ref[...]), nw_ref[...]))


def _ssd(xbc, z, misc, conv_w, conv_b, dtb, alog, dskip_x, norm_w, l, batch, seq):
    L = SSD_CHUNK
    nc = seq // L
    row_spec = lambda w: pl.BlockSpec((L, w), lambda b, c: (b * nc + c, 0))
    head_of_col = lax.broadcasted_iota(jnp.int32, (LANES, SSD_INNER), 1) // SSD_HEAD_DIM
    spread = _bf(head_of_col == lax.broadcasted_iota(jnp.int32, (LANES, SSD_INNER), 0))
    est = (2 * L * (SSD_CONV_DIM + SSD_INNER + LANES) * 4 + 2 * L * SSD_INNER * 2
           + (L + 2 * SUBLANES) * SSD_CONV_DIM * 4 + SSD_GROUPS * SSD_STATE * SSD_INNER // SSD_GROUPS * 4
           + 40 * L * SSD_INNER * 4)
    return pl.pallas_call(
        _ssd_kernel,
        grid=(batch, nc),
        in_specs=[row_spec(SSD_CONV_DIM), row_spec(SSD_INNER), row_spec(LANES),
                  _layer_spec((CONV_WIDTH, SSD_CONV_DIM), l), _layer_spec((1, SSD_CONV_DIM), l),
                  _layer_spec((1, LANES), l), _layer_spec((1, LANES), l),
                  _layer_spec((1, SSD_INNER), l), _layer_spec((1, SSD_INNER), l),
                  pl.BlockSpec((LANES, SSD_INNER), lambda b, c: (0, 0))],
        out_specs=row_spec(SSD_INNER),
        out_shape=jax.ShapeDtypeStruct((batch * seq, SSD_INNER), jnp.bfloat16),
        scratch_shapes=[pltpu.VMEM((L + 2 * SUBLANES, SSD_CONV_DIM), jnp.float32),
                        pltpu.VMEM((SSD_GROUPS, SSD_STATE, SSD_INNER // SSD_GROUPS), jnp.float32)],
        compiler_params=_params(("parallel", "arbitrary"), est),
        name="ssd",
    )(xbc, z, misc, conv_w, conv_b, dtb, alog, dskip_x, norm_w, spread)


def _bit_planes(a):
    a = list(a)
    j, m = 16, 0x0000FFFF
    while j:
        for k in range(32):
            if not k & j:
                t = (a[k] ^ (a[k + j] >> j)) & m
                a[k] = a[k] ^ t
                a[k + j] = a[k + j] ^ (t << j)
        j >>= 1
        m ^= m << j
    return a


def _dsa_kernel(q_ref, qi_ref, misc_ref, k_ref, vt_ref, ki_ref, tri_ref, o_ref, plane_ref, bits_ref, s_ref, m_ref,
                acc_ref, *, topk):
    f32 = jnp.float32
    i32 = jnp.int32
    KC = KEY_CHUNK
    i = pl.program_id(1)
    nch = (i * Q_TILE + Q_TILE + KC - 1) // KC
    QT = Q_TILE
    qpos = i * QT + lax.broadcasted_iota(i32, (1, QT), 1)
    left = lax.broadcasted_iota(i32, (1, LANES), 1) < ATT_HEAD_DIM
    sub_iota = lax.broadcasted_iota(i32, (KC, QT), 0)

    misc_t = misc_ref[...].T
    qi = qi_ref[...].astype(f32)
    qi_heads, w_heads = [], []
    for h in range(IDX_HEADS):
        qp = qi[:, (h // 2) * LANES:(h // 2 + 1) * LANES]
        qi_heads.append(_bf(jnp.where(left, qp, 0.0) if h % 2 == 0 else jnp.where(left, 0.0, qp)))
        w_heads.append(misc_t[MISC_WI + h:MISC_WI + h + 1, :] * IDX_SCALE)
    qi_pairs = [jnp.concatenate(qi_heads[h:h + 2], axis=0) for h in range(0, IDX_HEADS, 2)]

    def score_body(c, carry):
        row0 = pl.multiple_of(c * KC, KC)
        ki_c = ki_ref[pl.ds(row0, KC), :]
        sc = jnp.zeros((KC, QT), f32)
        for hp, qi_pair in enumerate(qi_pairs):
            s2 = lax.dot_general(ki_c, qi_pair, _NT, preferred_element_type=f32)
            for j in range(2):
                s = s2[:, j * QT:(j + 1) * QT]
                sc = sc + jnp.maximum(s, 0.0) * w_heads[2 * hp + j]
        sc = jnp.where(sc == 0.0, 0.0, sc)
        bits = lax.bitcast_convert_type(sc, i32)
        key = bits ^ ((bits >> 31) & 0x7FFFFFFF)
        key = jnp.where(row0 + sub_iota <= qpos, key, INT_MIN)
        for blk in range(BLOCKS_PER_CHUNK):
            planes = _bit_planes([key[blk * PLANE_BLOCK + SUBLANES * j:blk * PLANE_BLOCK + SUBLANES * (j + 1)]
                                  for j in range(32)])
            planes[0] = planes[0] ^ -1
            for p in range(32):
                plane_ref[p, BLOCKS_PER_CHUNK * c + blk] = planes[p]
        return carry

    lax.fori_loop(0, nch, score_body, 0)

    nblk = plane_ref.shape[1]

    def clear_body(c, carry):
        for blk in range(BLOCKS_PER_CHUNK):
            for p in range(32):
                plane_ref[p, BLOCKS_PER_CHUNK * c + blk] = jnp.zeros((SUBLANES, QT), i32)
        return carry

    lax.fori_loop(nch, nblk // BLOCKS_PER_CHUNK, clear_body, 0)

    def pass_body(p, carry):
        remaining, took = carry
        plane = plane_ref[p]
        alive = bits_ref[0]
        hit = alive & plane
        ones = jnp.sum(lax.population_count(hit), axis=0).astype(f32)
        ones = jnp.sum(ones, axis=0, keepdims=True)
        take = ones >= remaining
        drop = jnp.where(take, 0, -1)
        bits_ref[0] = alive & (plane ^ drop)
        bits_ref[1] = bits_ref[1] | (hit & drop)
        return jnp.where(take, remaining, remaining - ones), jnp.where(take, 1.0, took)

    in_range = lax.broadcasted_iota(i32, (nblk, SUBLANES, QT), 0) < BLOCKS_PER_CHUNK * nch
    bits_ref[0] = jnp.where(in_range, -1, 0)
    bits_ref[1] = jnp.zeros((nblk, SUBLANES, QT), i32)
    need, took = lax.fori_loop(0, 32, pass_body, (jnp.full((1, QT), float(topk), f32), jnp.zeros((1, QT), f32)))
    need = jnp.where(took > 0.0, need, 0.0)
    tri = tri_ref[...]

    def unpack(words):
        return jnp.concatenate([(words >> (31 - j)) & 1 for j in range(32)], axis=0).astype(f32)

    def select_bias(c, seen):
        parts = []
        for blk in range(BLOCKS_PER_CHUNK):
            tie = unpack(bits_ref[0, BLOCKS_PER_CHUNK * c + blk])
            rank = seen + jnp.dot(tri, _bf(tie), preferred_element_type=f32)
            seen = rank[PLANE_BLOCK - 1:PLANE_BLOCK]
            sel = unpack(bits_ref[1, BLOCKS_PER_CHUNK * c + blk]) + jnp.where(rank <= need, tie, 0.0)
            parts.append((sel - 1.0) * (-NEG_BIG))
        return jnp.concatenate(parts, axis=0), seen

    q = q_ref[...].astype(f32)
    hpg = ATT_HEADS // ATT_KV_HEADS
    q_stacks = []
    for g in range(ATT_KV_HEADS):
        parts = []
        for hh in range(hpg):
            h = g * hpg + hh
            qp = q[:, (h // 2) * LANES:(h // 2 + 1) * LANES]
            parts.append(jnp.where(left, qp, 0.0) if h % 2 == 0 else jnp.where(left, 0.0, qp))
        q_stacks.append(_bf(jnp.concatenate(parts, axis=0)))
    ones_lo = _bf(jnp.where(lax.broadcasted_iota(i32, (LANES, KC), 0) < ATT_HEAD_DIM, 0.0, 1.0))
    m_ref[...] = jnp.full(m_ref.shape, NEG_BIG, f32)
    acc_ref[...] = jnp.zeros(acc_ref.shape, f32)

    def chunk_start(c):
        return c * KC if isinstance(c, int) else pl.multiple_of(c * KC, KC)

    def logits(c, slot, seen):
        row0 = chunk_start(c)
        bias, seen = select_bias(c, seen)
        bias = jnp.concatenate([bias] * hpg, axis=1)
        for g in range(ATT_KV_HEADS):
            k2 = k_ref[pl.ds(row0, KC), g * LANES:(g + 1) * LANES]
            s_ref[slot, g] = lax.dot_general(k2, q_stacks[g], _NT, preferred_element_type=f32) + bias
        return seen

    def softmax_pv(c, slot):
        for g in range(ATT_KV_HEADS):
            v1 = vt_ref[c, g * LANES:(g + 1) * LANES, :] + ones_lo
            s = s_ref[slot, g]
            m_prev = m_ref[g]
            m_new = jnp.maximum(m_prev, jnp.max(s, axis=0, keepdims=True))
            alpha = jnp.exp2(m_prev - m_new)
            p = jnp.exp2(s - m_new)
            acc_ref[g] = alpha * acc_ref[g] + jnp.dot(v1, _bf(p), preferred_element_type=f32)
            m_ref[g] = m_new

    def step(c, slot, seen):
        seen = logits(c + 1, 1 - slot, seen)
        softmax_pv(c, slot)
        return seen

    def pair_body(j, seen):
        return step(2 * j + 1, 1, step(2 * j, 0, seen))

    npairs = (nch - 1) // 2
    seen = lax.fori_loop(0, npairs, pair_body, logits(0, 0, jnp.zeros((1, QT), f32)))

    @pl.when(nch - 1 > 2 * npairs)
    def _():
        step(nch - 2, 0, seen)
        softmax_pv(nch - 1, 1)

    @pl.when(nch - 1 == 2 * npairs)
    def _():
        softmax_pv(nch - 1, 0)

    for qt in range(QT // LANES):
        outs = []
        for g in range(ATT_KV_HEADS):
            heads = []
            for hh in range(hpg):
                col = hh * QT + qt * LANES
                blk = acc_ref[g, :, col:col + LANES].T
                heads.append(blk / pltpu.roll(blk, ATT_HEAD_DIM, axis=1))
            for pp in range(hpg // 2):
                outs.append(jnp.where(left, heads[2 * pp], pltpu.roll(heads[2 * pp + 1], ATT_HEAD_DIM, axis=1)))
        o_ref[qt * LANES:(qt + 1) * LANES, :] = _bf(jnp.concatenate(outs, axis=1))


def _dsa(q, qi, misc, kv, ki, batch, seq):
    nq = seq // Q_TILE
    nkc = seq // KEY_CHUNK
    topk = min(TOPK_MAX, seq // 4)
    hpg = ATT_HEADS // ATT_KV_HEADS
    qrow = lambda w: pl.BlockSpec((Q_TILE, w), lambda b, i: (b * nq + i, 0))
    seq_spec = lambda w: pl.BlockSpec((seq, w), lambda b, i: (b, 0))
    tri = _bf(jnp.tril(jnp.ones((TIE_BLOCK, TIE_BLOCK), jnp.float32)))
    half = SEG_KV // 2
    vt = kv[:, half:].reshape(batch * nkc, KEY_CHUNK, half).transpose(0, 2, 1)
    s_bytes = 2 * ATT_KV_HEADS * hpg * Q_TILE * KEY_CHUNK * 4
    est = (2 * seq * (SEG_KV + SEG_KI) * 2 + nkc * KEY_CHUNK * Q_TILE * 4 + s_bytes
           + ATT_KV_HEADS * (LANES + SUBLANES) * hpg * Q_TILE * 4 + 8 * hpg * Q_TILE * KEY_CHUNK * 4)
    return pl.pallas_call(
        functools.partial(_dsa_kernel, topk=topk),
        grid=(batch, nq),
        in_specs=[qrow(SEG_Q), qrow(SEG_QI), qrow(SEG_MISC), seq_spec(half),
                  pl.BlockSpec((nkc, half, KEY_CHUNK), lambda b, i: (b, 0, 0)), seq_spec(SEG_KI),
                  pl.BlockSpec((TIE_BLOCK, TIE_BLOCK), lambda b, i: (0, 0))],
        out_specs=qrow(ATT_WIDTH),
        out_shape=jax.ShapeDtypeStruct((batch * seq, ATT_WIDTH), jnp.bfloat16),
        scratch_shapes=[pltpu.VMEM((32, nkc * BLOCKS_PER_CHUNK, SUBLANES, Q_TILE), jnp.int32),
                        pltpu.VMEM((2, nkc * BLOCKS_PER_CHUNK, SUBLANES, Q_TILE), jnp.int32),
                        pltpu.VMEM((2, ATT_KV_HEADS, KEY_CHUNK, hpg * Q_TILE), jnp.float32),
                        pltpu.VMEM((ATT_KV_HEADS, 1, hpg * Q_TILE), jnp.float32),
                        pltpu.VMEM((ATT_KV_HEADS, LANES, hpg * Q_TILE), jnp.float32)],
        compiler_params=_params(("parallel", "arbitrary"), est),
        name="dsa",
    )(q, qi, misc, kv, vt, ki, tri)


def _merge_kernel(yssd_ref, yatt_ref, u_ref, uh_ref, g_ref, h_ref, pw_ref, ps_ref, wus_ref, wua_ref, wup_ref,
                  wo_ref, o_ref, ubuf, *, tiles_per_seq):
    f32 = jnp.float32
    tm = u_ref.shape[0]
    it = pl.program_id(0) % tiles_per_seq
    ubuf[POOL_HALO:POOL_HALO + tm, :] = u_ref[...]
    ubuf[0:POOL_HALO, :] = jnp.where(it == 0, 0.0, uh_ref[...])
    t1 = (it * tm + 1 + lax.broadcasted_iota(jnp.int32, (tm, POOL_GROUP_DIM), 0)).astype(f32)
    mixed = []
    for gi, win in enumerate(POOL_WINDOWS):
        cols = slice(gi * POOL_GROUP_DIM, (gi + 1) * POOL_GROUP_DIM)
        cur = ubuf[POOL_HALO:POOL_HALO + tm, cols]
        s = cur
        for d in range(1, win):
            s = s + ubuf[POOL_HALO - d:POOL_HALO - d + tm, cols]
        pooled = s / jnp.minimum(t1, float(win)) - cur
        mixed.append(jnp.dot(_bf(pooled), pw_ref[gi], preferred_element_type=f32))
    ypool = _bf(jnp.concatenate(mixed, axis=1) * ps_ref[...])

    g = g_ref[...]
    merged = (_sigmoid(g[:, :D_MODEL]) * jnp.dot(yssd_ref[...], wus_ref[...], preferred_element_type=f32)
              + _sigmoid(g[:, D_MODEL:2 * D_MODEL]) * jnp.dot(yatt_ref[...], wua_ref[...], preferred_element_type=f32)
              + _sigmoid(g[:, 2 * D_MODEL:]) * jnp.dot(ypool, wup_ref[...], preferred_element_type=f32))
    o_ref[...] = h_ref[...] + jnp.dot(_bf(merged), wo_ref[...], preferred_element_type=f32)


def _merge(y_ssd, y_att, u, gates, h, pool_w, pool_scale, w_up_ssd, w_up_attn, w_up_pool, w_out, l, seq):
    n = h.shape[0]
    tm = ROW_TILE
    tiles_per_seq = seq // tm
    row = lambda w: pl.BlockSpec((tm, w), lambda r: (r, 0))
    halo = pl.BlockSpec((POOL_HALO, POOL_WIDTH), lambda r: (jnp.maximum(r * (tm // POOL_HALO) - 1, 0), 0))
    w_bytes = (POOL_GROUPS * POOL_GROUP_DIM ** 2 + (SSD_INNER + ATT_WIDTH + POOL_WIDTH + D_MODEL) * D_MODEL) * 2
    est = (w_bytes + 2 * tm * ((SSD_INNER + ATT_WIDTH) * 2 + (POOL_WIDTH + SEG_G + 2 * D_MODEL) * 4)
           + (tm + POOL_HALO) * POOL_WIDTH * 4 + 12 * tm * D_MODEL * 4)
    return pl.pallas_call(
        functools.partial(_merge_kernel, tiles_per_seq=tiles_per_seq),
        grid=(n // tm,),
        in_specs=[row(SSD_INNER), row(ATT_WIDTH), row(POOL_WIDTH), halo, row(SEG_G), row(D_MODEL),
                  _layer_spec((POOL_GROUPS, POOL_GROUP_DIM, POOL_GROUP_DIM), l), _layer_spec((1, POOL_WIDTH), l),
                  _layer_spec((SSD_INNER, D_MODEL), l), _layer_spec((ATT_WIDTH, D_MODEL), l),
                  _layer_spec((POOL_WIDTH, D_MODEL), l), _layer_spec((D_MODEL, D_MODEL), l)],
        out_specs=row(D_MODEL),
        out_shape=jax.ShapeDtypeStruct((n, D_MODEL), jnp.float32),
        scratch_shapes=[pltpu.VMEM((tm + POOL_HALO, POOL_WIDTH), jnp.float32)],
        compiler_params=_params(("parallel",), est),
        name="merge",
    )(y_ssd, y_att, u, u, gates, h, pool_w, pool_scale, w_up_ssd, w_up_attn, w_up_pool, w_out)


def _ffn_kernel(h_ref, nw_ref, wi_ref, wo_ref, fw_ref, o_ref, *, final):
    f32 = jnp.float32
    h = h_ref[...]
    hn = _bf(_rms(h, nw_ref[...]))
    blk = FFN_HIDDEN // FFN_SPLIT
    out = h
    for j in range(FFN_SPLIT):
        a = jnp.dot(hn, wi_ref[:, j * blk:(j + 1) * blk], preferred_element_type=f32)
        b = jnp.dot(hn, wi_ref[:, FFN_HIDDEN + j * blk:FFN_HIDDEN + (j + 1) * blk], preferred_element_type=f32)
        out = out + jnp.dot(_bf(a * _sigmoid(a) * b), wo_ref[j * blk:(j + 1) * blk, :], preferred_element_type=f32)
    if final:
        out = _rms(out, fw_ref[...])
    o_ref[...] = out


def _ffn(h, norm_w, w_ffn_in, w_ffn_out, final_w, l, final):
    n = h.shape[0]
    tm = ROW_TILE
    row = pl.BlockSpec((tm, D_MODEL), lambda r: (r, 0))
    est = (3 * D_MODEL * FFN_HIDDEN * 2 + 4 * tm * D_MODEL * 4 + 6 * tm * (FFN_HIDDEN // FFN_SPLIT) * 4
           + 4 * tm * D_MODEL * 4)
    return pl.pallas_call(
        functools.partial(_ffn_kernel, final=final),
        grid=(n // tm,),
        in_specs=[row, _layer_spec((1, D_MODEL), l), _layer_spec((D_MODEL, 2 * FFN_HIDDEN), l),
                  _layer_spec((FFN_HIDDEN, D_MODEL), l),
                  pl.BlockSpec((1, D_MODEL), lambda r: (0, 0))],
        out_specs=row,
        out_shape=jax.ShapeDtypeStruct((n, D_MODEL), jnp.float32),
        compiler_params=_params(("parallel",), est),
        name="ffn",
    )(h, norm_w, w_ffn_in, w_ffn_out, final_w)


def _pack_w_in(w_in):
    offs, acc = [], 0
    for s in (SSD_INNER, SSD_CONV_DIM, SSD_HEADS, ATT_WIDTH, ATT_KV_HEADS * ATT_HEAD_DIM,
              ATT_KV_HEADS * ATT_HEAD_DIM, IDX_HEADS * IDX_HEAD_DIM, IDX_HEAD_DIM, IDX_HEADS, POOL_WIDTH):
        acc += s
        offs.append(acc)
    z, xbc, dt, q, k, v, qi, ki, wi, u, gates = jnp.split(w_in, offs, axis=-1)
    q = q * (ATT_SCALE * LOG2_E)
    dh = ATT_HEAD_DIM
    kv = [k[..., g * dh:(g + 1) * dh] for g in range(ATT_KV_HEADS) for _ in range(2)]
    for g in range(ATT_KV_HEADS):
        kv += [v[..., g * dh:(g + 1) * dh], jnp.zeros_like(v[..., :dh])]
    pad = jnp.zeros(w_in.shape[:-1] + (SEG_MISC - SSD_HEADS - IDX_HEADS,), w_in.dtype)
    return _bf(jnp.concatenate([z, xbc, q] + kv + [qi, ki, ki, dt, wi, pad, u, gates], axis=-1))


def _pad_lanes(a):
    depth, w = a.shape
    return jnp.pad(a, ((0, 0), (0, LANES - w))).reshape(depth, 1, LANES)


def kernel(x, norm1_w, w_in, conv_w, conv_b, dt_bias, a_log, d_skip, ssd_norm_w, pool_w, pool_scale, w_up_ssd,
           w_up_attn, w_up_pool, w_out, norm2_w, w_ffn_in, w_ffn_out, final_norm_w):
    batch, seq, d = x.shape
    assert d == D_MODEL and seq % KEY_CHUNK == 0 and seq % ROW_TILE == 0 and (batch * seq) % ROW_TILE == 0
    depth = w_in.shape[0]
    row3 = lambda a: a.reshape(depth, 1, a.shape[-1])

    wcat = _pack_w_in(w_in)
    norm1 = row3(norm1_w)
    conv_b3 = row3(conv_b)
    dtb = _pad_lanes(dt_bias)
    alog = _pad_lanes(a_log)
    dskip_x = row3(jnp.repeat(d_skip, SSD_HEAD_DIM, axis=-1))
    ssd_nw = row3(ssd_norm_w)
    pool_s = row3(pool_scale)
    norm2 = row3(norm2_w)
    final_w = final_norm_w.reshape(1, D_MODEL)
    pool_wb, wus, wua, wup, wo, wfi, wfo = map(_bf, (pool_w, w_up_ssd, w_up_attn, w_up_pool, w_out, w_ffn_in,
                                                     w_ffn_out))

    h = x.reshape(batch * seq, D_MODEL)
    for l in range(depth):
        z, xbc, q, kv, qi, ki, misc, u, gates = _inproj(h, norm1, wcat, l)
        y_ssd = _ssd(xbc, z, misc, conv_w, conv_b3, dtb, alog, dskip_x, ssd_nw, l, batch, seq)
        y_att = _dsa(q, qi, misc, kv, ki, batch, seq)
        h = _merge(y_ssd, y_att, u, gates, h, pool_wb, pool_s, wus, wua, wup, wo, l, seq)
        h = _ffn(h, norm2, wfi, wfo, final_w, l, final=(l == depth - 1))
    return h.reshape(batch, seq, D_MODEL)
```

```python
import functools

import jax
import jax.numpy as jnp
from jax import lax
from jax.experimental import pallas as pl
from jax.experimental.pallas import tpu as pltpu

D_MODEL = 1024
DEPTH = 4
SSD_HEADS = 16
SSD_HEAD_DIM = 64
SSD_INNER = SSD_HEADS * SSD_HEAD_DIM
SSD_GROUPS = 2
SSD_STATE = 128
SSD_CONV_DIM = SSD_INNER + 2 * SSD_GROUPS * SSD_STATE
CONV_WIDTH = 4
SSD_CHUNK = 128
ATT_HEADS = 8
ATT_KV_HEADS = 2
ATT_HEAD_DIM = 64
ATT_WIDTH = ATT_HEADS * ATT_HEAD_DIM
IDX_HEADS = 4
IDX_HEAD_DIM = 64
TOPK_MAX = 256
POOL_WINDOWS = (2, 4, 8, 16)
POOL_GROUPS = 4
POOL_GROUP_DIM = 128
POOL_WIDTH = POOL_GROUPS * POOL_GROUP_DIM
N_BRANCH = 3
FFN_HIDDEN = 2816
EPS = 1e-6
ATT_SCALE = ATT_HEAD_DIM ** -0.5
LOG2_E = 1.4426950408889634
IDX_SCALE = (IDX_HEADS ** -0.5) * (IDX_HEAD_DIM ** -0.5)

LANES = 128
SUBLANES = 8
VMEM_BUDGET_BYTES = 60000 * 1024

ROW_TILE = 512
Q_TILE = 2 * LANES
KEY_CHUNK = 512
POOL_HALO = 16
FFN_SPLIT = 2
SSD_STEP_CHUNKS = 2
PLANE_BLOCK = 32 * SUBLANES
BLOCKS_PER_CHUNK = KEY_CHUNK // PLANE_BLOCK
TIE_BLOCK = PLANE_BLOCK

SEG_Z = SSD_INNER
SEG_XBC = SSD_CONV_DIM
SEG_Q = ATT_WIDTH
SEG_KV = 2 * ATT_KV_HEADS * 2 * ATT_HEAD_DIM
SEG_QI = IDX_HEADS * IDX_HEAD_DIM
SEG_KI = 2 * IDX_HEAD_DIM
SEG_MISC = LANES
SEG_U = POOL_WIDTH
SEG_G = N_BRANCH * D_MODEL
SEGS = (SEG_Z, SEG_XBC, SEG_Q, SEG_KV, SEG_QI, SEG_KI, SEG_MISC, SEG_U, SEG_G)
W_CAT = sum(SEGS)
MISC_DT = 0
MISC_WI = SSD_HEADS

INT_MIN = -(2 ** 31)
NEG_BIG = -1e30

_NT = (((1,), (1,)), ((), ()))


def _bf(x):
    return x.astype(jnp.bfloat16)


def _sigmoid(x):
    return 1.0 / (1.0 + jnp.exp(-x))


def _rms(x, w):
    return x * lax.rsqrt(jnp.mean(x * x, axis=-1, keepdims=True) + EPS) * w


def _params(sem, est_bytes):
    return pltpu.CompilerParams(dimension_semantics=sem,
                                vmem_limit_bytes=int(min(VMEM_BUDGET_BYTES, est_bytes)))


def _layer_spec(shape, l):
    nd = len(shape)
    return pl.BlockSpec((None,) + tuple(shape), lambda *_: (l,) + (0,) * nd, pipeline_mode=pl.Buffered(1))


def _inproj_kernel(h_ref, nw_ref, w_ref, *out_refs):
    xn = _bf(_rms(h_ref[...], nw_ref[...]))
    off = 0
    for ref, width in zip(out_refs, SEGS):
        r = jnp.dot(xn, w_ref[:, off:off + width], preferred_element_type=jnp.float32)
        ref[...] = r.astype(ref.dtype)
        off += width


def _inproj(h, norm_w, wcat, l):
    n = h.shape[0]
    tm = ROW_TILE
    dts = (jnp.float32, jnp.float32, jnp.bfloat16, jnp.bfloat16, jnp.bfloat16, jnp.bfloat16,
           jnp.float32, jnp.float32, jnp.bfloat16)
    out_shape = tuple(jax.ShapeDtypeStruct((n, w), dt) for w, dt in zip(SEGS, dts))
    out_specs = tuple(pl.BlockSpec((tm, w), lambda i: (i, 0)) for w in SEGS)
    out_bytes = sum(tm * w * jnp.dtype(dt).itemsize for w, dt in zip(SEGS, dts))
    est = D_MODEL * W_CAT * 2 + 2 * tm * D_MODEL * 4 + 2 * out_bytes + 3 * tm * SEG_G * 4
    return pl.pallas_call(
        _inproj_kernel,
        grid=(n // tm,),
        in_specs=[pl.BlockSpec((tm, D_MODEL), lambda i: (i, 0)),
                  _layer_spec((1, D_MODEL), l),
                  _layer_spec((D_MODEL, W_CAT), l)],
        out_specs=out_specs,
        out_shape=out_shape,
        compiler_params=_params(("parallel",), est),
        name="inproj",
    )(h, norm_w, wcat)


def _col(a, j, width=LANES):
    return jnp.broadcast_to(a[:, j:j + 1], (a.shape[0], width))


def _expand_heads(a, spread):
    hi = _bf(a)
    lo = _bf(a - hi.astype(jnp.float32))
    return (jnp.dot(hi, spread, preferred_element_type=jnp.float32)
            + jnp.dot(lo, spread, preferred_element_type=jnp.float32))


def _silu(x):
    h = 0.5 * x
    return h + h * jnp.tanh(h)


def _ssd_kernel(xbc_ref, z_ref, misc_ref, cw_ref, cb_ref, dtb_ref, alog_ref, dskip_ref, nw_ref, spread_ref, o_ref,
                cbuf, state):
    L = SSD_CHUNK
    f32 = jnp.float32

    @pl.when(pl.program_id(1) == 0)
    def _():
        cbuf[0:SUBLANES, :] = jnp.zeros((SUBLANES, SSD_CONV_DIM), f32)
        state[...] = jnp.zeros(state.shape, f32)

    lane = lax.broadcasted_iota(jnp.int32, (L, LANES), 1)
    row = lax.broadcasted_iota(jnp.int32, (L, LANES), 0)
    left = lane < SSD_HEAD_DIM
    causal = lane <= row
    tri = jnp.where(causal, 1.0, 0.0)
    for sub in range(SSD_STEP_CHUNKS):
        _ssd_chunk(slice(sub * L, (sub + 1) * L), left, causal, tri, xbc_ref, z_ref, misc_ref, cw_ref, cb_ref, dtb_ref,
                   alog_ref, dskip_ref, nw_ref, spread_ref, o_ref, cbuf, state)


def _ssd_chunk(rows, left, causal, tri, xbc_ref, z_ref, misc_ref, cw_ref, cb_ref, dtb_ref, alog_ref, dskip_ref, nw_ref,
               spread_ref, o_ref, cbuf, state):
    L = SSD_CHUNK
    f32 = jnp.float32
    cbuf[SUBLANES:SUBLANES + L, :] = xbc_ref[rows, :]
    conv = cb_ref[...]
    for k in range(CONV_WIDTH):
        start = SUBLANES - (CONV_WIDTH - 1) + k
        conv = conv + cw_ref[k:k + 1, :] * cbuf[start:start + L, :]
    cbuf[0:SUBLANES, :] = cbuf[L:L + SUBLANES, :]
    xbc = _silu(conv)
    xs = xbc[:, :SSD_INNER]

    dtr = misc_ref[rows, :] + dtb_ref[...]
    dt = jnp.maximum(dtr, 0.0) + jnp.log1p(jnp.exp(-jnp.abs(dtr)))
    a = dt * (-jnp.exp(alog_ref[...]))
    a_cs = jnp.dot(tri, a, preferred_element_type=f32, precision=lax.Precision.HIGHEST)
    a_cs_t = a_cs.T
    ea = jnp.exp(a_cs)
    decay = jnp.exp(a_cs[L - 1:L, :] - a_cs)

    spread = spread_ref[...]
    x_all = xs * _expand_heads(dt, spread)
    ea_x = _expand_heads(ea, spread)
    xd_all = x_all * _expand_heads(decay, spread)

    gw = SSD_INNER // SSD_GROUPS
    y_parts = []
    for g in range(SSD_GROUPS):
        b_g = xbc[:, SSD_INNER + g * SSD_STATE:SSD_INNER + (g + 1) * SSD_STATE]
        c_off = SSD_INNER + SSD_GROUPS * SSD_STATE
        c_g = xbc[:, c_off + g * SSD_STATE:c_off + (g + 1) * SSD_STATE]
        cb = lax.dot_general(_bf(c_g), _bf(b_g), _NT, preferred_element_type=f32)
        st = state[g]
        y_off = jnp.dot(_bf(c_g), _bf(st), preferred_element_type=f32)
        new_st = jnp.dot(_bf(b_g.T), _bf(xd_all[:, g * gw:(g + 1) * gw]), preferred_element_type=f32)
        state[g] = st * ea_x[L - 1:L, g * gw:(g + 1) * gw] + new_st
        pairs = []
        for mm in range(SSD_HEADS // SSD_GROUPS // 2):
            m = g * (SSD_HEADS // SSD_GROUPS // 2) + mm
            lhs = []
            for hd in (2 * m, 2 * m + 1):
                seg = _col(a_cs, hd) - a_cs_t[hd:hd + 1, :]
                lhs.append(cb * jnp.where(causal, jnp.exp(seg), 0.0))
            xp = x_all[:, m * LANES:(m + 1) * LANES]
            rhs = jnp.concatenate([jnp.where(left, xp, 0.0), jnp.where(left, 0.0, xp)], axis=0)
            pairs.append(jnp.dot(_bf(jnp.concatenate(lhs, axis=1)), _bf(rhs), preferred_element_type=f32))
        y_parts.append(jnp.concatenate(pairs, axis=1) + y_off * ea_x[:, g * gw:(g + 1) * gw])
    y = jnp.concatenate(y_parts, axis=1) + dskip_ref[...] * xs
    o_ref[rows, :] = _bf(_rms(y * _silu(z_ref[rows, :]), nw_ref[...]))


def _ssd(xbc, z, misc, conv_w, conv_b, dtb, alog, dskip_x, norm_w, l, batch, seq):
    L = SSD_CHUNK * SSD_STEP_CHUNKS
    nc = seq // L
    row_spec = lambda w: pl.BlockSpec((L, w), lambda b, c: (b * nc + c, 0))
    head_of_col = lax.broadcasted_iota(jnp.int32, (LANES, SSD_INNER), 1) // SSD_HEAD_DIM
    spread = _bf(head_of_col == lax.broadcasted_iota(jnp.int32, (LANES, SSD_INNER), 0))
    est = (2 * L * (SSD_CONV_DIM + SSD_INNER + LANES) * 4 + 2 * L * SSD_INNER * 2
           + (L + 2 * SUBLANES) * SSD_CONV_DIM * 4 + SSD_GROUPS * SSD_STATE * SSD_INNER // SSD_GROUPS * 4
           + 40 * L * SSD_INNER * 4)
    return pl.pallas_call(
        _ssd_kernel,
        grid=(batch, nc),
        in_specs=[row_spec(SSD_CONV_DIM), row_spec(SSD_INNER), row_spec(LANES),
                  _layer_spec((CONV_WIDTH, SSD_CONV_DIM), l), _layer_spec((1, SSD_CONV_DIM), l),
                  _layer_spec((1, LANES), l), _layer_spec((1, LANES), l),
                  _layer_spec((1, SSD_INNER), l), _layer_spec((1, SSD_INNER), l),
                  pl.BlockSpec((LANES, SSD_INNER), lambda b, c: (0, 0))],
        out_specs=row_spec(SSD_INNER),
        out_shape=jax.ShapeDtypeStruct((batch * seq, SSD_INNER), jnp.bfloat16),
        scratch_shapes=[pltpu.VMEM((SSD_CHUNK + 2 * SUBLANES, SSD_CONV_DIM), jnp.float32),
                        pltpu.VMEM((SSD_GROUPS, SSD_STATE, SSD_INNER // SSD_GROUPS), jnp.float32)],
        compiler_params=_params(("parallel", "arbitrary"), est),
        name="ssd",
    )(xbc, z, misc, conv_w, conv_b, dtb, alog, dskip_x, norm_w, spread)


def _bit_planes(a):
    a = list(a)
    j, m = 16, 0x0000FFFF
    while j:
        for k in range(32):
            if not k & j:
                t = (a[k] ^ (a[k + j] >> j)) & m
                a[k] = a[k] ^ t
                a[k + j] = a[k + j] ^ (t << j)
        j >>= 1
        m ^= m << j
    return a


def _dsa_kernel(q_ref, qi_ref, misc_ref, k_ref, vt_ref, ki_ref, tri_ref, o_ref, plane_ref, bits_ref, s_ref, m_ref,
                acc_ref, *, topk):
    f32 = jnp.float32
    i32 = jnp.int32
    KC = KEY_CHUNK
    i = pl.program_id(1)
    nch = (i * Q_TILE + Q_TILE + KC - 1) // KC
    QT = Q_TILE
    qpos = i * QT + lax.broadcasted_iota(i32, (1, QT), 1)
    left = lax.broadcasted_iota(i32, (1, LANES), 1) < ATT_HEAD_DIM
    sub_iota = lax.broadcasted_iota(i32, (KC, QT), 0)

    misc_t = misc_ref[...].T
    qi = qi_ref[...].astype(f32)
    qi_heads, w_heads = [], []
    for h in range(IDX_HEADS):
        qp = qi[:, (h // 2) * LANES:(h // 2 + 1) * LANES]
        qi_heads.append(_bf(jnp.where(left, qp, 0.0) if h % 2 == 0 else jnp.where(left, 0.0, qp)))
        w_heads.append(misc_t[MISC_WI + h:MISC_WI + h + 1, :] * IDX_SCALE)
    qi_pairs = [jnp.concatenate(qi_heads[h:h + 2], axis=0) for h in range(0, IDX_HEADS, 2)]

    def score_body(c, carry):
        row0 = pl.multiple_of(c * KC, KC)
        ki_c = ki_ref[pl.ds(row0, KC), :]
        sc = jnp.zeros((KC, QT), f32)
        for hp, qi_pair in enumerate(qi_pairs):
            s2 = lax.dot_general(ki_c, qi_pair, _NT, preferred_element_type=f32)
            for j in range(2):
                s = s2[:, j * QT:(j + 1) * QT]
                sc = sc + jnp.maximum(s, 0.0) * w_heads[2 * hp + j]
        sc = jnp.where(sc == 0.0, 0.0, sc)
        bits = lax.bitcast_convert_type(sc, i32)
        key = bits ^ ((bits >> 31) & 0x7FFFFFFF)
        key = jnp.where(row0 + sub_iota <= qpos, key, INT_MIN)
        for blk in range(BLOCKS_PER_CHUNK):
            planes = _bit_planes([key[blk * PLANE_BLOCK + SUBLANES * j:blk * PLANE_BLOCK + SUBLANES * (j + 1)]
                                  for j in range(32)])
            planes[0] = planes[0] ^ -1
            for p in range(32):
                plane_ref[p, BLOCKS_PER_CHUNK * c + blk] = planes[p]
        return carry

    lax.fori_loop(0, nch, score_body, 0)

    nblk = plane_ref.shape[1]

    def clear_body(c, carry):
        for blk in range(BLOCKS_PER_CHUNK):
            for p in range(32):
                plane_ref[p, BLOCKS_PER_CHUNK * c + blk] = jnp.zeros((SUBLANES, QT), i32)
        return carry

    lax.fori_loop(nch, nblk // BLOCKS_PER_CHUNK, clear_body, 0)

    def pass_body(p, carry):
        remaining, took = carry
        plane = plane_ref[p]
        alive = bits_ref[0]
        hit = alive & plane
        ones = jnp.sum(lax.population_count(hit), axis=0).astype(f32)
        ones = jnp.sum(ones, axis=0, keepdims=True)
        take = ones >= remaining
        drop = jnp.where(take, 0, -1)
        bits_ref[0] = alive & (plane ^ drop)
        bits_ref[1] = bits_ref[1] | (hit & drop)
        return jnp.where(take, remaining, remaining - ones), jnp.where(take, 1.0, took)

    in_range = lax.broadcasted_iota(i32, (nblk, SUBLANES, QT), 0) < BLOCKS_PER_CHUNK * nch
    bits_ref[0] = jnp.where(in_range, -1, 0)
    bits_ref[1] = jnp.zeros((nblk, SUBLANES, QT), i32)
    need, took = lax.fori_loop(0, 32, pass_body, (jnp.full((1, QT), float(topk), f32), jnp.zeros((1, QT), f32)))
    need = jnp.where(took > 0.0, need, 0.0)
    tri = tri_ref[...]

    def unpack(words):
        return jnp.concatenate([(words >> (31 - j)) & 1 for j in range(32)], axis=0).astype(f32)

    def select_bias(c, seen):
        parts = []
        for blk in range(BLOCKS_PER_CHUNK):
            tie = unpack(bits_ref[0, BLOCKS_PER_CHUNK * c + blk])
            rank = seen + jnp.dot(tri, _bf(tie), preferred_element_type=f32)
            seen = rank[PLANE_BLOCK - 1:PLANE_BLOCK]
            sel = unpack(bits_ref[1, BLOCKS_PER_CHUNK * c + blk]) + jnp.where(rank <= need, tie, 0.0)
            parts.append((sel - 1.0) * (-NEG_BIG))
        return jnp.concatenate(parts, axis=0), seen

    q = q_ref[...].astype(f32)
    hpg = ATT_HEADS // ATT_KV_HEADS
    q_stacks = []
    for g in range(ATT_KV_HEADS):
        parts = []
        for hh in range(hpg):
            h = g * hpg + hh
            qp = q[:, (h // 2) * LANES:(h // 2 + 1) * LANES]
            parts.append(jnp.where(left, qp, 0.0) if h % 2 == 0 else jnp.where(left, 0.0, qp))
        q_stacks.append(_bf(jnp.concatenate(parts, axis=0)))
    ones_lo = _bf(jnp.where(lax.broadcasted_iota(i32, (LANES, KC), 0) < ATT_HEAD_DIM, 0.0, 1.0))
    m_ref[...] = jnp.full(m_ref.shape, NEG_BIG, f32)
    acc_ref[...] = jnp.zeros(acc_ref.shape, f32)

    def chunk_start(c):
        return c * KC if isinstance(c, int) else pl.multiple_of(c * KC, KC)

    def logits(c, slot, seen):
        row0 = chunk_start(c)
        bias, seen = select_bias(c, seen)
        bias = jnp.concatenate([bias] * hpg, axis=1)
        for g in range(ATT_KV_HEADS):
            k2 = k_ref[pl.ds(row0, KC), g * LANES:(g + 1) * LANES]
            s_ref[slot, g] = lax.dot_general(k2, q_stacks[g], _NT, preferred_element_type=f32) + bias
        return seen

    def softmax_pv(c, slot):
        for g in range(ATT_KV_HEADS):
            v1 = vt_ref[c, g * LANES:(g + 1) * LANES, :] + ones_lo
            s = s_ref[slot, g]
            m_prev = m_ref[g]
            m_new = jnp.maximum(m_prev, jnp.max(s, axis=0, keepdims=True))
            alpha = jnp.exp2(m_prev - m_new)
            p = jnp.exp2(s - m_new)
            acc_ref[g] = alpha * acc_ref[g] + jnp.dot(v1, _bf(p), preferred_element_type=f32)
            m_ref[g] = m_new

    def step(c, slot, seen):
        seen = logits(c + 1, 1 - slot, seen)
        softmax_pv(c, slot)
        return seen

    def pair_body(j, seen):
        return step(2 * j + 1, 1, step(2 * j, 0, seen))

    npairs = (nch - 1) // 2
    seen = lax.fori_loop(0, npairs, pair_body, logits(0, 0, jnp.zeros((1, QT), f32)))

    @pl.when(nch - 1 > 2 * npairs)
    def _():
        step(nch - 2, 0, seen)
        softmax_pv(nch - 1, 1)

    @pl.when(nch - 1 == 2 * npairs)
    def _():
        softmax_pv(nch - 1, 0)

    for qt in range(QT // LANES):
        outs = []
        for g in range(ATT_KV_HEADS):
            heads = []
            for hh in range(hpg):
                col = hh * QT + qt * LANES
                blk = acc_ref[g, :, col:col + LANES].T
                heads.append(blk / pltpu.roll(blk, ATT_HEAD_DIM, axis=1))
            for pp in range(hpg // 2):
                outs.append(jnp.where(left, heads[2 * pp], pltpu.roll(heads[2 * pp + 1], ATT_HEAD_DIM, axis=1)))
        o_ref[qt * LANES:(qt + 1) * LANES, :] = _bf(jnp.concatenate(outs, axis=1))


def _dsa(q, qi, misc, kv, ki, batch, seq):
    nq = seq // Q_TILE
    nkc = seq // KEY_CHUNK
    topk = min(TOPK_MAX, seq // 4)
    hpg = ATT_HEADS // ATT_KV_HEADS
    qrow = lambda w: pl.BlockSpec((Q_TILE, w), lambda b, i: (b * nq + i, 0))
    seq_spec = lambda w: pl.BlockSpec((seq, w), lambda b, i: (b, 0))
    tri = _bf(jnp.tril(jnp.ones((TIE_BLOCK, TIE_BLOCK), jnp.float32)))
    half = SEG_KV // 2
    vt = kv[:, half:].reshape(batch * nkc, KEY_CHUNK, half).transpose(0, 2, 1)
    s_bytes = 2 * ATT_KV_HEADS * hpg * Q_TILE * KEY_CHUNK * 4
    est = (2 * seq * (SEG_KV + SEG_KI) * 2 + nkc * KEY_CHUNK * Q_TILE * 4 + s_bytes
           + ATT_KV_HEADS * (LANES + SUBLANES) * hpg * Q_TILE * 4 + 8 * hpg * Q_TILE * KEY_CHUNK * 4)
    return pl.pallas_call(
        functools.partial(_dsa_kernel, topk=topk),
        grid=(batch, nq),
        in_specs=[qrow(SEG_Q), qrow(SEG_QI), qrow(SEG_MISC), seq_spec(half),
                  pl.BlockSpec((nkc, half, KEY_CHUNK), lambda b, i: (b, 0, 0)), seq_spec(SEG_KI),
                  pl.BlockSpec((TIE_BLOCK, TIE_BLOCK), lambda b, i: (0, 0))],
        out_specs=qrow(ATT_WIDTH),
        out_shape=jax.ShapeDtypeStruct((batch * seq, ATT_WIDTH), jnp.bfloat16),
        scratch_shapes=[pltpu.VMEM((32, nkc * BLOCKS_PER_CHUNK, SUBLANES, Q_TILE), jnp.int32),
                        pltpu.VMEM((2, nkc * BLOCKS_PER_CHUNK, SUBLANES, Q_TILE), jnp.int32),
                        pltpu.VMEM((2, ATT_KV_HEADS, KEY_CHUNK, hpg * Q_TILE), jnp.float32),
                        pltpu.VMEM((ATT_KV_HEADS, 1, hpg * Q_TILE), jnp.float32),
                        pltpu.VMEM((ATT_KV_HEADS, LANES, hpg * Q_TILE), jnp.float32)],
        compiler_params=_params(("parallel", "arbitrary"), est),
        name="dsa",
    )(q, qi, misc, kv, vt, ki, tri)


def _merge_kernel(yssd_ref, yatt_ref, u_ref, uh_ref, g_ref, h_ref, pw_ref, ps_ref, wus_ref, wua_ref, wup_ref,
                  wo_ref, o_ref, ubuf, *, tiles_per_seq):
    f32 = jnp.float32
    tm = u_ref.shape[0]
    it = pl.program_id(0) % tiles_per_seq
    ubuf[POOL_HALO:POOL_HALO + tm, :] = u_ref[...]
    ubuf[0:POOL_HALO, :] = jnp.where(it == 0, 0.0, uh_ref[...])
    t1 = (it * tm + 1 + lax.broadcasted_iota(jnp.int32, (tm, POOL_GROUP_DIM), 0)).astype(f32)
    mixed = []
    for gi, win in enumerate(POOL_WINDOWS):
        cols = slice(gi * POOL_GROUP_DIM, (gi + 1) * POOL_GROUP_DIM)
        cur = ubuf[POOL_HALO:POOL_HALO + tm, cols]
        s = cur
        for d in range(1, win):
            s = s + ubuf[POOL_HALO - d:POOL_HALO - d + tm, cols]
        pooled = s / jnp.minimum(t1, float(win)) - cur
        mixed.append(jnp.dot(_bf(pooled), pw_ref[gi], preferred_element_type=f32))
    ypool = _bf(jnp.concatenate(mixed, axis=1) * ps_ref[...])

    g = g_ref[...].astype(f32)
    merged = (_sigmoid(g[:, :D_MODEL]) * jnp.dot(yssd_ref[...], wus_ref[...], preferred_element_type=f32)
              + _sigmoid(g[:, D_MODEL:2 * D_MODEL]) * jnp.dot(yatt_ref[...], wua_ref[...], preferred_element_type=f32)
              + _sigmoid(g[:, 2 * D_MODEL:]) * jnp.dot(ypool, wup_ref[...], preferred_element_type=f32))
    o_ref[...] = h_ref[...] + jnp.dot(_bf(merged), wo_ref[...], preferred_element_type=f32)


def _merge(y_ssd, y_att, u, gates, h, pool_w, pool_scale, w_up_ssd, w_up_attn, w_up_pool, w_out, l, seq):
    n = h.shape[0]
    tm = ROW_TILE
    tiles_per_seq = seq // tm
    row = lambda w: pl.BlockSpec((tm, w), lambda r: (r, 0))
    halo = pl.BlockSpec((POOL_HALO, POOL_WIDTH), lambda r: (jnp.maximum(r * (tm // POOL_HALO) - 1, 0), 0))
    w_bytes = (POOL_GROUPS * POOL_GROUP_DIM ** 2 + (SSD_INNER + ATT_WIDTH + POOL_WIDTH + D_MODEL) * D_MODEL) * 2
    est = (w_bytes + 2 * tm * ((SSD_INNER + ATT_WIDTH) * 2 + (POOL_WIDTH + SEG_G + 2 * D_MODEL) * 4)
           + (tm + POOL_HALO) * POOL_WIDTH * 4 + 12 * tm * D_MODEL * 4)
    return pl.pallas_call(
        functools.partial(_merge_kernel, tiles_per_seq=tiles_per_seq),
        grid=(n // tm,),
        in_specs=[row(SSD_INNER), row(ATT_WIDTH), row(POOL_WIDTH), halo, row(SEG_G), row(D_MODEL),
                  _layer_spec((POOL_GROUPS, POOL_GROUP_DIM, POOL_GROUP_DIM), l), _layer_spec((1, POOL_WIDTH), l),
                  _layer_spec((SSD_INNER, D_MODEL), l), _layer_spec((ATT_WIDTH, D_MODEL), l),
                  _layer_spec((POOL_WIDTH, D_MODEL), l), _layer_spec((D_MODEL, D_MODEL), l)],
        out_specs=row(D_MODEL),
        out_shape=jax.ShapeDtypeStruct((n, D_MODEL), jnp.float32),
        scratch_shapes=[pltpu.VMEM((tm + POOL_HALO, POOL_WIDTH), jnp.float32)],
        compiler_params=_params(("parallel",), est),
        name="merge",
    )(y_ssd, y_att, u, u, gates, h, pool_w, pool_scale, w_up_ssd, w_up_attn, w_up_pool, w_out)


def _ffn_kernel(h_ref, nw_ref, wi_ref, wo_ref, fw_ref, o_ref, *, final):
    f32 = jnp.float32
    h = h_ref[...]
    hn = _bf(_rms(h, nw_ref[...]))
    blk = FFN_HIDDEN // FFN_SPLIT
    out = h
    for j in range(FFN_SPLIT):
        a = jnp.dot(hn, wi_ref[:, j * blk:(j + 1) * blk], preferred_element_type=f32)
        b = jnp.dot(hn, wi_ref[:, FFN_HIDDEN + j * blk:FFN_HIDDEN + (j + 1) * blk], preferred_element_type=f32)
        out = out + jnp.dot(_bf(a * _sigmoid(a) * b), wo_ref[j * blk:(j + 1) * blk, :], preferred_element_type=f32)
    if final:
        out = _rms(out, fw_ref[...])
    o_ref[...] = out


def _ffn(h, norm_w, w_ffn_in, w_ffn_out, final_w, l, final):
    n = h.shape[0]
    tm = ROW_TILE
    row = pl.BlockSpec((tm, D_MODEL), lambda r: (r, 0))
    est = (3 * D_MODEL * FFN_HIDDEN * 2 + 4 * tm * D_MODEL * 4 + 6 * tm * (FFN_HIDDEN // FFN_SPLIT) * 4
           + 4 * tm * D_MODEL * 4)
    return pl.pallas_call(
        functools.partial(_ffn_kernel, final=final),
        grid=(n // tm,),
        in_specs=[row, _layer_spec((1, D_MODEL), l), _layer_spec((D_MODEL, 2 * FFN_HIDDEN), l),
                  _layer_spec((FFN_HIDDEN, D_MODEL), l),
                  pl.BlockSpec((1, D_MODEL), lambda r: (0, 0))],
        out_specs=row,
        out_shape=jax.ShapeDtypeStruct((n, D_MODEL), jnp.float32),
        compiler_params=_params(("parallel",), est),
        name="ffn",
    )(h, norm_w, w_ffn_in, w_ffn_out, final_w)


def _pack_w_in(w_in):
    offs, acc = [], 0
    for s in (SSD_INNER, SSD_CONV_DIM, SSD_HEADS, ATT_WIDTH, ATT_KV_HEADS * ATT_HEAD_DIM,
              ATT_KV_HEADS * ATT_HEAD_DIM, IDX_HEADS * IDX_HEAD_DIM, IDX_HEAD_DIM, IDX_HEADS, POOL_WIDTH):
        acc += s
        offs.append(acc)
    z, xbc, dt, q, k, v, qi, ki, wi, u, gates = jnp.split(w_in, offs, axis=-1)
    q = q * (ATT_SCALE * LOG2_E)
    dh = ATT_HEAD_DIM
    kv = [k[..., g * dh:(g + 1) * dh] for g in range(ATT_KV_HEADS) for _ in range(2)]
    for g in range(ATT_KV_HEADS):
        kv += [v[..., g * dh:(g + 1) * dh], jnp.zeros_like(v[..., :dh])]
    pad = jnp.zeros(w_in.shape[:-1] + (SEG_MISC - SSD_HEADS - IDX_HEADS,), w_in.dtype)
    return _bf(jnp.concatenate([z, xbc, q] + kv + [qi, ki, ki, dt, wi, pad, u, gates], axis=-1))


def _pad_lanes(a):
    depth, w = a.shape
    return jnp.pad(a, ((0, 0), (0, LANES - w))).reshape(depth, 1, LANES)


def kernel(x, norm1_w, w_in, conv_w, conv_b, dt_bias, a_log, d_skip, ssd_norm_w, pool_w, pool_scale, w_up_ssd,
           w_up_attn, w_up_pool, w_out, norm2_w, w_ffn_in, w_ffn_out, final_norm_w):
    batch, seq, d = x.shape
    assert d == D_MODEL and seq % KEY_CHUNK == 0 and seq % ROW_TILE == 0 and (batch * seq) % ROW_TILE == 0
    assert seq % Q_TILE == 0 and seq % (SSD_CHUNK * SSD_STEP_CHUNKS) == 0
    depth = w_in.shape[0]
    row3 = lambda a: a.reshape(depth, 1, a.shape[-1])

    wcat = _pack_w_in(w_in)
    norm1 = row3(norm1_w)
    conv_b3 = row3(conv_b)
    dtb = _pad_lanes(dt_bias)
    alog = _pad_lanes(a_log)
    dskip_x = row3(jnp.repeat(d_skip, SSD_HEAD_DIM, axis=-1))
    ssd_nw = row3(ssd_norm_w)
    pool_s = row3(pool_scale)
    norm2 = row3(norm2_w)
    final_w = final_norm_w.reshape(1, D_MODEL)
    pool_wb, wus, wua, wup, wo, wfi, wfo = map(_bf, (pool_w, w_up_ssd, w_up_attn, w_up_pool, w_out, w_ffn_in,
                                                     w_ffn_out))

    h = x.reshape(batch * seq, D_MODEL)
    for l in range(depth):
        z, xbc, q, kv, qi, ki, misc, u, gates = _inproj(h, norm1, wcat, l)
        y_ssd = _ssd(xbc, z, misc, conv_w, conv_b3, dtb, alog, dskip_x, ssd_nw, l, batch, seq)
        y_att = _dsa(q, qi, misc, kv, ki, batch, seq)
        h = _merge(y_ssd, y_att, u, gates, h, pool_wb, pool_s, wus, wua, wup, wo, l, seq)
        h = _ffn(h, norm2, wfi, wfo, final_w, l, final=(l == depth - 1))
    return h.reshape(batch, seq, D_MODEL)
```

```python
import functools

import jax
import jax.numpy as jnp
from jax import lax
from jax.experimental import pallas as pl
from jax.experimental.pallas import tpu as pltpu

D_MODEL = 1024
DEPTH = 4
SSD_HEADS = 16
SSD_HEAD_DIM = 64
SSD_INNER = SSD_HEADS * SSD_HEAD_DIM
SSD_GROUPS = 2
SSD_STATE = 128
SSD_CONV_DIM = SSD_INNER + 2 * SSD_GROUPS * SSD_STATE
CONV_WIDTH = 4
SSD_CHUNK = 128
ATT_HEADS = 8
ATT_KV_HEADS = 2
ATT_HEAD_DIM = 64
ATT_WIDTH = ATT_HEADS * ATT_HEAD_DIM
IDX_HEADS = 4
IDX_HEAD_DIM = 64
TOPK_MAX = 256
POOL_WINDOWS = (2, 4, 8, 16)
POOL_GROUPS = 4
POOL_GROUP_DIM = 128
POOL_WIDTH = POOL_GROUPS * POOL_GROUP_DIM
N_BRANCH = 3
FFN_HIDDEN = 2816
EPS = 1e-6
ATT_SCALE = ATT_HEAD_DIM ** -0.5
LOG2_E = 1.4426950408889634
IDX_SCALE = (IDX_HEADS ** -0.5) * (IDX_HEAD_DIM ** -0.5)

LANES = 128
SUBLANES = 8
VMEM_BUDGET_BYTES = 60000 * 1024

ROW_TILE = 512
Q_TILE = 2 * LANES
KEY_CHUNK = 512
POOL_HALO = 16
FFN_SPLIT = 2
SSD_STEP_CHUNKS = 2
PLANE_BLOCK = 32 * SUBLANES
BLOCKS_PER_CHUNK = KEY_CHUNK // PLANE_BLOCK
TIE_BLOCK = PLANE_BLOCK

SEG_Z = SSD_INNER
SEG_XBC = SSD_CONV_DIM
SEG_Q = ATT_WIDTH
SEG_K = ATT_KV_HEADS * 2 * ATT_HEAD_DIM
SEG_V = ATT_KV_HEADS * 2 * ATT_HEAD_DIM
SEG_QI = IDX_HEADS * IDX_HEAD_DIM
SEG_KI = 2 * IDX_HEAD_DIM
SEG_MISC = LANES
SEG_U = POOL_WIDTH
SEG_G = N_BRANCH * D_MODEL
SEGS = (SEG_Z, SEG_XBC, SEG_Q, SEG_K, SEG_V, SEG_QI, SEG_KI, SEG_MISC, SEG_U, SEG_G)
V_SEG = 4
W_CAT = sum(SEGS)
MISC_WI = SSD_HEADS

INT_MIN = -(2 ** 31)
NEG_BIG = -1e30

_NT = (((1,), (1,)), ((), ()))


def _bf(x):
    return x.astype(jnp.bfloat16)


def _sigmoid(x):
    return 1.0 / (1.0 + jnp.exp(-x))


def _rms(x, w):
    return x * lax.rsqrt(jnp.mean(x * x, axis=-1, keepdims=True) + EPS) * w


def _params(sem, est_bytes):
    return pltpu.CompilerParams(dimension_semantics=sem,
                                vmem_limit_bytes=int(min(VMEM_BUDGET_BYTES, est_bytes)))


def _layer_spec(shape, l):
    nd = len(shape)
    return pl.BlockSpec((None,) + tuple(shape), lambda *_: (l,) + (0,) * nd, pipeline_mode=pl.Buffered(1))


def _inproj_kernel(h_ref, nw_ref, w_ref, *refs):
    out_refs, v_buf = refs[:-1], refs[-1]
    xn = _bf(_rms(h_ref[...], nw_ref[...]))
    off = 0
    for s, (ref, width) in enumerate(zip(out_refs, SEGS)):
        r = jnp.dot(xn, w_ref[:, off:off + width], preferred_element_type=jnp.float32)
        if s == V_SEG:
            v_buf[...] = r
            r = v_buf[...].T
        ref[...] = r.astype(ref.dtype)
        off += width


def _inproj(h, norm_w, wcat, l):
    n = h.shape[0]
    tm = ROW_TILE
    dts = (jnp.float32, jnp.float32, jnp.bfloat16, jnp.bfloat16, jnp.bfloat16, jnp.bfloat16, jnp.bfloat16,
           jnp.float32, jnp.float32, jnp.bfloat16)
    out_shape = [jax.ShapeDtypeStruct((n, w), dt) for w, dt in zip(SEGS, dts)]
    out_specs = [pl.BlockSpec((tm, w), lambda i: (i, 0)) for w in SEGS]
    out_shape[V_SEG] = jax.ShapeDtypeStruct((n // tm, SEG_V, tm), dts[V_SEG])
    out_specs[V_SEG] = pl.BlockSpec((None, SEG_V, tm), lambda i: (i, 0, 0))
    out_bytes = sum(tm * w * jnp.dtype(dt).itemsize for w, dt in zip(SEGS, dts))
    est = D_MODEL * W_CAT * 2 + 2 * tm * D_MODEL * 4 + 2 * out_bytes + 3 * tm * SEG_G * 4
    return pl.pallas_call(
        _inproj_kernel,
        grid=(n // tm,),
        in_specs=[pl.BlockSpec((tm, D_MODEL), lambda i: (i, 0)),
                  _layer_spec((1, D_MODEL), l),
                  _layer_spec((D_MODEL, W_CAT), l)],
        out_specs=out_specs,
        out_shape=out_shape,
        scratch_shapes=[pltpu.VMEM((tm, SEG_V), jnp.float32)],
        compiler_params=_params(("parallel",), est),
        name="inproj",
    )(h, norm_w, wcat)


def _col(a, j, width=LANES):
    return jnp.broadcast_to(a[:, j:j + 1], (a.shape[0], width))


def _expand_heads(a, spread):
    hi = _bf(a)
    lo = _bf(a - hi.astype(jnp.float32))
    return (jnp.dot(hi, spread, preferred_element_type=jnp.float32)
            + jnp.dot(lo, spread, preferred_element_type=jnp.float32))


def _silu(x):
    h = 0.5 * x
    return h + h * jnp.tanh(h)


def _ssd_kernel(xbc_ref, z_ref, misc_ref, cw_ref, cb_ref, dtb_ref, alog_ref, dskip_ref, nw_ref, spread_ref, o_ref,
                cbuf, state):
    L = SSD_CHUNK
    f32 = jnp.float32

    @pl.when(pl.program_id(1) == 0)
    def _():
        cbuf[0:SUBLANES, :] = jnp.zeros((SUBLANES, SSD_CONV_DIM), f32)
        state[...] = jnp.zeros(state.shape, f32)

    lane = lax.broadcasted_iota(jnp.int32, (L, LANES), 1)
    row = lax.broadcasted_iota(jnp.int32, (L, LANES), 0)
    left = lane < SSD_HEAD_DIM
    causal = lane <= row
    tri = jnp.where(causal, 1.0, 0.0)
    for sub in range(SSD_STEP_CHUNKS):
        _ssd_chunk(slice(sub * L, (sub + 1) * L), left, causal, tri, xbc_ref, z_ref, misc_ref, cw_ref, cb_ref, dtb_ref,
                   alog_ref, dskip_ref, nw_ref, spread_ref, o_ref, cbuf, state)


def _ssd_chunk(rows, left, causal, tri, xbc_ref, z_ref, misc_ref, cw_ref, cb_ref, dtb_ref, alog_ref, dskip_ref, nw_ref,
               spread_ref, o_ref, cbuf, state):
    L = SSD_CHUNK
    f32 = jnp.float32
    cbuf[SUBLANES:SUBLANES + L, :] = xbc_ref[rows, :]
    conv = cb_ref[...]
    for k in range(CONV_WIDTH):
        start = SUBLANES - (CONV_WIDTH - 1) + k
        conv = conv + cw_ref[k:k + 1, :] * cbuf[start:start + L, :]
    cbuf[0:SUBLANES, :] = cbuf[L:L + SUBLANES, :]
    xbc = _silu(conv)
    xs = xbc[:, :SSD_INNER]

    dtr = misc_ref[rows, :] + dtb_ref[...]
    dt = jnp.maximum(dtr, 0.0) + jnp.log1p(jnp.exp(-jnp.abs(dtr)))
    a = dt * (-jnp.exp(alog_ref[...]))
    a_cs = jnp.dot(tri, a, preferred_element_type=f32, precision=lax.Precision.HIGHEST)
    a_cs_t = a_cs.T
    ea = jnp.exp(a_cs)
    decay = jnp.exp(a_cs[L - 1:L, :] - a_cs)

    spread = spread_ref[...]
    x_all = xs * _expand_heads(dt, spread)
    ea_x = _expand_heads(ea, spread)
    xd_all = x_all * _expand_heads(decay, spread)

    gw = SSD_INNER // SSD_GROUPS
    y_parts = []
    for g in range(SSD_GROUPS):
        b_g = xbc[:, SSD_INNER + g * SSD_STATE:SSD_INNER + (g + 1) * SSD_STATE]
        c_off = SSD_INNER + SSD_GROUPS * SSD_STATE
        c_g = xbc[:, c_off + g * SSD_STATE:c_off + (g + 1) * SSD_STATE]
        cb = lax.dot_general(_bf(c_g), _bf(b_g), _NT, preferred_element_type=f32)
        st = state[g]
        y_off = jnp.dot(_bf(c_g), _bf(st), preferred_element_type=f32)
        new_st = jnp.dot(_bf(b_g.T), _bf(xd_all[:, g * gw:(g + 1) * gw]), preferred_element_type=f32)
        state[g] = st * ea_x[L - 1:L, g * gw:(g + 1) * gw] + new_st
        pairs = []
        for mm in range(SSD_HEADS // SSD_GROUPS // 2):
            m = g * (SSD_HEADS // SSD_GROUPS // 2) + mm
            lhs = []
            for hd in (2 * m, 2 * m + 1):
                seg = _col(a_cs, hd) - a_cs_t[hd:hd + 1, :]
                lhs.append(cb * jnp.where(causal, jnp.exp(seg), 0.0))
            xp = x_all[:, m * LANES:(m + 1) * LANES]
            rhs = jnp.concatenate([jnp.where(left, xp, 0.0), jnp.where(left, 0.0, xp)], axis=0)
            pairs.append(jnp.dot(_bf(jnp.concatenate(lhs, axis=1)), _bf(rhs), preferred_element_type=f32))
        y_parts.append(jnp.concatenate(pairs, axis=1) + y_off * ea_x[:, g * gw:(g + 1) * gw])
    y = jnp.concatenate(y_parts, axis=1) + dskip_ref[...] * xs
    o_ref[rows, :] = _bf(_rms(y * _silu(z_ref[rows, :]), nw_ref[...]))


def _ssd(xbc, z, misc, conv_w, conv_b, dtb, alog, dskip_x, norm_w, l, batch, seq):
    L = SSD_CHUNK * SSD_STEP_CHUNKS
    nc = seq // L
    row_spec = lambda w: pl.BlockSpec((L, w), lambda b, c: (b * nc + c, 0))
    head_of_col = lax.broadcasted_iota(jnp.int32, (LANES, SSD_INNER), 1) // SSD_HEAD_DIM
    spread = _bf(head_of_col == lax.broadcasted_iota(jnp.int32, (LANES, SSD_INNER), 0))
    est = (2 * L * (SSD_CONV_DIM + SSD_INNER + LANES) * 4 + 2 * L * SSD_INNER * 2
           + (L + 2 * SUBLANES) * SSD_CONV_DIM * 4 + SSD_GROUPS * SSD_STATE * SSD_INNER // SSD_GROUPS * 4
           + 40 * L * SSD_INNER * 4)
    return pl.pallas_call(
        _ssd_kernel,
        grid=(batch, nc),
        in_specs=[row_spec(SSD_CONV_DIM), row_spec(SSD_INNER), row_spec(LANES),
                  _layer_spec((CONV_WIDTH, SSD_CONV_DIM), l), _layer_spec((1, SSD_CONV_DIM), l),
                  _layer_spec((1, LANES), l), _layer_spec((1, LANES), l),
                  _layer_spec((1, SSD_INNER), l), _layer_spec((1, SSD_INNER), l),
                  pl.BlockSpec((LANES, SSD_INNER), lambda b, c: (0, 0))],
        out_specs=row_spec(SSD_INNER),
        out_shape=jax.ShapeDtypeStruct((batch * seq, SSD_INNER), jnp.bfloat16),
        scratch_shapes=[pltpu.VMEM((SSD_CHUNK + 2 * SUBLANES, SSD_CONV_DIM), jnp.float32),
                        pltpu.VMEM((SSD_GROUPS, SSD_STATE, SSD_INNER // SSD_GROUPS), jnp.float32)],
        compiler_params=_params(("parallel", "arbitrary"), est),
        name="ssd",
    )(xbc, z, misc, conv_w, conv_b, dtb, alog, dskip_x, norm_w, spread)


def _bit_planes(a):
    a = list(a)
    j, m = 16, 0x0000FFFF
    while j:
        for k in range(32):
            if not k & j:
                t = (a[k] ^ (a[k + j] >> j)) & m
                a[k] = a[k] ^ t
                a[k + j] = a[k + j] ^ (t << j)
        j >>= 1
        m ^= m << j
    return a


def _dsa_kernel(q_ref, qi_ref, misc_ref, k_ref, vt_ref, ki_ref, tri_ref, o_ref, plane_ref, bits_ref, s_ref, m_ref,
                acc_ref, *, topk):
    f32 = jnp.float32
    i32 = jnp.int32
    KC = KEY_CHUNK
    i = pl.program_id(1)
    nch = (i * Q_TILE + Q_TILE + KC - 1) // KC
    QT = Q_TILE
    qpos = i * QT + lax.broadcasted_iota(i32, (1, QT), 1)
    left = lax.broadcasted_iota(i32, (1, LANES), 1) < ATT_HEAD_DIM
    sub_iota = lax.broadcasted_iota(i32, (KC, QT), 0)

    misc_t = misc_ref[...].T
    qi = qi_ref[...].astype(f32)
    qi_heads, w_heads = [], []
    for h in range(IDX_HEADS):
        qp = qi[:, (h // 2) * LANES:(h // 2 + 1) * LANES]
        qi_heads.append(_bf(jnp.where(left, qp, 0.0) if h % 2 == 0 else jnp.where(left, 0.0, qp)))
        w_heads.append(misc_t[MISC_WI + h:MISC_WI + h + 1, :] * IDX_SCALE)
    qi_pairs = [jnp.concatenate(qi_heads[h:h + 2], axis=0) for h in range(0, IDX_HEADS, 2)]

    def score_body(c, carry):
        row0 = pl.multiple_of(c * KC, KC)
        ki_c = ki_ref[pl.ds(row0, KC), :]
        sc = jnp.zeros((KC, QT), f32)
        for hp, qi_pair in enumerate(qi_pairs):
            s2 = lax.dot_general(ki_c, qi_pair, _NT, preferred_element_type=f32)
            for j in range(2):
                s = s2[:, j * QT:(j + 1) * QT]
                sc = sc + jnp.maximum(s, 0.0) * w_heads[2 * hp + j]
        sc = jnp.where(sc == 0.0, 0.0, sc)
        bits = lax.bitcast_convert_type(sc, i32)
        key = bits ^ ((bits >> 31) & 0x7FFFFFFF)
        key = jnp.where(row0 + sub_iota <= qpos, key, INT_MIN)
        for blk in range(BLOCKS_PER_CHUNK):
            planes = _bit_planes([key[blk * PLANE_BLOCK + SUBLANES * j:blk * PLANE_BLOCK + SUBLANES * (j + 1)]
                                  for j in range(32)])
            planes[0] = planes[0] ^ -1
            for p in range(32):
                plane_ref[p, BLOCKS_PER_CHUNK * c + blk] = planes[p]
        return carry

    lax.fori_loop(0, nch, score_body, 0)

    nblk = plane_ref.shape[1]

    def clear_body(c, carry):
        for blk in range(BLOCKS_PER_CHUNK):
            for p in range(32):
                plane_ref[p, BLOCKS_PER_CHUNK * c + blk] = jnp.zeros((SUBLANES, QT), i32)
        return carry

    lax.fori_loop(nch, nblk // BLOCKS_PER_CHUNK, clear_body, 0)

    def pass_body(p, carry):
        remaining, took = carry
        plane = plane_ref[p]
        alive = bits_ref[0]
        hit = alive & plane
        ones = jnp.sum(lax.population_count(hit), axis=0).astype(f32)
        ones = jnp.sum(ones, axis=0, keepdims=True)
        take = ones >= remaining
        drop = jnp.where(take, 0, -1)
        bits_ref[0] = alive & (plane ^ drop)
        bits_ref[1] = bits_ref[1] | (hit & drop)
        return jnp.where(take, remaining, remaining - ones), jnp.where(take, 1.0, took)

    in_range = lax.broadcasted_iota(i32, (nblk, SUBLANES, QT), 0) < BLOCKS_PER_CHUNK * nch
    bits_ref[0] = jnp.where(in_range, -1, 0)
    bits_ref[1] = jnp.zeros((nblk, SUBLANES, QT), i32)
    need, took = lax.fori_loop(0, 32, pass_body, (jnp.full((1, QT), float(topk), f32), jnp.zeros((1, QT), f32)))
    need = jnp.where(took > 0.0, need, 0.0)
    tri = tri_ref[...]

    def unpack(words):
        return jnp.concatenate([(words >> (31 - j)) & 1 for j in range(32)], axis=0).astype(f32)

    def select_bias(c, seen):
        parts = []
        for blk in range(BLOCKS_PER_CHUNK):
            tie = unpack(bits_ref[0, BLOCKS_PER_CHUNK * c + blk])
            rank = seen + jnp.dot(tri, _bf(tie), preferred_element_type=f32)
            seen = rank[PLANE_BLOCK - 1:PLANE_BLOCK]
            sel = unpack(bits_ref[1, BLOCKS_PER_CHUNK * c + blk]) + jnp.where(rank <= need, tie, 0.0)
            parts.append((sel - 1.0) * (-NEG_BIG))
        return jnp.concatenate(parts, axis=0), seen

    q = q_ref[...].astype(f32)
    hpg = ATT_HEADS // ATT_KV_HEADS
    q_stacks = []
    for g in range(ATT_KV_HEADS):
        parts = []
        for hh in range(hpg):
            h = g * hpg + hh
            qp = q[:, (h // 2) * LANES:(h // 2 + 1) * LANES]
            parts.append(jnp.where(left, qp, 0.0) if h % 2 == 0 else jnp.where(left, 0.0, qp))
        q_stacks.append(_bf(jnp.concatenate(parts, axis=0)))
    ones_lo = _bf(jnp.where(lax.broadcasted_iota(i32, (LANES, KC), 0) < ATT_HEAD_DIM, 0.0, 1.0))
    m_ref[...] = jnp.full(m_ref.shape, NEG_BIG, f32)
    acc_ref[...] = jnp.zeros(acc_ref.shape, f32)

    def chunk_start(c):
        return c * KC if isinstance(c, int) else pl.multiple_of(c * KC, KC)

    def logits(c, slot, seen):
        row0 = chunk_start(c)
        bias, seen = select_bias(c, seen)
        bias = jnp.concatenate([bias] * hpg, axis=1)
        for g in range(ATT_KV_HEADS):
            k2 = k_ref[pl.ds(row0, KC), g * LANES:(g + 1) * LANES]
            s_ref[slot, g] = lax.dot_general(k2, q_stacks[g], _NT, preferred_element_type=f32) + bias
        return seen

    def softmax_pv(c, slot):
        for g in range(ATT_KV_HEADS):
            v1 = vt_ref[c, g * LANES:(g + 1) * LANES, :] + ones_lo
            s = s_ref[slot, g]
            m_prev = m_ref[g]
            m_new = jnp.maximum(m_prev, jnp.max(s, axis=0, keepdims=True))
            alpha = jnp.exp2(m_prev - m_new)
            p = jnp.exp2(s - m_new)
            acc_ref[g] = alpha * acc_ref[g] + jnp.dot(v1, _bf(p), preferred_element_type=f32)
            m_ref[g] = m_new

    def step(c, slot, seen):
        seen = logits(c + 1, 1 - slot, seen)
        softmax_pv(c, slot)
        return seen

    def pair_body(j, seen):
        return step(2 * j + 1, 1, step(2 * j, 0, seen))

    npairs = (nch - 1) // 2
    seen = lax.fori_loop(0, npairs, pair_body, logits(0, 0, jnp.zeros((1, QT), f32)))

    @pl.when(nch - 1 > 2 * npairs)
    def _():
        step(nch - 2, 0, seen)
        softmax_pv(nch - 1, 1)

    @pl.when(nch - 1 == 2 * npairs)
    def _():
        softmax_pv(nch - 1, 0)

    for qt in range(QT // LANES):
        outs = []
        for g in range(ATT_KV_HEADS):
            heads = []
            for hh in range(hpg):
                col = hh * QT + qt * LANES
                blk = acc_ref[g, :, col:col + LANES].T
                heads.append(blk / pltpu.roll(blk, ATT_HEAD_DIM, axis=1))
            for pp in range(hpg // 2):
                outs.append(jnp.where(left, heads[2 * pp], pltpu.roll(heads[2 * pp + 1], ATT_HEAD_DIM, axis=1)))
        o_ref[qt * LANES:(qt + 1) * LANES, :] = _bf(jnp.concatenate(outs, axis=1))


def _dsa(q, qi, misc, k, vt, ki, batch, seq):
    nq = seq // Q_TILE
    nkc = seq // KEY_CHUNK
    topk = min(TOPK_MAX, seq // 4)
    hpg = ATT_HEADS // ATT_KV_HEADS
    qrow = lambda w: pl.BlockSpec((Q_TILE, w), lambda b, i: (b * nq + i, 0))
    seq_spec = lambda w: pl.BlockSpec((seq, w), lambda b, i: (b, 0))
    tri = _bf(jnp.tril(jnp.ones((TIE_BLOCK, TIE_BLOCK), jnp.float32)))
    assert vt.shape == (batch * nkc, SEG_V, KEY_CHUNK)
    s_bytes = 2 * ATT_KV_HEADS * hpg * Q_TILE * KEY_CHUNK * 4
    est = (2 * seq * (SEG_K + SEG_V + SEG_KI) * 2 + nkc * KEY_CHUNK * Q_TILE * 4 + s_bytes
           + ATT_KV_HEADS * (LANES + SUBLANES) * hpg * Q_TILE * 4 + 8 * hpg * Q_TILE * KEY_CHUNK * 4)
    return pl.pallas_call(
        functools.partial(_dsa_kernel, topk=topk),
        grid=(batch, nq),
        in_specs=[qrow(SEG_Q), qrow(SEG_QI), qrow(SEG_MISC), seq_spec(SEG_K),
                  pl.BlockSpec((nkc, SEG_V, KEY_CHUNK), lambda b, i: (b, 0, 0)), seq_spec(SEG_KI),
                  pl.BlockSpec((TIE_BLOCK, TIE_BLOCK), lambda b, i: (0, 0))],
        out_specs=qrow(ATT_WIDTH),
        out_shape=jax.ShapeDtypeStruct((batch * seq, ATT_WIDTH), jnp.bfloat16),
        scratch_shapes=[pltpu.VMEM((32, nkc * BLOCKS_PER_CHUNK, SUBLANES, Q_TILE), jnp.int32),
                        pltpu.VMEM((2, nkc * BLOCKS_PER_CHUNK, SUBLANES, Q_TILE), jnp.int32),
                        pltpu.VMEM((2, ATT_KV_HEADS, KEY_CHUNK, hpg * Q_TILE), jnp.float32),
                        pltpu.VMEM((ATT_KV_HEADS, 1, hpg * Q_TILE), jnp.float32),
                        pltpu.VMEM((ATT_KV_HEADS, LANES, hpg * Q_TILE), jnp.float32)],
        compiler_params=_params(("parallel", "arbitrary"), est),
        name="dsa",
    )(q, qi, misc, k, vt, ki, tri)


def _merge_kernel(yssd_ref, yatt_ref, u_ref, uh_ref, g_ref, h_ref, pw_ref, ps_ref, wus_ref, wua_ref, wup_ref,
                  wo_ref, o_ref, ubuf, *, tiles_per_seq):
    f32 = jnp.float32
    tm = u_ref.shape[0]
    it = pl.program_id(0) % tiles_per_seq
    ubuf[POOL_HALO:POOL_HALO + tm, :] = u_ref[...]
    ubuf[0:POOL_HALO, :] = jnp.where(it == 0, 0.0, uh_ref[...])
    t1 = (it * tm + 1 + lax.broadcasted_iota(jnp.int32, (tm, POOL_GROUP_DIM), 0)).astype(f32)
    mixed = []
    for gi, win in enumerate(POOL_WINDOWS):
        cols = slice(gi * POOL_GROUP_DIM, (gi + 1) * POOL_GROUP_DIM)
        cur = ubuf[POOL_HALO:POOL_HALO + tm, cols]
        s = cur
        for d in range(1, win):
            s = s + ubuf[POOL_HALO - d:POOL_HALO - d + tm, cols]
        pooled = s / jnp.minimum(t1, float(win)) - cur
        mixed.append(jnp.dot(_bf(pooled), pw_ref[gi], preferred_element_type=f32))
    ypool = _bf(jnp.concatenate(mixed, axis=1) * ps_ref[...])

    g = g_ref[...].astype(f32)
    merged = (_sigmoid(g[:, :D_MODEL]) * jnp.dot(yssd_ref[...], wus_ref[...], preferred_element_type=f32)
              + _sigmoid(g[:, D_MODEL:2 * D_MODEL]) * jnp.dot(yatt_ref[...], wua_ref[...], preferred_element_type=f32)
              + _sigmoid(g[:, 2 * D_MODEL:]) * jnp.dot(ypool, wup_ref[...], preferred_element_type=f32))
    o_ref[...] = h_ref[...] + jnp.dot(_bf(merged), wo_ref[...], preferred_element_type=f32)


def _merge(y_ssd, y_att, u, gates, h, pool_w, pool_scale, w_up_ssd, w_up_attn, w_up_pool, w_out, l, seq):
    n = h.shape[0]
    tm = ROW_TILE
    tiles_per_seq = seq // tm
    row = lambda w: pl.BlockSpec((tm, w), lambda r: (r, 0))
    halo = pl.BlockSpec((POOL_HALO, POOL_WIDTH), lambda r: (jnp.maximum(r * (tm // POOL_HALO) - 1, 0), 0))
    w_bytes = (POOL_GROUPS * POOL_GROUP_DIM ** 2 + (SSD_INNER + ATT_WIDTH + POOL_WIDTH + D_MODEL) * D_MODEL) * 2
    est = (w_bytes + 2 * tm * ((SSD_INNER + ATT_WIDTH) * 2 + (POOL_WIDTH + SEG_G + 2 * D_MODEL) * 4)
           + (tm + POOL_HALO) * POOL_WIDTH * 4 + 12 * tm * D_MODEL * 4)
    return pl.pallas_call(
        functools.partial(_merge_kernel, tiles_per_seq=tiles_per_seq),
        grid=(n // tm,),
        in_specs=[row(SSD_INNER), row(ATT_WIDTH), row(POOL_WIDTH), halo, row(SEG_G), row(D_MODEL),
                  _layer_spec((POOL_GROUPS, POOL_GROUP_DIM, POOL_GROUP_DIM), l), _layer_spec((1, POOL_WIDTH), l),
                  _layer_spec((SSD_INNER, D_MODEL), l), _layer_spec((ATT_WIDTH, D_MODEL), l),
                  _layer_spec((POOL_WIDTH, D_MODEL), l), _layer_spec((D_MODEL, D_MODEL), l)],
        out_specs=row(D_MODEL),
        out_shape=jax.ShapeDtypeStruct((n, D_MODEL), jnp.float32),
        scratch_shapes=[pltpu.VMEM((tm + POOL_HALO, POOL_WIDTH), jnp.float32)],
        compiler_params=_params(("parallel",), est),
        name="merge",
    )(y_ssd, y_att, u, u, gates, h, pool_w, pool_scale, w_up_ssd, w_up_attn, w_up_pool, w_out)


def _ffn_kernel(h_ref, nw_ref, wi_ref, wo_ref, fw_ref, o_ref, *, final):
    f32 = jnp.float32
    h = h_ref[...]
    hn = _bf(_rms(h, nw_ref[...]))
    blk = FFN_HIDDEN // FFN_SPLIT
    out = h
    for j in range(FFN_SPLIT):
        a = jnp.dot(hn, wi_ref[:, j * blk:(j + 1) * blk], preferred_element_type=f32)
        b = jnp.dot(hn, wi_ref[:, FFN_HIDDEN + j * blk:FFN_HIDDEN + (j + 1) * blk], preferred_element_type=f32)
        out = out + jnp.dot(_bf(a * _sigmoid(a) * b), wo_ref[j * blk:(j + 1) * blk, :], preferred_element_type=f32)
    if final:
        out = _rms(out, fw_ref[...])
    o_ref[...] = out


def _ffn(h, norm_w, w_ffn_in, w_ffn_out, final_w, l, final):
    n = h.shape[0]
    tm = ROW_TILE
    row = pl.BlockSpec((tm, D_MODEL), lambda r: (r, 0))
    est = (3 * D_MODEL * FFN_HIDDEN * 2 + 4 * tm * D_MODEL * 4 + 6 * tm * (FFN_HIDDEN // FFN_SPLIT) * 4
           + 4 * tm * D_MODEL * 4)
    return pl.pallas_call(
        functools.partial(_ffn_kernel, final=final),
        grid=(n // tm,),
        in_specs=[row, _layer_spec((1, D_MODEL), l), _layer_spec((D_MODEL, 2 * FFN_HIDDEN), l),
                  _layer_spec((FFN_HIDDEN, D_MODEL), l),
                  pl.BlockSpec((1, D_MODEL), lambda r: (0, 0))],
        out_specs=row,
        out_shape=jax.ShapeDtypeStruct((n, D_MODEL), jnp.float32),
        compiler_params=_params(("parallel",), est),
        name="ffn",
    )(h, norm_w, w_ffn_in, w_ffn_out, final_w)


def _pack_w_in(w_in):
    offs, acc = [], 0
    for s in (SSD_INNER, SSD_CONV_DIM, SSD_HEADS, ATT_WIDTH, ATT_KV_HEADS * ATT_HEAD_DIM,
              ATT_KV_HEADS * ATT_HEAD_DIM, IDX_HEADS * IDX_HEAD_DIM, IDX_HEAD_DIM, IDX_HEADS, POOL_WIDTH):
        acc += s
        offs.append(acc)
    z, xbc, dt, q, k, v, qi, ki, wi, u, gates = jnp.split(w_in, offs, axis=-1)
    q = q * (ATT_SCALE * LOG2_E)
    dh = ATT_HEAD_DIM
    kv = [k[..., g * dh:(g + 1) * dh] for g in range(ATT_KV_HEADS) for _ in range(2)]
    for g in range(ATT_KV_HEADS):
        kv += [v[..., g * dh:(g + 1) * dh], jnp.zeros_like(v[..., :dh])]
    pad = jnp.zeros(w_in.shape[:-1] + (SEG_MISC - SSD_HEADS - IDX_HEADS,), w_in.dtype)
    return _bf(jnp.concatenate([z, xbc, q] + kv + [qi, ki, ki, dt, wi, pad, u, gates], axis=-1))


def _pad_lanes(a):
    depth, w = a.shape
    return jnp.pad(a, ((0, 0), (0, LANES - w))).reshape(depth, 1, LANES)


def kernel(x, norm1_w, w_in, conv_w, conv_b, dt_bias, a_log, d_skip, ssd_norm_w, pool_w, pool_scale, w_up_ssd,
           w_up_attn, w_up_pool, w_out, norm2_w, w_ffn_in, w_ffn_out, final_norm_w):
    batch, seq, d = x.shape
    assert d == D_MODEL and seq % KEY_CHUNK == 0 and seq % ROW_TILE == 0 and (batch * seq) % ROW_TILE == 0
    assert seq % Q_TILE == 0 and seq % (SSD_CHUNK * SSD_STEP_CHUNKS) == 0
    depth = w_in.shape[0]
    row3 = lambda a: a.reshape(depth, 1, a.shape[-1])

    wcat = _pack_w_in(w_in)
    norm1 = row3(norm1_w)
    conv_b3 = row3(conv_b)
    dtb = _pad_lanes(dt_bias)
    alog = _pad_lanes(a_log)
    dskip_x = row3(jnp.repeat(d_skip, SSD_HEAD_DIM, axis=-1))
    ssd_nw = row3(ssd_norm_w)
    pool_s = row3(pool_scale)
    norm2 = row3(norm2_w)
    final_w = final_norm_w.reshape(1, D_MODEL)
    pool_wb, wus, wua, wup, wo, wfi, wfo = map(_bf, (pool_w, w_up_ssd, w_up_attn, w_up_pool, w_out, w_ffn_in,
                                                     w_ffn_out))

    h = x.reshape(batch * seq, D_MODEL)
    for l in range(depth):
        z, xbc, q, k, vt, qi, ki, misc, u, gates = _inproj(h, norm1, wcat, l)
        y_ssd = _ssd(xbc, z, misc, conv_w, conv_b3, dtb, alog, dskip_x, ssd_nw, l, batch, seq)
        y_att = _dsa(q, qi, misc, k, vt, ki, batch, seq)
        h = _merge(y_ssd, y_att, u, gates, h, pool_wb, pool_s, wus, wua, wup, wo, l, seq)
        h = _ffn(h, norm2, wfi, wfo, final_w, l, final=(l == depth - 1))
    return h.reshape(batch, seq, D_MODEL)
```

```python
import functools

import jax
import jax.numpy as jnp
from jax import lax
from jax.experimental import pallas as pl
from jax.experimental.pallas import tpu as pltpu

D_MODEL = 1024
DEPTH = 4
SSD_HEADS = 16
SSD_HEAD_DIM = 64
SSD_INNER = SSD_HEADS * SSD_HEAD_DIM
SSD_GROUPS = 2
SSD_STATE = 128
SSD_CONV_DIM = SSD_INNER + 2 * SSD_GROUPS * SSD_STATE
CONV_WIDTH = 4
SSD_CHUNK = 128
ATT_HEADS = 8
ATT_KV_HEADS = 2
ATT_HEAD_DIM = 64
ATT_WIDTH = ATT_HEADS * ATT_HEAD_DIM
IDX_HEADS = 4
IDX_HEAD_DIM = 64
TOPK_MAX = 256
POOL_WINDOWS = (2, 4, 8, 16)
POOL_GROUPS = 4
POOL_GROUP_DIM = 128
POOL_WIDTH = POOL_GROUPS * POOL_GROUP_DIM
N_BRANCH = 3
FFN_HIDDEN = 2816
EPS = 1e-6
ATT_SCALE = ATT_HEAD_DIM ** -0.5
LOG2_E = 1.4426950408889634
IDX_SCALE = (IDX_HEADS ** -0.5) * (IDX_HEAD_DIM ** -0.5)

LANES = 128
SUBLANES = 8
VMEM_BUDGET_BYTES = 60000 * 1024

ROW_TILE = 512
Q_TILE = 2 * LANES
KEY_CHUNK = 512
POOL_HALO = 16
MXU_TILE = 256
FFN_BLOCKS = (6 * MXU_TILE, FFN_HIDDEN - 6 * MXU_TILE)
SSD_STEP_CHUNKS = 2
PLANE_BLOCK = 32 * SUBLANES
BLOCKS_PER_CHUNK = KEY_CHUNK // PLANE_BLOCK
TIE_BLOCK = PLANE_BLOCK

SEG_Z = SSD_INNER
SEG_XBC = SSD_CONV_DIM
SEG_Q = ATT_WIDTH
SEG_K = ATT_KV_HEADS * 2 * ATT_HEAD_DIM
SEG_V = ATT_KV_HEADS * 2 * ATT_HEAD_DIM
SEG_QI = IDX_HEADS * IDX_HEAD_DIM
SEG_KI = 2 * IDX_HEAD_DIM
SEG_MISC = LANES
SEG_U = POOL_WIDTH
SEG_G = N_BRANCH * D_MODEL
SEGS = (SEG_Z, SEG_XBC, SEG_Q, SEG_K, SEG_V, SEG_QI, SEG_KI, SEG_MISC, SEG_U, SEG_G)
V_SEG = 4
W_CAT = sum(SEGS)
MISC_WI = SSD_HEADS

INT_MIN = -(2 ** 31)
NEG_BIG = -1e30

_NT = (((1,), (1,)), ((), ()))


def _bf(x):
    return x.astype(jnp.bfloat16)


def _sigmoid(x):
    return 1.0 / (1.0 + jnp.exp(-x))


def _rms(x, w):
    return x * lax.rsqrt(jnp.mean(x * x, axis=-1, keepdims=True) + EPS) * w


def _params(sem, est_bytes):
    return pltpu.CompilerParams(dimension_semantics=sem,
                                vmem_limit_bytes=int(min(VMEM_BUDGET_BYTES, est_bytes)))


def _layer_spec(shape, l):
    nd = len(shape)
    return pl.BlockSpec((None,) + tuple(shape), lambda *_: (l,) + (0,) * nd, pipeline_mode=pl.Buffered(1))


def _inproj_kernel(h_ref, nw_ref, w_ref, *refs):
    out_refs, v_buf = refs[:-1], refs[-1]
    xn = _bf(_rms(h_ref[...], nw_ref[...]))
    off = 0
    for s, (ref, width) in enumerate(zip(out_refs, SEGS)):
        r = jnp.dot(xn, w_ref[:, off:off + width], preferred_element_type=jnp.float32)
        if s == V_SEG:
            v_buf[...] = r
            r = v_buf[...].T
        ref[...] = r.astype(ref.dtype)
        off += width


def _inproj(h, norm_w, wcat, l):
    n = h.shape[0]
    tm = ROW_TILE
    dts = (jnp.float32, jnp.float32, jnp.bfloat16, jnp.bfloat16, jnp.bfloat16, jnp.bfloat16, jnp.bfloat16,
           jnp.float32, jnp.float32, jnp.bfloat16)
    out_shape = [jax.ShapeDtypeStruct((n, w), dt) for w, dt in zip(SEGS, dts)]
    out_specs = [pl.BlockSpec((tm, w), lambda i: (i, 0)) for w in SEGS]
    out_shape[V_SEG] = jax.ShapeDtypeStruct((n // tm, SEG_V, tm), dts[V_SEG])
    out_specs[V_SEG] = pl.BlockSpec((None, SEG_V, tm), lambda i: (i, 0, 0))
    out_bytes = sum(tm * w * jnp.dtype(dt).itemsize for w, dt in zip(SEGS, dts))
    est = D_MODEL * W_CAT * 2 + 2 * tm * D_MODEL * 4 + 2 * out_bytes + 3 * tm * SEG_G * 4
    return pl.pallas_call(
        _inproj_kernel,
        grid=(n // tm,),
        in_specs=[pl.BlockSpec((tm, D_MODEL), lambda i: (i, 0)),
                  _layer_spec((1, D_MODEL), l),
                  _layer_spec((D_MODEL, W_CAT), l)],
        out_specs=out_specs,
        out_shape=out_shape,
        scratch_shapes=[pltpu.VMEM((tm, SEG_V), jnp.float32)],
        compiler_params=_params(("parallel",), est),
        name="inproj",
    )(h, norm_w, wcat)


def _col(a, j, width=LANES):
    return jnp.broadcast_to(a[:, j:j + 1], (a.shape[0], width))


def _expand_heads(a, spread):
    hi = _bf(a)
    lo = _bf(a - hi.astype(jnp.float32))
    return (jnp.dot(hi, spread, preferred_element_type=jnp.float32)
            + jnp.dot(lo, spread, preferred_element_type=jnp.float32))


def _silu(x):
    h = 0.5 * x
    return h + h * jnp.tanh(h)


def _ssd_kernel(xbc_ref, z_ref, misc_ref, cw_ref, cb_ref, dtb_ref, alog_ref, dskip_ref, nw_ref, spread_ref, o_ref,
                cbuf, state):
    L = SSD_CHUNK
    f32 = jnp.float32

    @pl.when(pl.program_id(1) == 0)
    def _():
        cbuf[0:SUBLANES, :] = jnp.zeros((SUBLANES, SSD_CONV_DIM), f32)
        state[...] = jnp.zeros(state.shape, f32)

    lane = lax.broadcasted_iota(jnp.int32, (L, LANES), 1)
    row = lax.broadcasted_iota(jnp.int32, (L, LANES), 0)
    left = lane < SSD_HEAD_DIM
    causal = lane <= row
    tri = jnp.where(causal, 1.0, 0.0)
    for sub in range(SSD_STEP_CHUNKS):
        _ssd_chunk(slice(sub * L, (sub + 1) * L), left, causal, tri, xbc_ref, z_ref, misc_ref, cw_ref, cb_ref, dtb_ref,
                   alog_ref, dskip_ref, nw_ref, spread_ref, o_ref, cbuf, state)


def _ssd_chunk(rows, left, causal, tri, xbc_ref, z_ref, misc_ref, cw_ref, cb_ref, dtb_ref, alog_ref, dskip_ref, nw_ref,
               spread_ref, o_ref, cbuf, state):
    L = SSD_CHUNK
    f32 = jnp.float32
    cbuf[SUBLANES:SUBLANES + L, :] = xbc_ref[rows, :]
    conv = cb_ref[...]
    for k in range(CONV_WIDTH):
        start = SUBLANES - (CONV_WIDTH - 1) + k
        conv = conv + cw_ref[k:k + 1, :] * cbuf[start:start + L, :]
    cbuf[0:SUBLANES, :] = cbuf[L:L + SUBLANES, :]
    xbc = _silu(conv)
    xs = xbc[:, :SSD_INNER]

    dtr = misc_ref[rows, :] + dtb_ref[...]
    dt = jnp.maximum(dtr, 0.0) + jnp.log1p(jnp.exp(-jnp.abs(dtr)))
    a = dt * (-jnp.exp(alog_ref[...]))
    a_cs = jnp.dot(tri, a, preferred_element_type=f32, precision=lax.Precision.HIGHEST)
    a_cs_t = a_cs.T
    ea = jnp.exp(a_cs)
    decay = jnp.exp(a_cs[L - 1:L, :] - a_cs)

    spread = spread_ref[...]
    x_all = xs * _expand_heads(dt, spread)
    ea_x = _expand_heads(ea, spread)
    xd_all = x_all * _expand_heads(decay, spread)

    gw = SSD_INNER // SSD_GROUPS
    y_parts = []
    for g in range(SSD_GROUPS):
        b_g = xbc[:, SSD_INNER + g * SSD_STATE:SSD_INNER + (g + 1) * SSD_STATE]
        c_off = SSD_INNER + SSD_GROUPS * SSD_STATE
        c_g = xbc[:, c_off + g * SSD_STATE:c_off + (g + 1) * SSD_STATE]
        cb = lax.dot_general(_bf(c_g), _bf(b_g), _NT, preferred_element_type=f32)
        st = state[g]
        y_off = jnp.dot(_bf(c_g), _bf(st), preferred_element_type=f32)
        new_st = jnp.dot(_bf(b_g.T), _bf(xd_all[:, g * gw:(g + 1) * gw]), preferred_element_type=f32)
        state[g] = st * ea_x[L - 1:L, g * gw:(g + 1) * gw] + new_st
        pairs = []
        for mm in range(SSD_HEADS // SSD_GROUPS // 2):
            m = g * (SSD_HEADS // SSD_GROUPS // 2) + mm
            lhs = []
            for hd in (2 * m, 2 * m + 1):
                seg = _col(a_cs, hd) - a_cs_t[hd:hd + 1, :]
                lhs.append(cb * jnp.where(causal, jnp.exp(seg), 0.0))
            xp = x_all[:, m * LANES:(m + 1) * LANES]
            rhs = jnp.concatenate([jnp.where(left, xp, 0.0), jnp.where(left, 0.0, xp)], axis=0)
            pairs.append(jnp.dot(_bf(jnp.concatenate(lhs, axis=1)), _bf(rhs), preferred_element_type=f32))
        y_parts.append(jnp.concatenate(pairs, axis=1) + y_off * ea_x[:, g * gw:(g + 1) * gw])
    y = jnp.concatenate(y_parts, axis=1) + dskip_ref[...] * xs
    o_ref[rows, :] = _bf(_rms(y * _silu(z_ref[rows, :]), nw_ref[...]))


def _ssd(xbc, z, misc, conv_w, conv_b, dtb, alog, dskip_x, norm_w, l, batch, seq):
    L = SSD_CHUNK * SSD_STEP_CHUNKS
    nc = seq // L
    row_spec = lambda w: pl.BlockSpec((L, w), lambda b, c: (b * nc + c, 0))
    head_of_col = lax.broadcasted_iota(jnp.int32, (LANES, SSD_INNER), 1) // SSD_HEAD_DIM
    spread = _bf(head_of_col == lax.broadcasted_iota(jnp.int32, (LANES, SSD_INNER), 0))
    est = (2 * L * (SSD_CONV_DIM + SSD_INNER + LANES) * 4 + 2 * L * SSD_INNER * 2
           + (L + 2 * SUBLANES) * SSD_CONV_DIM * 4 + SSD_GROUPS * SSD_STATE * SSD_INNER // SSD_GROUPS * 4
           + 40 * L * SSD_INNER * 4)
    return pl.pallas_call(
        _ssd_kernel,
        grid=(batch, nc),
        in_specs=[row_spec(SSD_CONV_DIM), row_spec(SSD_INNER), row_spec(LANES),
                  _layer_spec((CONV_WIDTH, SSD_CONV_DIM), l), _layer_spec((1, SSD_CONV_DIM), l),
                  _layer_spec((1, LANES), l), _layer_spec((1, LANES), l),
                  _layer_spec((1, SSD_INNER), l), _layer_spec((1, SSD_INNER), l),
                  pl.BlockSpec((LANES, SSD_INNER), lambda b, c: (0, 0))],
        out_specs=row_spec(SSD_INNER),
        out_shape=jax.ShapeDtypeStruct((batch * seq, SSD_INNER), jnp.bfloat16),
        scratch_shapes=[pltpu.VMEM((SSD_CHUNK + 2 * SUBLANES, SSD_CONV_DIM), jnp.float32),
                        pltpu.VMEM((SSD_GROUPS, SSD_STATE, SSD_INNER // SSD_GROUPS), jnp.float32)],
        compiler_params=_params(("parallel", "arbitrary"), est),
        name="ssd",
    )(xbc, z, misc, conv_w, conv_b, dtb, alog, dskip_x, norm_w, spread)


def _bit_planes(a):
    a = list(a)
    j, m = 16, 0x0000FFFF
    while j:
        for k in range(32):
            if not k & j:
                t = (a[k] ^ (a[k + j] >> j)) & m
                a[k] = a[k] ^ t
                a[k + j] = a[k + j] ^ (t << j)
        j >>= 1
        m ^= m << j
    return a


def _dsa_kernel(q_ref, qi_ref, misc_ref, k_ref, vt_ref, ki_ref, tri_ref, o_ref, plane_ref, bits_ref, s_ref, m_ref,
                acc_ref, *, topk):
    f32 = jnp.float32
    i32 = jnp.int32
    KC = KEY_CHUNK
    i = pl.program_id(1)
    nch = (i * Q_TILE + Q_TILE + KC - 1) // KC
    QT = Q_TILE
    qpos = i * QT + lax.broadcasted_iota(i32, (1, QT), 1)
    left = lax.broadcasted_iota(i32, (1, LANES), 1) < ATT_HEAD_DIM
    sub_iota = lax.broadcasted_iota(i32, (KC, QT), 0)

    misc_t = misc_ref[...].T
    qi = qi_ref[...].astype(f32)
    qi_heads, w_heads = [], []
    for h in range(IDX_HEADS):
        qp = qi[:, (h // 2) * LANES:(h // 2 + 1) * LANES]
        qi_heads.append(_bf(jnp.where(left, qp, 0.0) if h % 2 == 0 else jnp.where(left, 0.0, qp)))
        w_heads.append(misc_t[MISC_WI + h:MISC_WI + h + 1, :] * IDX_SCALE)
    qi_pairs = [jnp.concatenate(qi_heads[h:h + 2], axis=0) for h in range(0, IDX_HEADS, 2)]

    def score_body(c, carry):
        row0 = pl.multiple_of(c * KC, KC)
        ki_c = ki_ref[pl.ds(row0, KC), :]
        sc = jnp.zeros((KC, QT), f32)
        for hp, qi_pair in enumerate(qi_pairs):
            s2 = lax.dot_general(ki_c, qi_pair, _NT, preferred_element_type=f32)
            for j in range(2):
                s = s2[:, j * QT:(j + 1) * QT]
                sc = sc + jnp.maximum(s, 0.0) * w_heads[2 * hp + j]
        sc = jnp.where(sc == 0.0, 0.0, sc)
        bits = lax.bitcast_convert_type(sc, i32)
        key = bits ^ ((bits >> 31) & 0x7FFFFFFF)
        key = jnp.where(row0 + sub_iota <= qpos, key, INT_MIN)
        for blk in range(BLOCKS_PER_CHUNK):
            planes = _bit_planes([key[blk * PLANE_BLOCK + SUBLANES * j:blk * PLANE_BLOCK + SUBLANES * (j + 1)]
                                  for j in range(32)])
            planes[0] = planes[0] ^ -1
            for p in range(32):
                plane_ref[p, BLOCKS_PER_CHUNK * c + blk] = planes[p]
        return carry

    lax.fori_loop(0, nch, score_body, 0)

    nblk = plane_ref.shape[1]

    def clear_body(c, carry):
        for blk in range(BLOCKS_PER_CHUNK):
            for p in range(32):
                plane_ref[p, BLOCKS_PER_CHUNK * c + blk] = jnp.zeros((SUBLANES, QT), i32)
        return carry

    lax.fori_loop(nch, nblk // BLOCKS_PER_CHUNK, clear_body, 0)

    def pass_body(p, carry):
        remaining, took = carry
        plane = plane_ref[p]
        alive = bits_ref[0]
        hit = alive & plane
        ones = jnp.sum(lax.population_count(hit), axis=0).astype(f32)
        ones = jnp.sum(ones, axis=0, keepdims=True)
        take = ones >= remaining
        drop = jnp.where(take, 0, -1)
        bits_ref[0] = alive & (plane ^ drop)
        bits_ref[1] = bits_ref[1] | (hit & drop)
        return jnp.where(take, remaining, remaining - ones), jnp.where(take, 1.0, took)

    in_range = lax.broadcasted_iota(i32, (nblk, SUBLANES, QT), 0) < BLOCKS_PER_CHUNK * nch
    bits_ref[0] = jnp.where(in_range, -1, 0)
    bits_ref[1] = jnp.zeros((nblk, SUBLANES, QT), i32)
    need, took = lax.fori_loop(0, 32, pass_body, (jnp.full((1, QT), float(topk), f32), jnp.zeros((1, QT), f32)))
    need = jnp.where(took > 0.0, need, 0.0)
    tri = tri_ref[...]

    def unpack(words):
        return jnp.concatenate([(words >> (31 - j)) & 1 for j in range(32)], axis=0).astype(f32)

    def select_bias(c, seen):
        parts = []
        for blk in range(BLOCKS_PER_CHUNK):
            tie = unpack(bits_ref[0, BLOCKS_PER_CHUNK * c + blk])
            rank = seen + jnp.dot(tri, _bf(tie), preferred_element_type=f32)
            seen = rank[PLANE_BLOCK - 1:PLANE_BLOCK]
            sel = unpack(bits_ref[1, BLOCKS_PER_CHUNK * c + blk]) + jnp.where(rank <= need, tie, 0.0)
            parts.append((sel - 1.0) * (-NEG_BIG))
        return jnp.concatenate(parts, axis=0), seen

    q = q_ref[...].astype(f32)
    hpg = ATT_HEADS // ATT_KV_HEADS
    q_stacks = []
    for g in range(ATT_KV_HEADS):
        parts = []
        for hh in range(hpg):
            h = g * hpg + hh
            qp = q[:, (h // 2) * LANES:(h // 2 + 1) * LANES]
            parts.append(jnp.where(left, qp, 0.0) if h % 2 == 0 else jnp.where(left, 0.0, qp))
        q_stacks.append(_bf(jnp.concatenate(parts, axis=0)))
    ones_lo = _bf(jnp.where(lax.broadcasted_iota(i32, (LANES, KC), 0) < ATT_HEAD_DIM, 0.0, 1.0))
    m_ref[...] = jnp.full(m_ref.shape, NEG_BIG, f32)
    acc_ref[...] = jnp.zeros(acc_ref.shape, f32)

    def chunk_start(c):
        return c * KC if isinstance(c, int) else pl.multiple_of(c * KC, KC)

    def logits(c, slot, seen):
        row0 = chunk_start(c)
        bias, seen = select_bias(c, seen)
        bias = jnp.concatenate([bias] * hpg, axis=1)
        for g in range(ATT_KV_HEADS):
            k2 = k_ref[pl.ds(row0, KC), g * LANES:(g + 1) * LANES]
            s_ref[slot, g] = lax.dot_general(k2, q_stacks[g], _NT, preferred_element_type=f32) + bias
        return seen

    def softmax_pv(c, slot):
        for g in range(ATT_KV_HEADS):
            v1 = vt_ref[c, g * LANES:(g + 1) * LANES, :] + ones_lo
            s = s_ref[slot, g]
            m_prev = m_ref[g]
            m_new = jnp.maximum(m_prev, jnp.max(s, axis=0, keepdims=True))
            alpha = jnp.exp2(m_prev - m_new)
            p = jnp.exp2(s - m_new)
            acc_ref[g] = alpha * acc_ref[g] + jnp.dot(v1, _bf(p), preferred_element_type=f32)
            m_ref[g] = m_new

    def step(c, slot, seen):
        seen = logits(c + 1, 1 - slot, seen)
        softmax_pv(c, slot)
        return seen

    def pair_body(j, seen):
        return step(2 * j + 1, 1, step(2 * j, 0, seen))

    npairs = (nch - 1) // 2
    seen = lax.fori_loop(0, npairs, pair_body, logits(0, 0, jnp.zeros((1, QT), f32)))

    @pl.when(nch - 1 > 2 * npairs)
    def _():
        step(nch - 2, 0, seen)
        softmax_pv(nch - 1, 1)

    @pl.when(nch - 1 == 2 * npairs)
    def _():
        softmax_pv(nch - 1, 0)

    for qt in range(QT // LANES):
        outs = []
        for g in range(ATT_KV_HEADS):
            heads = []
            for hh in range(hpg):
                col = hh * QT + qt * LANES
                blk = acc_ref[g, :, col:col + LANES].T
                heads.append(blk / pltpu.roll(blk, ATT_HEAD_DIM, axis=1))
            for pp in range(hpg // 2):
                outs.append(jnp.where(left, heads[2 * pp], pltpu.roll(heads[2 * pp + 1], ATT_HEAD_DIM, axis=1)))
        o_ref[qt * LANES:(qt + 1) * LANES, :] = _bf(jnp.concatenate(outs, axis=1))


def _dsa(q, qi, misc, k, vt, ki, batch, seq):
    nq = seq // Q_TILE
    nkc = seq // KEY_CHUNK
    topk = min(TOPK_MAX, seq // 4)
    hpg = ATT_HEADS // ATT_KV_HEADS
    qrow = lambda w: pl.BlockSpec((Q_TILE, w), lambda b, i: (b * nq + i, 0))
    seq_spec = lambda w: pl.BlockSpec((seq, w), lambda b, i: (b, 0))
    tri = _bf(jnp.tril(jnp.ones((TIE_BLOCK, TIE_BLOCK), jnp.float32)))
    assert vt.shape == (batch * nkc, SEG_V, KEY_CHUNK)
    s_bytes = 2 * ATT_KV_HEADS * hpg * Q_TILE * KEY_CHUNK * 4
    est = (2 * seq * (SEG_K + SEG_V + SEG_KI) * 2 + nkc * KEY_CHUNK * Q_TILE * 4 + s_bytes
           + ATT_KV_HEADS * (LANES + SUBLANES) * hpg * Q_TILE * 4 + 8 * hpg * Q_TILE * KEY_CHUNK * 4)
    return pl.pallas_call(
        functools.partial(_dsa_kernel, topk=topk),
        grid=(batch, nq),
        in_specs=[qrow(SEG_Q), qrow(SEG_QI), qrow(SEG_MISC), seq_spec(SEG_K),
                  pl.BlockSpec((nkc, SEG_V, KEY_CHUNK), lambda b, i: (b, 0, 0)), seq_spec(SEG_KI),
                  pl.BlockSpec((TIE_BLOCK, TIE_BLOCK), lambda b, i: (0, 0))],
        out_specs=qrow(ATT_WIDTH),
        out_shape=jax.ShapeDtypeStruct((batch * seq, ATT_WIDTH), jnp.bfloat16),
        scratch_shapes=[pltpu.VMEM((32, nkc * BLOCKS_PER_CHUNK, SUBLANES, Q_TILE), jnp.int32),
                        pltpu.VMEM((2, nkc * BLOCKS_PER_CHUNK, SUBLANES, Q_TILE), jnp.int32),
                        pltpu.VMEM((2, ATT_KV_HEADS, KEY_CHUNK, hpg * Q_TILE), jnp.float32),
                        pltpu.VMEM((ATT_KV_HEADS, 1, hpg * Q_TILE), jnp.float32),
                        pltpu.VMEM((ATT_KV_HEADS, LANES, hpg * Q_TILE), jnp.float32)],
        compiler_params=_params(("parallel", "arbitrary"), est),
        name="dsa",
    )(q, qi, misc, k, vt, ki, tri)


def _merge_kernel(yssd_ref, yatt_ref, u_ref, uh_ref, g_ref, h_ref, pw_ref, ps_ref, wus_ref, wua_ref, wup_ref,
                  wo_ref, o_ref, ubuf, *, tiles_per_seq):
    f32 = jnp.float32
    tm = u_ref.shape[0]
    it = pl.program_id(0) % tiles_per_seq
    ubuf[POOL_HALO:POOL_HALO + tm, :] = u_ref[...]
    ubuf[0:POOL_HALO, :] = jnp.where(it == 0, 0.0, uh_ref[...])
    t1 = (it * tm + 1 + lax.broadcasted_iota(jnp.int32, (tm, POOL_GROUP_DIM), 0)).astype(f32)
    mixed = []
    for gi, win in enumerate(POOL_WINDOWS):
        cols = slice(gi * POOL_GROUP_DIM, (gi + 1) * POOL_GROUP_DIM)
        cur = ubuf[POOL_HALO:POOL_HALO + tm, cols]
        s = cur
        for d in range(1, win):
            s = s + ubuf[POOL_HALO - d:POOL_HALO - d + tm, cols]
        pooled = s / jnp.minimum(t1, float(win)) - cur
        mixed.append(jnp.dot(_bf(pooled), pw_ref[gi], preferred_element_type=f32))
    ypool = _bf(jnp.concatenate(mixed, axis=1) * ps_ref[...])

    g = g_ref[...].astype(f32)
    merged = (_sigmoid(g[:, :D_MODEL]) * jnp.dot(yssd_ref[...], wus_ref[...], preferred_element_type=f32)
              + _sigmoid(g[:, D_MODEL:2 * D_MODEL]) * jnp.dot(yatt_ref[...], wua_ref[...], preferred_element_type=f32)
              + _sigmoid(g[:, 2 * D_MODEL:]) * jnp.dot(ypool, wup_ref[...], preferred_element_type=f32))
    o_ref[...] = h_ref[...] + jnp.dot(_bf(merged), wo_ref[...], preferred_element_type=f32)


def _merge(y_ssd, y_att, u, gates, h, pool_w, pool_scale, w_up_ssd, w_up_attn, w_up_pool, w_out, l, seq):
    n = h.shape[0]
    tm = ROW_TILE
    tiles_per_seq = seq // tm
    row = lambda w: pl.BlockSpec((tm, w), lambda r: (r, 0))
    halo = pl.BlockSpec((POOL_HALO, POOL_WIDTH), lambda r: (jnp.maximum(r * (tm // POOL_HALO) - 1, 0), 0))
    w_bytes = (POOL_GROUPS * POOL_GROUP_DIM ** 2 + (SSD_INNER + ATT_WIDTH + POOL_WIDTH + D_MODEL) * D_MODEL) * 2
    est = (w_bytes + 2 * tm * ((SSD_INNER + ATT_WIDTH) * 2 + (POOL_WIDTH + SEG_G + 2 * D_MODEL) * 4)
           + (tm + POOL_HALO) * POOL_WIDTH * 4 + 12 * tm * D_MODEL * 4)
    return pl.pallas_call(
        functools.partial(_merge_kernel, tiles_per_seq=tiles_per_seq),
        grid=(n // tm,),
        in_specs=[row(SSD_INNER), row(ATT_WIDTH), row(POOL_WIDTH), halo, row(SEG_G), row(D_MODEL),
                  _layer_spec((POOL_GROUPS, POOL_GROUP_DIM, POOL_GROUP_DIM), l), _layer_spec((1, POOL_WIDTH), l),
                  _layer_spec((SSD_INNER, D_MODEL), l), _layer_spec((ATT_WIDTH, D_MODEL), l),
                  _layer_spec((POOL_WIDTH, D_MODEL), l), _layer_spec((D_MODEL, D_MODEL), l)],
        out_specs=row(D_MODEL),
        out_shape=jax.ShapeDtypeStruct((n, D_MODEL), jnp.float32),
        scratch_shapes=[pltpu.VMEM((tm + POOL_HALO, POOL_WIDTH), jnp.float32)],
        compiler_params=_params(("parallel",), est),
        name="merge",
    )(y_ssd, y_att, u, u, gates, h, pool_w, pool_scale, w_up_ssd, w_up_attn, w_up_pool, w_out)


def _ffn_kernel(h_ref, nw_ref, wi_ref, wo_ref, fw_ref, o_ref, *, final):
    f32 = jnp.float32
    h = h_ref[...]
    hn = _bf(_rms(h, nw_ref[...]))
    out = h
    lo = 0
    for blk in FFN_BLOCKS:
        a = jnp.dot(hn, wi_ref[:, lo:lo + blk], preferred_element_type=f32)
        b = jnp.dot(hn, wi_ref[:, FFN_HIDDEN + lo:FFN_HIDDEN + lo + blk], preferred_element_type=f32)
        out = out + jnp.dot(_bf(a * _sigmoid(a) * b), wo_ref[lo:lo + blk, :], preferred_element_type=f32)
        lo += blk
    if final:
        out = _rms(out, fw_ref[...])
    o_ref[...] = out


def _ffn(h, norm_w, w_ffn_in, w_ffn_out, final_w, l, final):
    n = h.shape[0]
    tm = ROW_TILE
    row = pl.BlockSpec((tm, D_MODEL), lambda r: (r, 0))
    est = (3 * D_MODEL * FFN_HIDDEN * 2 + 4 * tm * D_MODEL * 4 + 6 * tm * max(FFN_BLOCKS) * 4
           + 4 * tm * D_MODEL * 4)
    return pl.pallas_call(
        functools.partial(_ffn_kernel, final=final),
        grid=(n // tm,),
        in_specs=[row, _layer_spec((1, D_MODEL), l), _layer_spec((D_MODEL, 2 * FFN_HIDDEN), l),
                  _layer_spec((FFN_HIDDEN, D_MODEL), l),
                  pl.BlockSpec((1, D_MODEL), lambda r: (0, 0))],
        out_specs=row,
        out_shape=jax.ShapeDtypeStruct((n, D_MODEL), jnp.float32),
        compiler_params=_params(("parallel",), est),
        name="ffn",
    )(h, norm_w, w_ffn_in, w_ffn_out, final_w)


def _pack_w_in(w_in):
    offs, acc = [], 0
    for s in (SSD_INNER, SSD_CONV_DIM, SSD_HEADS, ATT_WIDTH, ATT_KV_HEADS * ATT_HEAD_DIM,
              ATT_KV_HEADS * ATT_HEAD_DIM, IDX_HEADS * IDX_HEAD_DIM, IDX_HEAD_DIM, IDX_HEADS, POOL_WIDTH):
        acc += s
        offs.append(acc)
    z, xbc, dt, q, k, v, qi, ki, wi, u, gates = jnp.split(w_in, offs, axis=-1)
    q = q * (ATT_SCALE * LOG2_E)
    dh = ATT_HEAD_DIM
    kv = [k[..., g * dh:(g + 1) * dh] for g in range(ATT_KV_HEADS) for _ in range(2)]
    for g in range(ATT_KV_HEADS):
        kv += [v[..., g * dh:(g + 1) * dh], jnp.zeros_like(v[..., :dh])]
    pad = jnp.zeros(w_in.shape[:-1] + (SEG_MISC - SSD_HEADS - IDX_HEADS,), w_in.dtype)
    return _bf(jnp.concatenate([z, xbc, q] + kv + [qi, ki, ki, dt, wi, pad, u, gates], axis=-1))


def _pad_lanes(a):
    depth, w = a.shape
    return jnp.pad(a, ((0, 0), (0, LANES - w))).reshape(depth, 1, LANES)


def kernel(x, norm1_w, w_in, conv_w, conv_b, dt_bias, a_log, d_skip, ssd_norm_w, pool_w, pool_scale, w_up_ssd,
           w_up_attn, w_up_pool, w_out, norm2_w, w_ffn_in, w_ffn_out, final_norm_w):
    batch, seq, d = x.shape
    assert d == D_MODEL and seq % KEY_CHUNK == 0 and seq % ROW_TILE == 0 and (batch * seq) % ROW_TILE == 0
    assert seq % Q_TILE == 0 and seq % (SSD_CHUNK * SSD_STEP_CHUNKS) == 0
    depth = w_in.shape[0]
    row3 = lambda a: a.reshape(depth, 1, a.shape[-1])

    wcat = _pack_w_in(w_in)
    norm1 = row3(norm1_w)
    conv_b3 = row3(conv_b)
    dtb = _pad_lanes(dt_bias)
    alog = _pad_lanes(a_log)
    dskip_x = row3(jnp.repeat(d_skip, SSD_HEAD_DIM, axis=-1))
    ssd_nw = row3(ssd_norm_w)
    pool_s = row3(pool_scale)
    norm2 = row3(norm2_w)
    final_w = final_norm_w.reshape(1, D_MODEL)
    pool_wb, wus, wua, wup, wo, wfi, wfo = map(_bf, (pool_w, w_up_ssd, w_up_attn, w_up_pool, w_out, w_ffn_in,
                                                     w_ffn_out))

    h = x.reshape(batch * seq, D_MODEL)
    for l in range(depth):
        z, xbc, q, k, vt, qi, ki, misc, u, gates = _inproj(h, norm1, wcat, l)
        y_ssd = _ssd(xbc, z, misc, conv_w, conv_b3, dtb, alog, dskip_x, ssd_nw, l, batch, seq)
        y_att = _dsa(q, qi, misc, k, vt, ki, batch, seq)
        h = _merge(y_ssd, y_att, u, gates, h, pool_wb, pool_s, wus, wua, wup, wo, l, seq)
        h = _ffn(h, norm2, wfi, wfo, final_w, l, final=(l == depth - 1))
    return h.reshape(batch, seq, D_MODEL)
```

```python
import functools

import jax
import jax.numpy as jnp
from jax import lax
from jax.experimental import pallas as pl
from jax.experimental.pallas import tpu as pltpu

D_MODEL = 1024
DEPTH = 4
SSD_HEADS = 16
SSD_HEAD_DIM = 64
SSD_INNER = SSD_HEADS * SSD_HEAD_DIM
SSD_GROUPS = 2
SSD_STATE = 128
SSD_CONV_DIM = SSD_INNER + 2 * SSD_GROUPS * SSD_STATE
CONV_WIDTH = 4
SSD_CHUNK = 128
ATT_HEADS = 8
ATT_KV_HEADS = 2
ATT_HEAD_DIM = 64
ATT_WIDTH = ATT_HEADS * ATT_HEAD_DIM
IDX_HEADS = 4
IDX_HEAD_DIM = 64
TOPK_MAX = 256
POOL_WINDOWS = (2, 4, 8, 16)
POOL_GROUPS = 4
POOL_GROUP_DIM = 128
POOL_WIDTH = POOL_GROUPS * POOL_GROUP_DIM
N_BRANCH = 3
FFN_HIDDEN = 2816
EPS = 1e-6
ATT_SCALE = ATT_HEAD_DIM ** -0.5
LOG2_E = 1.4426950408889634
IDX_SCALE = (IDX_HEADS ** -0.5) * (IDX_HEAD_DIM ** -0.5)

LANES = 128
SUBLANES = 8
VMEM_BUDGET_BYTES = 60000 * 1024

ROW_TILE = 512
Q_TILE = 2 * LANES
KEY_CHUNK = 512
POOL_HALO = 16
MXU_TILE = 256
FFN_BLOCKS = (6 * MXU_TILE, FFN_HIDDEN - 6 * MXU_TILE)
SSD_STEP_CHUNKS = 2
PLANE_BLOCK = 32 * SUBLANES
BLOCKS_PER_CHUNK = KEY_CHUNK // PLANE_BLOCK
TIE_BLOCK = PLANE_BLOCK

SEG_Z = SSD_INNER
SEG_XBC = SSD_CONV_DIM
SEG_Q = ATT_WIDTH
SEG_K = ATT_KV_HEADS * 2 * ATT_HEAD_DIM
SEG_V = ATT_KV_HEADS * 2 * ATT_HEAD_DIM
SEG_QI = IDX_HEADS * IDX_HEAD_DIM
SEG_KI = 2 * IDX_HEAD_DIM
SEG_MISC = LANES
SEG_U = POOL_WIDTH
SEG_G = N_BRANCH * D_MODEL
SEGS = (SEG_Z, SEG_XBC, SEG_Q, SEG_K, SEG_V, SEG_QI, SEG_KI, SEG_MISC, SEG_U, SEG_G)
V_SEG = 4
W_CAT = sum(SEGS)
MISC_WI = SSD_HEADS

INT_MIN = -(2 ** 31)
NEG_BIG = -1e30

_NT = (((1,), (1,)), ((), ()))


def _bf(x):
    return x.astype(jnp.bfloat16)


def _sigmoid(x):
    return 1.0 / (1.0 + jnp.exp(-x))


def _rms(x, w):
    return x * lax.rsqrt(jnp.mean(x * x, axis=-1, keepdims=True) + EPS) * w


def _params(sem, est_bytes):
    return pltpu.CompilerParams(dimension_semantics=sem,
                                vmem_limit_bytes=int(min(VMEM_BUDGET_BYTES, est_bytes)))


def _layer_spec(shape, l):
    nd = len(shape)
    return pl.BlockSpec((None,) + tuple(shape), lambda *_: (l,) + (0,) * nd, pipeline_mode=pl.Buffered(1))


def _inproj_kernel(h_ref, nw_ref, w_ref, *refs):
    out_refs, v_buf = refs[:-1], refs[-1]
    xn = _bf(_rms(h_ref[...], nw_ref[...]))
    off = 0
    for s, (ref, width) in enumerate(zip(out_refs, SEGS)):
        r = jnp.dot(xn, w_ref[:, off:off + width], preferred_element_type=jnp.float32)
        if s == V_SEG:
            v_buf[...] = r
            r = v_buf[...].T
        ref[...] = r.astype(ref.dtype)
        off += width


def _inproj(h, norm_w, wcat, l):
    n = h.shape[0]
    tm = ROW_TILE
    dts = (jnp.float32, jnp.float32, jnp.bfloat16, jnp.bfloat16, jnp.bfloat16, jnp.bfloat16, jnp.bfloat16,
           jnp.float32, jnp.float32, jnp.bfloat16)
    out_shape = [jax.ShapeDtypeStruct((n, w), dt) for w, dt in zip(SEGS, dts)]
    out_specs = [pl.BlockSpec((tm, w), lambda i: (i, 0)) for w in SEGS]
    out_shape[V_SEG] = jax.ShapeDtypeStruct((n // tm, SEG_V, tm), dts[V_SEG])
    out_specs[V_SEG] = pl.BlockSpec((None, SEG_V, tm), lambda i: (i, 0, 0))
    out_bytes = sum(tm * w * jnp.dtype(dt).itemsize for w, dt in zip(SEGS, dts))
    est = D_MODEL * W_CAT * 2 + 2 * tm * D_MODEL * 4 + 2 * out_bytes + 3 * tm * SEG_G * 4
    return pl.pallas_call(
        _inproj_kernel,
        grid=(n // tm,),
        in_specs=[pl.BlockSpec((tm, D_MODEL), lambda i: (i, 0)),
                  _layer_spec((1, D_MODEL), l),
                  _layer_spec((D_MODEL, W_CAT), l)],
        out_specs=out_specs,
        out_shape=out_shape,
        scratch_shapes=[pltpu.VMEM((tm, SEG_V), jnp.float32)],
        compiler_params=_params(("parallel",), est),
        name="inproj",
    )(h, norm_w, wcat)


def _col(a, j, width=LANES):
    return jnp.broadcast_to(a[:, j:j + 1], (a.shape[0], width))


def _expand_heads(a, spread):
    hi = _bf(a)
    lo = _bf(a - hi.astype(jnp.float32))
    return (jnp.dot(hi, spread, preferred_element_type=jnp.float32)
            + jnp.dot(lo, spread, preferred_element_type=jnp.float32))


def _silu(x):
    h = 0.5 * x
    return h + h * jnp.tanh(h)


def _ssd_kernel(xbc_ref, z_ref, misc_ref, cw_ref, cb_ref, dtb_ref, alog_ref, dskip_ref, nw_ref, spread_ref, o_ref,
                cbuf, state):
    L = SSD_CHUNK
    f32 = jnp.float32

    @pl.when(pl.program_id(1) == 0)
    def _():
        cbuf[0:SUBLANES, :] = jnp.zeros((SUBLANES, SSD_CONV_DIM), f32)
        state[...] = jnp.zeros(state.shape, f32)

    lane = lax.broadcasted_iota(jnp.int32, (L, LANES), 1)
    row = lax.broadcasted_iota(jnp.int32, (L, LANES), 0)
    left = lane < SSD_HEAD_DIM
    causal = lane <= row
    tri = jnp.where(causal, 1.0, 0.0)
    for sub in range(SSD_STEP_CHUNKS):
        _ssd_chunk(slice(sub * L, (sub + 1) * L), left, causal, tri, xbc_ref, z_ref, misc_ref, cw_ref, cb_ref, dtb_ref,
                   alog_ref, dskip_ref, nw_ref, spread_ref, o_ref, cbuf, state)


def _ssd_chunk(rows, left, causal, tri, xbc_ref, z_ref, misc_ref, cw_ref, cb_ref, dtb_ref, alog_ref, dskip_ref, nw_ref,
               spread_ref, o_ref, cbuf, state):
    L = SSD_CHUNK
    f32 = jnp.float32
    cbuf[SUBLANES:SUBLANES + L, :] = xbc_ref[rows, :]
    conv = cb_ref[...]
    for k in range(CONV_WIDTH):
        start = SUBLANES - (CONV_WIDTH - 1) + k
        conv = conv + cw_ref[k:k + 1, :] * cbuf[start:start + L, :]
    cbuf[0:SUBLANES, :] = cbuf[L:L + SUBLANES, :]
    xbc = _silu(conv)
    xs = xbc[:, :SSD_INNER]

    dtr = misc_ref[rows, :] + dtb_ref[...]
    dt = jnp.maximum(dtr, 0.0) + jnp.log1p(jnp.exp(-jnp.abs(dtr)))
    a = dt * (-jnp.exp(alog_ref[...]))
    a_cs = jnp.dot(tri, a, preferred_element_type=f32, precision=lax.Precision.HIGHEST)
    a_cs_t = a_cs.T
    ea = jnp.exp(a_cs)
    decay = jnp.exp(a_cs[L - 1:L, :] - a_cs)

    spread = spread_ref[...]
    x_all = xs * _expand_heads(dt, spread)
    ea_x = _expand_heads(ea, spread)
    xd_all = x_all * _expand_heads(decay, spread)

    gw = SSD_INNER // SSD_GROUPS
    y_parts = []
    for g in range(SSD_GROUPS):
        b_g = xbc[:, SSD_INNER + g * SSD_STATE:SSD_INNER + (g + 1) * SSD_STATE]
        c_off = SSD_INNER + SSD_GROUPS * SSD_STATE
        c_g = xbc[:, c_off + g * SSD_STATE:c_off + (g + 1) * SSD_STATE]
        cb = lax.dot_general(_bf(c_g), _bf(b_g), _NT, preferred_element_type=f32)
        st = state[g]
        y_off = jnp.dot(_bf(c_g), _bf(st), preferred_element_type=f32)
        new_st = jnp.dot(_bf(b_g.T), _bf(xd_all[:, g * gw:(g + 1) * gw]), preferred_element_type=f32)
        state[g] = st * ea_x[L - 1:L, g * gw:(g + 1) * gw] + new_st
        pairs = []
        for mm in range(SSD_HEADS // SSD_GROUPS // 2):
            m = g * (SSD_HEADS // SSD_GROUPS // 2) + mm
            lhs = []
            for hd in (2 * m, 2 * m + 1):
                seg = _col(a_cs, hd) - a_cs_t[hd:hd + 1, :]
                lhs.append(cb * jnp.where(causal, jnp.exp(seg), 0.0))
            xp = x_all[:, m * LANES:(m + 1) * LANES]
            rhs = jnp.concatenate([jnp.where(left, xp, 0.0), jnp.where(left, 0.0, xp)], axis=0)
            pairs.append(jnp.dot(_bf(jnp.concatenate(lhs, axis=1)), _bf(rhs), preferred_element_type=f32))
        y_parts.append(jnp.concatenate(pairs, axis=1) + y_off * ea_x[:, g * gw:(g + 1) * gw])
    y = jnp.concatenate(y_parts, axis=1) + dskip_ref[...] * xs
    o_ref[rows, :] = _bf(_rms(y * _silu(z_ref[rows, :]), nw_ref[...]))


def _ssd(xbc, z, misc, conv_w, conv_b, dtb, alog, dskip_x, norm_w, l, batch, seq):
    L = SSD_CHUNK * SSD_STEP_CHUNKS
    nc = seq // L
    row_spec = lambda w: pl.BlockSpec((L, w), lambda b, c: (b * nc + c, 0))
    head_of_col = lax.broadcasted_iota(jnp.int32, (LANES, SSD_INNER), 1) // SSD_HEAD_DIM
    spread = _bf(head_of_col == lax.broadcasted_iota(jnp.int32, (LANES, SSD_INNER), 0))
    est = (2 * L * (SSD_CONV_DIM + SSD_INNER + LANES) * 4 + 2 * L * SSD_INNER * 2
           + (L + 2 * SUBLANES) * SSD_CONV_DIM * 4 + SSD_GROUPS * SSD_STATE * SSD_INNER // SSD_GROUPS * 4
           + 40 * L * SSD_INNER * 4)
    return pl.pallas_call(
        _ssd_kernel,
        grid=(batch, nc),
        in_specs=[row_spec(SSD_CONV_DIM), row_spec(SSD_INNER), row_spec(LANES),
                  _layer_spec((CONV_WIDTH, SSD_CONV_DIM), l), _layer_spec((1, SSD_CONV_DIM), l),
                  _layer_spec((1, LANES), l), _layer_spec((1, LANES), l),
                  _layer_spec((1, SSD_INNER), l), _layer_spec((1, SSD_INNER), l),
                  pl.BlockSpec((LANES, SSD_INNER), lambda b, c: (0, 0))],
        out_specs=row_spec(SSD_INNER),
        out_shape=jax.ShapeDtypeStruct((batch * seq, SSD_INNER), jnp.bfloat16),
        scratch_shapes=[pltpu.VMEM((SSD_CHUNK + 2 * SUBLANES, SSD_CONV_DIM), jnp.float32),
                        pltpu.VMEM((SSD_GROUPS, SSD_STATE, SSD_INNER // SSD_GROUPS), jnp.float32)],
        compiler_params=_params(("parallel", "arbitrary"), est),
        name="ssd",
    )(xbc, z, misc, conv_w, conv_b, dtb, alog, dskip_x, norm_w, spread)


def _bit_planes(a):
    a = list(a)
    j, m = 16, 0x0000FFFF
    while j:
        for k in range(32):
            if not k & j:
                t = (a[k] ^ (a[k + j] >> j)) & m
                a[k] = a[k] ^ t
                a[k + j] = a[k + j] ^ (t << j)
        j >>= 1
        m ^= m << j
    return a


def _dsa_kernel(q_ref, qi_ref, misc_ref, k_ref, vt_ref, ki_ref, tri_ref, o_ref, plane_ref, bits_ref, s_ref, m_ref,
                acc_ref, *, topk):
    f32 = jnp.float32
    i32 = jnp.int32
    KC = KEY_CHUNK
    i = pl.program_id(1)
    nch = (i * Q_TILE + Q_TILE + KC - 1) // KC
    QT = Q_TILE
    qpos = i * QT + lax.broadcasted_iota(i32, (1, QT), 1)
    left = lax.broadcasted_iota(i32, (1, LANES), 1) < ATT_HEAD_DIM
    sub_iota = lax.broadcasted_iota(i32, (KC, QT), 0)

    misc_t = misc_ref[...].T
    qi = qi_ref[...].astype(f32)
    qi_heads, w_heads = [], []
    for h in range(IDX_HEADS):
        qp = qi[:, (h // 2) * LANES:(h // 2 + 1) * LANES]
        qi_heads.append(_bf(jnp.where(left, qp, 0.0) if h % 2 == 0 else jnp.where(left, 0.0, qp)))
        w_heads.append(misc_t[MISC_WI + h:MISC_WI + h + 1, :] * IDX_SCALE)
    qi_pairs = [jnp.concatenate(qi_heads[h:h + 2], axis=0) for h in range(0, IDX_HEADS, 2)]

    def score_body(c, carry):
        row0 = pl.multiple_of(c * KC, KC)
        ki_c = ki_ref[pl.ds(row0, KC), :]
        sc = jnp.zeros((KC, QT), f32)
        for hp, qi_pair in enumerate(qi_pairs):
            s2 = lax.dot_general(ki_c, qi_pair, _NT, preferred_element_type=f32)
            for j in range(2):
                s = s2[:, j * QT:(j + 1) * QT]
                sc = sc + jnp.maximum(s, 0.0) * w_heads[2 * hp + j]
        sc = jnp.where(sc == 0.0, 0.0, sc)
        bits = lax.bitcast_convert_type(sc, i32)
        key = bits ^ ((bits >> 31) & 0x7FFFFFFF)
        key = jnp.where(row0 + sub_iota <= qpos, key, INT_MIN)
        for blk in range(BLOCKS_PER_CHUNK):
            planes = _bit_planes([key[blk * PLANE_BLOCK + SUBLANES * j:blk * PLANE_BLOCK + SUBLANES * (j + 1)]
                                  for j in range(32)])
            planes[0] = planes[0] ^ -1
            for p in range(32):
                plane_ref[p, BLOCKS_PER_CHUNK * c + blk] = planes[p]
        return carry

    lax.fori_loop(0, nch, score_body, 0)

    nblk = plane_ref.shape[1]

    def clear_body(c, carry):
        for blk in range(BLOCKS_PER_CHUNK):
            for p in range(32):
                plane_ref[p, BLOCKS_PER_CHUNK * c + blk] = jnp.zeros((SUBLANES, QT), i32)
        return carry

    lax.fori_loop(nch, nblk // BLOCKS_PER_CHUNK, clear_body, 0)

    def pass_body(p, carry):
        remaining, took = carry
        plane = plane_ref[p]
        alive = bits_ref[0]
        hit = alive & plane
        ones = jnp.sum(lax.population_count(hit), axis=0).astype(f32)
        ones = jnp.sum(ones, axis=0, keepdims=True)
        take = ones >= remaining
        drop = jnp.where(take, 0, -1)
        bits_ref[0] = alive & (plane ^ drop)
        bits_ref[1] = bits_ref[1] | (hit & drop)
        return jnp.where(take, remaining, remaining - ones), jnp.where(take, 1.0, took)

    in_range = lax.broadcasted_iota(i32, (nblk, SUBLANES, QT), 0) < BLOCKS_PER_CHUNK * nch
    bits_ref[0] = jnp.where(in_range, -1, 0)
    bits_ref[1] = jnp.zeros((nblk, SUBLANES, QT), i32)
    need, took = lax.fori_loop(0, 32, pass_body, (jnp.full((1, QT), float(topk), f32), jnp.zeros((1, QT), f32)))
    need = jnp.where(took > 0.0, need, 0.0)
    tri = tri_ref[...]

    def unpack(words):
        return jnp.concatenate([(words >> (31 - j)) & 1 for j in range(32)], axis=0).astype(f32)

    def select_bias(blk, seen):
        tie = unpack(bits_ref[0, blk])
        rank = seen + jnp.dot(tri, _bf(tie), preferred_element_type=f32)
        sel = unpack(bits_ref[1, blk]) + jnp.where(rank <= need, tie, 0.0)
        return (sel - 1.0) * (-NEG_BIG), rank[PLANE_BLOCK - 1:PLANE_BLOCK]

    q = q_ref[...].astype(f32)
    hpg = ATT_HEADS // ATT_KV_HEADS
    q_stacks = []
    for g in range(ATT_KV_HEADS):
        parts = []
        for hh in range(hpg):
            h = g * hpg + hh
            qp = q[:, (h // 2) * LANES:(h // 2 + 1) * LANES]
            parts.append(jnp.where(left, qp, 0.0) if h % 2 == 0 else jnp.where(left, 0.0, qp))
        q_stacks.append(_bf(jnp.concatenate(parts, axis=0)))
    HB = PLANE_BLOCK
    ones_lo = _bf(jnp.where(lax.broadcasted_iota(i32, (LANES, HB), 0) < ATT_HEAD_DIM, 0.0, 1.0))
    m_ref[...] = jnp.full(m_ref.shape, NEG_BIG, f32)
    acc_ref[...] = jnp.zeros(acc_ref.shape, f32)

    def logits(c, half, seen):
        row0 = c * KC + half * HB
        row0 = row0 if isinstance(row0, int) else pl.multiple_of(row0, HB)
        bias, seen = select_bias(BLOCKS_PER_CHUNK * c + half, seen)
        bias = jnp.concatenate([bias] * hpg, axis=1)
        for g in range(ATT_KV_HEADS):
            k2 = k_ref[pl.ds(row0, HB), g * LANES:(g + 1) * LANES]
            s_ref[half, g] = lax.dot_general(k2, q_stacks[g], _NT, preferred_element_type=f32) + bias
        return seen

    def softmax_pv(c, half):
        for g in range(ATT_KV_HEADS):
            v1 = vt_ref[c, g * LANES:(g + 1) * LANES, half * HB:(half + 1) * HB] + ones_lo
            s = s_ref[half, g]
            m_prev = m_ref[g]
            m_new = jnp.maximum(m_prev, jnp.max(s, axis=0, keepdims=True))
            alpha = jnp.exp2(m_prev - m_new)
            p = jnp.exp2(s - m_new)
            acc_ref[g] = alpha * acc_ref[g] + jnp.dot(v1, _bf(p), preferred_element_type=f32)
            m_ref[g] = m_new

    def chunk_body(c, seen):
        seen = logits(c, 1, seen)
        softmax_pv(c, 0)
        seen = logits(c + 1, 0, seen)
        softmax_pv(c, 1)
        return seen

    seen = lax.fori_loop(0, nch - 1, chunk_body, logits(0, 0, jnp.zeros((1, QT), f32)))
    logits(nch - 1, 1, seen)
    softmax_pv(nch - 1, 0)
    softmax_pv(nch - 1, 1)

    for qt in range(QT // LANES):
        outs = []
        for g in range(ATT_KV_HEADS):
            heads = []
            for hh in range(hpg):
                col = hh * QT + qt * LANES
                blk = acc_ref[g, :, col:col + LANES].T
                heads.append(blk / pltpu.roll(blk, ATT_HEAD_DIM, axis=1))
            for pp in range(hpg // 2):
                outs.append(jnp.where(left, heads[2 * pp], pltpu.roll(heads[2 * pp + 1], ATT_HEAD_DIM, axis=1)))
        o_ref[qt * LANES:(qt + 1) * LANES, :] = _bf(jnp.concatenate(outs, axis=1))


def _dsa(q, qi, misc, k, vt, ki, batch, seq):
    nq = seq // Q_TILE
    nkc = seq // KEY_CHUNK
    topk = min(TOPK_MAX, seq // 4)
    hpg = ATT_HEADS // ATT_KV_HEADS
    qrow = lambda w: pl.BlockSpec((Q_TILE, w), lambda b, i: (b * nq + i, 0))
    seq_spec = lambda w: pl.BlockSpec((seq, w), lambda b, i: (b, 0))
    tri = _bf(jnp.tril(jnp.ones((TIE_BLOCK, TIE_BLOCK), jnp.float32)))
    assert vt.shape == (batch * nkc, SEG_V, KEY_CHUNK)
    assert BLOCKS_PER_CHUNK == 2
    s_bytes = 2 * ATT_KV_HEADS * hpg * Q_TILE * KEY_CHUNK * 4
    est = (2 * seq * (SEG_K + SEG_V + SEG_KI) * 2 + nkc * KEY_CHUNK * Q_TILE * 4 + s_bytes
           + ATT_KV_HEADS * (LANES + SUBLANES) * hpg * Q_TILE * 4 + 8 * hpg * Q_TILE * KEY_CHUNK * 4)
    return pl.pallas_call(
        functools.partial(_dsa_kernel, topk=topk),
        grid=(batch, nq),
        in_specs=[qrow(SEG_Q), qrow(SEG_QI), qrow(SEG_MISC), seq_spec(SEG_K),
                  pl.BlockSpec((nkc, SEG_V, KEY_CHUNK), lambda b, i: (b, 0, 0)), seq_spec(SEG_KI),
                  pl.BlockSpec((TIE_BLOCK, TIE_BLOCK), lambda b, i: (0, 0))],
        out_specs=qrow(ATT_WIDTH),
        out_shape=jax.ShapeDtypeStruct((batch * seq, ATT_WIDTH), jnp.bfloat16),
        scratch_shapes=[pltpu.VMEM((32, nkc * BLOCKS_PER_CHUNK, SUBLANES, Q_TILE), jnp.int32),
                        pltpu.VMEM((2, nkc * BLOCKS_PER_CHUNK, SUBLANES, Q_TILE), jnp.int32),
                        pltpu.VMEM((BLOCKS_PER_CHUNK, ATT_KV_HEADS, PLANE_BLOCK, hpg * Q_TILE), jnp.float32),
                        pltpu.VMEM((ATT_KV_HEADS, 1, hpg * Q_TILE), jnp.float32),
                        pltpu.VMEM((ATT_KV_HEADS, LANES, hpg * Q_TILE), jnp.float32)],
        compiler_params=_params(("parallel", "arbitrary"), est),
        name="dsa",
    )(q, qi, misc, k, vt, ki, tri)


def _merge_kernel(yssd_ref, yatt_ref, u_ref, uh_ref, g_ref, h_ref, pw_ref, ps_ref, wus_ref, wua_ref, wup_ref,
                  wo_ref, o_ref, ubuf, *, tiles_per_seq):
    f32 = jnp.float32
    tm = u_ref.shape[0]
    it = pl.program_id(0) % tiles_per_seq
    ubuf[POOL_HALO:POOL_HALO + tm, :] = u_ref[...]
    ubuf[0:POOL_HALO, :] = jnp.where(it == 0, 0.0, uh_ref[...])
    t1 = (it * tm + 1 + lax.broadcasted_iota(jnp.int32, (tm, POOL_GROUP_DIM), 0)).astype(f32)
    mixed = []
    for gi, win in enumerate(POOL_WINDOWS):
        cols = slice(gi * POOL_GROUP_DIM, (gi + 1) * POOL_GROUP_DIM)
        cur = ubuf[POOL_HALO:POOL_HALO + tm, cols]
        s = cur
        for d in range(1, win):
            s = s + ubuf[POOL_HALO - d:POOL_HALO - d + tm, cols]
        pooled = s / jnp.minimum(t1, float(win)) - cur
        mixed.append(jnp.dot(_bf(pooled), pw_ref[gi], preferred_element_type=f32))
    ypool = _bf(jnp.concatenate(mixed, axis=1) * ps_ref[...])

    g = g_ref[...].astype(f32)
    merged = (_sigmoid(g[:, :D_MODEL]) * jnp.dot(yssd_ref[...], wus_ref[...], preferred_element_type=f32)
              + _sigmoid(g[:, D_MODEL:2 * D_MODEL]) * jnp.dot(yatt_ref[...], wua_ref[...], preferred_element_type=f32)
              + _sigmoid(g[:, 2 * D_MODEL:]) * jnp.dot(ypool, wup_ref[...], preferred_element_type=f32))
    o_ref[...] = h_ref[...] + jnp.dot(_bf(merged), wo_ref[...], preferred_element_type=f32)


def _merge(y_ssd, y_att, u, gates, h, pool_w, pool_scale, w_up_ssd, w_up_attn, w_up_pool, w_out, l, seq):
    n = h.shape[0]
    tm = ROW_TILE
    tiles_per_seq = seq // tm
    row = lambda w: pl.BlockSpec((tm, w), lambda r: (r, 0))
    halo = pl.BlockSpec((POOL_HALO, POOL_WIDTH), lambda r: (jnp.maximum(r * (tm // POOL_HALO) - 1, 0), 0))
    w_bytes = (POOL_GROUPS * POOL_GROUP_DIM ** 2 + (SSD_INNER + ATT_WIDTH + POOL_WIDTH + D_MODEL) * D_MODEL) * 2
    est = (w_bytes + 2 * tm * ((SSD_INNER + ATT_WIDTH) * 2 + (POOL_WIDTH + SEG_G + 2 * D_MODEL) * 4)
           + (tm + POOL_HALO) * POOL_WIDTH * 4 + 12 * tm * D_MODEL * 4)
    return pl.pallas_call(
        functools.partial(_merge_kernel, tiles_per_seq=tiles_per_seq),
        grid=(n // tm,),
        in_specs=[row(SSD_INNER), row(ATT_WIDTH), row(POOL_WIDTH), halo, row(SEG_G), row(D_MODEL),
                  _layer_spec((POOL_GROUPS, POOL_GROUP_DIM, POOL_GROUP_DIM), l), _layer_spec((1, POOL_WIDTH), l),
                  _layer_spec((SSD_INNER, D_MODEL), l), _layer_spec((ATT_WIDTH, D_MODEL), l),
                  _layer_spec((POOL_WIDTH, D_MODEL), l), _layer_spec((D_MODEL, D_MODEL), l)],
        out_specs=row(D_MODEL),
        out_shape=jax.ShapeDtypeStruct((n, D_MODEL), jnp.float32),
        scratch_shapes=[pltpu.VMEM((tm + POOL_HALO, POOL_WIDTH), jnp.float32)],
        compiler_params=_params(("parallel",), est),
        name="merge",
    )(y_ssd, y_att, u, u, gates, h, pool_w, pool_scale, w_up_ssd, w_up_attn, w_up_pool, w_out)


def _ffn_kernel(h_ref, nw_ref, wi_ref, wo_ref, fw_ref, o_ref, *, final):
    f32 = jnp.float32
    h = h_ref[...]
    hn = _bf(_rms(h, nw_ref[...]))
    out = h
    lo = 0
    for blk in FFN_BLOCKS:
        a = jnp.dot(hn, wi_ref[:, lo:lo + blk], preferred_element_type=f32)
        b = jnp.dot(hn, wi_ref[:, FFN_HIDDEN + lo:FFN_HIDDEN + lo + blk], preferred_element_type=f32)
        out = out + jnp.dot(_bf(a * _sigmoid(a) * b), wo_ref[lo:lo + blk, :], preferred_element_type=f32)
        lo += blk
    if final:
        out = _rms(out, fw_ref[...])
    o_ref[...] = out


def _ffn(h, norm_w, w_ffn_in, w_ffn_out, final_w, l, final):
    n = h.shape[0]
    tm = ROW_TILE
    row = pl.BlockSpec((tm, D_MODEL), lambda r: (r, 0))
    est = (3 * D_MODEL * FFN_HIDDEN * 2 + 4 * tm * D_MODEL * 4 + 6 * tm * max(FFN_BLOCKS) * 4
           + 4 * tm * D_MODEL * 4)
    return pl.pallas_call(
        functools.partial(_ffn_kernel, final=final),
        grid=(n // tm,),
        in_specs=[row, _layer_spec((1, D_MODEL), l), _layer_spec((D_MODEL, 2 * FFN_HIDDEN), l),
                  _layer_spec((FFN_HIDDEN, D_MODEL), l),
                  pl.BlockSpec((1, D_MODEL), lambda r: (0, 0))],
        out_specs=row,
        out_shape=jax.ShapeDtypeStruct((n, D_MODEL), jnp.float32),
        compiler_params=_params(("parallel",), est),
        name="ffn",
    )(h, norm_w, w_ffn_in, w_ffn_out, final_w)


def _pack_w_in(w_in):
    offs, acc = [], 0
    for s in (SSD_INNER, SSD_CONV_DIM, SSD_HEADS, ATT_WIDTH, ATT_KV_HEADS * ATT_HEAD_DIM,
              ATT_KV_HEADS * ATT_HEAD_DIM, IDX_HEADS * IDX_HEAD_DIM, IDX_HEAD_DIM, IDX_HEADS, POOL_WIDTH):
        acc += s
        offs.append(acc)
    z, xbc, dt, q, k, v, qi, ki, wi, u, gates = jnp.split(w_in, offs, axis=-1)
    q = q * (ATT_SCALE * LOG2_E)
    dh = ATT_HEAD_DIM
    kv = [k[..., g * dh:(g + 1) * dh] for g in range(ATT_KV_HEADS) for _ in range(2)]
    for g in range(ATT_KV_HEADS):
        kv += [v[..., g * dh:(g + 1) * dh], jnp.zeros_like(v[..., :dh])]
    pad = jnp.zeros(w_in.shape[:-1] + (SEG_MISC - SSD_HEADS - IDX_HEADS,), w_in.dtype)
    return _bf(jnp.concatenate([z, xbc, q] + kv + [qi, ki, ki, dt, wi, pad, u, gates], axis=-1))


def _pad_lanes(a):
    depth, w = a.shape
    return jnp.pad(a, ((0, 0), (0, LANES - w))).reshape(depth, 1, LANES)


def kernel(x, norm1_w, w_in, conv_w, conv_b, dt_bias, a_log, d_skip, ssd_norm_w, pool_w, pool_scale, w_up_ssd,
           w_up_attn, w_up_pool, w_out, norm2_w, w_ffn_in, w_ffn_out, final_norm_w):
    batch, seq, d = x.shape
    assert d == D_MODEL and seq % KEY_CHUNK == 0 and seq % ROW_TILE == 0 and (batch * seq) % ROW_TILE == 0
    assert seq % Q_TILE == 0 and seq % (SSD_CHUNK * SSD_STEP_CHUNKS) == 0
    depth = w_in.shape[0]
    row3 = lambda a: a.reshape(depth, 1, a.shape[-1])

    wcat = _pack_w_in(w_in)
    norm1 = row3(norm1_w)
    conv_b3 = row3(conv_b)
    dtb = _pad_lanes(dt_bias)
    alog = _pad_lanes(a_log)
    dskip_x = row3(jnp.repeat(d_skip, SSD_HEAD_DIM, axis=-1))
    ssd_nw = row3(ssd_norm_w)
    pool_s = row3(pool_scale)
    norm2 = row3(norm2_w)
    final_w = final_norm_w.reshape(1, D_MODEL)
    pool_wb, wus, wua, wup, wo, wfi, wfo = map(_bf, (pool_w, w_up_ssd, w_up_attn, w_up_pool, w_out, w_ffn_in,
                                                     w_ffn_out))

    h = x.reshape(batch * seq, D_MODEL)
    for l in range(depth):
        z, xbc, q, k, vt, qi, ki, misc, u, gates = _inproj(h, norm1, wcat, l)
        y_ssd = _ssd(xbc, z, misc, conv_w, conv_b3, dtb, alog, dskip_x, ssd_nw, l, batch, seq)
        y_att = _dsa(q, qi, misc, k, vt, ki, batch, seq)
        h = _merge(y_ssd, y_att, u, gates, h, pool_wb, pool_s, wus, wua, wup, wo, l, seq)
        h = _ffn(h, norm2, wfi, wfo, final_w, l, final=(l == depth - 1))
    return h.reshape(batch, seq, D_MODEL)
```

```python
import functools

import jax
import jax.numpy as jnp
from jax import lax
from jax.experimental import pallas as pl
from jax.experimental.pallas import tpu as pltpu

D_MODEL = 1024
DEPTH = 4
SSD_HEADS = 16
SSD_HEAD_DIM = 64
SSD_INNER = SSD_HEADS * SSD_HEAD_DIM
SSD_GROUPS = 2
SSD_STATE = 128
SSD_CONV_DIM = SSD_INNER + 2 * SSD_GROUPS * SSD_STATE
CONV_WIDTH = 4
SSD_CHUNK = 128
ATT_HEADS = 8
ATT_KV_HEADS = 2
ATT_HEAD_DIM = 64
ATT_WIDTH = ATT_HEADS * ATT_HEAD_DIM
IDX_HEADS = 4
IDX_HEAD_DIM = 64
TOPK_MAX = 256
POOL_WINDOWS = (2, 4, 8, 16)
POOL_GROUPS = 4
POOL_GROUP_DIM = 128
POOL_WIDTH = POOL_GROUPS * POOL_GROUP_DIM
N_BRANCH = 3
FFN_HIDDEN = 2816
EPS = 1e-6
ATT_SCALE = ATT_HEAD_DIM ** -0.5
LOG2_E = 1.4426950408889634
IDX_SCALE = (IDX_HEADS ** -0.5) * (IDX_HEAD_DIM ** -0.5)

LANES = 128
SUBLANES = 8
VMEM_BUDGET_BYTES = 60000 * 1024

ROW_TILE = 512
Q_TILE = 2 * LANES
KEY_CHUNK = 512
POOL_HALO = 16
MXU_TILE = 256
FFN_BLOCKS = (6 * MXU_TILE, FFN_HIDDEN - 6 * MXU_TILE)
SSD_STEP_CHUNKS = 2
PLANE_BLOCK = 32 * SUBLANES
BLOCKS_PER_CHUNK = KEY_CHUNK // PLANE_BLOCK
TIE_BLOCK = PLANE_BLOCK

SEG_Z = SSD_INNER
SEG_XBC = SSD_CONV_DIM
SEG_Q = ATT_WIDTH
SEG_K = ATT_KV_HEADS * 2 * ATT_HEAD_DIM
SEG_V = ATT_KV_HEADS * 2 * ATT_HEAD_DIM
SEG_QI = IDX_HEADS * IDX_HEAD_DIM
SEG_KI = 2 * IDX_HEAD_DIM
SEG_MISC = LANES
SEG_U = POOL_WIDTH
SEG_G = N_BRANCH * D_MODEL
SEGS = (SEG_Z, SEG_XBC, SEG_Q, SEG_K, SEG_V, SEG_QI, SEG_KI, SEG_MISC, SEG_U, SEG_G)
V_SEG = 4
W_CAT = sum(SEGS)
MISC_WI = SSD_HEADS

NEG_BIG = -1e30

_NT = (((1,), (1,)), ((), ()))


def _bf(x):
    return x.astype(jnp.bfloat16)


def _sigmoid(x):
    return 1.0 / (1.0 + jnp.exp(-x))


def _rms(x, w):
    return x * lax.rsqrt(jnp.mean(x * x, axis=-1, keepdims=True) + EPS) * w


def _params(sem, est_bytes):
    return pltpu.CompilerParams(dimension_semantics=sem,
                                vmem_limit_bytes=int(min(VMEM_BUDGET_BYTES, est_bytes)))


def _layer_spec(shape, l):
    nd = len(shape)
    return pl.BlockSpec((None,) + tuple(shape), lambda *_: (l,) + (0,) * nd, pipeline_mode=pl.Buffered(1))


def _inproj_kernel(h_ref, nw_ref, w_ref, *refs):
    out_refs, v_buf = refs[:-1], refs[-1]
    xn = _bf(_rms(h_ref[...], nw_ref[...]))
    off = 0
    for s, (ref, width) in enumerate(zip(out_refs, SEGS)):
        r = jnp.dot(xn, w_ref[:, off:off + width], preferred_element_type=jnp.float32)
        if s == V_SEG:
            v_buf[...] = r
            r = v_buf[...].T
        ref[...] = r.astype(ref.dtype)
        off += width


def _inproj(h, norm_w, wcat, l):
    n = h.shape[0]
    tm = ROW_TILE
    dts = (jnp.float32, jnp.float32, jnp.bfloat16, jnp.bfloat16, jnp.bfloat16, jnp.bfloat16, jnp.bfloat16,
           jnp.float32, jnp.float32, jnp.bfloat16)
    out_shape = [jax.ShapeDtypeStruct((n, w), dt) for w, dt in zip(SEGS, dts)]
    out_specs = [pl.BlockSpec((tm, w), lambda i: (i, 0)) for w in SEGS]
    out_shape[V_SEG] = jax.ShapeDtypeStruct((n // tm, SEG_V, tm), dts[V_SEG])
    out_specs[V_SEG] = pl.BlockSpec((None, SEG_V, tm), lambda i: (i, 0, 0))
    out_bytes = sum(tm * w * jnp.dtype(dt).itemsize for w, dt in zip(SEGS, dts))
    est = D_MODEL * W_CAT * 2 + 2 * tm * D_MODEL * 4 + 2 * out_bytes + 3 * tm * SEG_G * 4
    return pl.pallas_call(
        _inproj_kernel,
        grid=(n // tm,),
        in_specs=[pl.BlockSpec((tm, D_MODEL), lambda i: (i, 0)),
                  _layer_spec((1, D_MODEL), l),
                  _layer_spec((D_MODEL, W_CAT), l)],
        out_specs=out_specs,
        out_shape=out_shape,
        scratch_shapes=[pltpu.VMEM((tm, SEG_V), jnp.float32)],
        compiler_params=_params(("parallel",), est),
        name="inproj",
    )(h, norm_w, wcat)


def _col(a, j, width=LANES):
    return jnp.broadcast_to(a[:, j:j + 1], (a.shape[0], width))


def _expand_heads(a, spread):
    hi = _bf(a)
    lo = _bf(a - hi.astype(jnp.float32))
    return (jnp.dot(hi, spread, preferred_element_type=jnp.float32)
            + jnp.dot(lo, spread, preferred_element_type=jnp.float32))


def _silu(x):
    h = 0.5 * x
    return h + h * jnp.tanh(h)


def _ssd_kernel(xbc_ref, z_ref, misc_ref, cw_ref, cb_ref, dtb_ref, alog_ref, dskip_ref, nw_ref, spread_ref, o_ref,
                cbuf, state):
    L = SSD_CHUNK
    f32 = jnp.float32

    @pl.when(pl.program_id(1) == 0)
    def _():
        cbuf[0:SUBLANES, :] = jnp.zeros((SUBLANES, SSD_CONV_DIM), f32)
        state[...] = jnp.zeros(state.shape, f32)

    lane = lax.broadcasted_iota(jnp.int32, (L, LANES), 1)
    row = lax.broadcasted_iota(jnp.int32, (L, LANES), 0)
    left = lane < SSD_HEAD_DIM
    causal = lane <= row
    tri = jnp.where(causal, 1.0, 0.0)
    for sub in range(SSD_STEP_CHUNKS):
        _ssd_chunk(slice(sub * L, (sub + 1) * L), left, causal, tri, xbc_ref, z_ref, misc_ref, cw_ref, cb_ref, dtb_ref,
                   alog_ref, dskip_ref, nw_ref, spread_ref, o_ref, cbuf, state)


def _ssd_chunk(rows, left, causal, tri, xbc_ref, z_ref, misc_ref, cw_ref, cb_ref, dtb_ref, alog_ref, dskip_ref, nw_ref,
               spread_ref, o_ref, cbuf, state):
    L = SSD_CHUNK
    f32 = jnp.float32
    cbuf[SUBLANES:SUBLANES + L, :] = xbc_ref[rows, :]
    conv = cb_ref[...]
    for k in range(CONV_WIDTH):
        start = SUBLANES - (CONV_WIDTH - 1) + k
        conv = conv + cw_ref[k:k + 1, :] * cbuf[start:start + L, :]
    cbuf[0:SUBLANES, :] = cbuf[L:L + SUBLANES, :]
    xbc = _silu(conv)
    xs = xbc[:, :SSD_INNER]

    dtr = misc_ref[rows, :] + dtb_ref[...]
    dt = jnp.maximum(dtr, 0.0) + jnp.log1p(jnp.exp(-jnp.abs(dtr)))
    a = dt * (-jnp.exp(alog_ref[...]))
    a_cs = jnp.dot(tri, a, preferred_element_type=f32, precision=lax.Precision.HIGHEST)
    a_cs_t = a_cs.T
    ea = jnp.exp(a_cs)
    decay = jnp.exp(a_cs[L - 1:L, :] - a_cs)

    spread = spread_ref[...]
    x_all = xs * _expand_heads(dt, spread)
    ea_x = _expand_heads(ea, spread)
    xd_all = x_all * _expand_heads(decay, spread)

    gw = SSD_INNER // SSD_GROUPS
    y_parts = []
    for g in range(SSD_GROUPS):
        b_g = xbc[:, SSD_INNER + g * SSD_STATE:SSD_INNER + (g + 1) * SSD_STATE]
        c_off = SSD_INNER + SSD_GROUPS * SSD_STATE
        c_g = xbc[:, c_off + g * SSD_STATE:c_off + (g + 1) * SSD_STATE]
        cb = lax.dot_general(_bf(c_g), _bf(b_g), _NT, preferred_element_type=f32)
        st = state[g]
        y_off = jnp.dot(_bf(c_g), _bf(st), preferred_element_type=f32)
        new_st = jnp.dot(_bf(b_g.T), _bf(xd_all[:, g * gw:(g + 1) * gw]), preferred_element_type=f32)
        state[g] = st * ea_x[L - 1:L, g * gw:(g + 1) * gw] + new_st
        pairs = []
        for mm in range(SSD_HEADS // SSD_GROUPS // 2):
            m = g * (SSD_HEADS // SSD_GROUPS // 2) + mm
            lhs = []
            for hd in (2 * m, 2 * m + 1):
                seg = _col(a_cs, hd) - a_cs_t[hd:hd + 1, :]
                lhs.append(cb * jnp.where(causal, jnp.exp(seg), 0.0))
            xp = x_all[:, m * LANES:(m + 1) * LANES]
            rhs = jnp.concatenate([jnp.where(left, xp, 0.0), jnp.where(left, 0.0, xp)], axis=0)
            pairs.append(jnp.dot(_bf(jnp.concatenate(lhs, axis=1)), _bf(rhs), preferred_element_type=f32))
        y_parts.append(jnp.concatenate(pairs, axis=1) + y_off * ea_x[:, g * gw:(g + 1) * gw])
    y = jnp.concatenate(y_parts, axis=1) + dskip_ref[...] * xs
    o_ref[rows, :] = _bf(_rms(y * _silu(z_ref[rows, :]), nw_ref[...]))


def _ssd(xbc, z, misc, conv_w, conv_b, dtb, alog, dskip_x, norm_w, l, batch, seq):
    L = SSD_CHUNK * SSD_STEP_CHUNKS
    nc = seq // L
    row_spec = lambda w: pl.BlockSpec((L, w), lambda b, c: (b * nc + c, 0))
    head_of_col = lax.broadcasted_iota(jnp.int32, (LANES, SSD_INNER), 1) // SSD_HEAD_DIM
    spread = _bf(head_of_col == lax.broadcasted_iota(jnp.int32, (LANES, SSD_INNER), 0))
    est = (2 * L * (SSD_CONV_DIM + SSD_INNER + LANES) * 4 + 2 * L * SSD_INNER * 2
           + (L + 2 * SUBLANES) * SSD_CONV_DIM * 4 + SSD_GROUPS * SSD_STATE * SSD_INNER // SSD_GROUPS * 4
           + 40 * L * SSD_INNER * 4)
    return pl.pallas_call(
        _ssd_kernel,
        grid=(batch, nc),
        in_specs=[row_spec(SSD_CONV_DIM), row_spec(SSD_INNER), row_spec(LANES),
                  _layer_spec((CONV_WIDTH, SSD_CONV_DIM), l), _layer_spec((1, SSD_CONV_DIM), l),
                  _layer_spec((1, LANES), l), _layer_spec((1, LANES), l),
                  _layer_spec((1, SSD_INNER), l), _layer_spec((1, SSD_INNER), l),
                  pl.BlockSpec((LANES, SSD_INNER), lambda b, c: (0, 0))],
        out_specs=row_spec(SSD_INNER),
        out_shape=jax.ShapeDtypeStruct((batch * seq, SSD_INNER), jnp.bfloat16),
        scratch_shapes=[pltpu.VMEM((SSD_CHUNK + 2 * SUBLANES, SSD_CONV_DIM), jnp.float32),
                        pltpu.VMEM((SSD_GROUPS, SSD_STATE, SSD_INNER // SSD_GROUPS), jnp.float32)],
        compiler_params=_params(("parallel", "arbitrary"), est),
        name="ssd",
    )(xbc, z, misc, conv_w, conv_b, dtb, alog, dskip_x, norm_w, spread)


def _bit_planes(a):
    a = list(a)
    j, m = 16, 0x0000FFFF
    while j:
        for k in range(32):
            if not k & j:
                t = (a[k] ^ (a[k + j] >> j)) & m
                a[k] = a[k] ^ t
                a[k + j] = a[k + j] ^ (t << j)
        j >>= 1
        m ^= m << j
    return a


def _dsa_kernel(q_ref, qi_ref, misc_ref, k_ref, vt_ref, ki_ref, tri_ref, o_ref, plane_ref, bits_ref, s_ref, m_ref,
                acc_ref, *, topk):
    f32 = jnp.float32
    i32 = jnp.int32
    KC = KEY_CHUNK
    i = pl.program_id(1)
    nch = (i * Q_TILE + Q_TILE + KC - 1) // KC
    QT = Q_TILE
    qpos = i * QT + lax.broadcasted_iota(i32, (1, QT), 1)
    left = lax.broadcasted_iota(i32, (1, LANES), 1) < ATT_HEAD_DIM

    misc_t = misc_ref[...].T
    qi = qi_ref[...].astype(f32)
    qi_heads, w_heads = [], []
    for h in range(IDX_HEADS):
        qp = qi[:, (h // 2) * LANES:(h // 2 + 1) * LANES]
        qi_heads.append(_bf(jnp.where(left, qp, 0.0) if h % 2 == 0 else jnp.where(left, 0.0, qp)))
        w_heads.append(misc_t[MISC_WI + h:MISC_WI + h + 1, :] * IDX_SCALE)
    qi_pairs = [jnp.concatenate(qi_heads[h:h + 2], axis=0) for h in range(0, IDX_HEADS, 2)]

    def score_body(c, carry):
        row0 = pl.multiple_of(c * KC, KC)
        ki_c = ki_ref[pl.ds(row0, KC), :]
        sc = jnp.zeros((KC, QT), f32)
        for hp, qi_pair in enumerate(qi_pairs):
            s2 = lax.dot_general(ki_c, qi_pair, _NT, preferred_element_type=f32)
            for j in range(2):
                s = s2[:, j * QT:(j + 1) * QT]
                sc = sc + jnp.maximum(s, 0.0) * w_heads[2 * hp + j]
        sc = jnp.where(sc == 0.0, 0.0, sc)
        bits = lax.bitcast_convert_type(sc, i32)
        key = bits ^ ((bits >> 31) & 0x7FFFFFFF)
        for blk in range(BLOCKS_PER_CHUNK):
            planes = _bit_planes([key[blk * PLANE_BLOCK + SUBLANES * j:blk * PLANE_BLOCK + SUBLANES * (j + 1)]
                                  for j in range(32)])
            planes[0] = planes[0] ^ -1
            for p in range(32):
                plane_ref[p, BLOCKS_PER_CHUNK * c + blk] = planes[p]
        return carry

    lax.fori_loop(0, nch, score_body, 0)

    nblk = plane_ref.shape[1]

    def clear_body(c, carry):
        for blk in range(BLOCKS_PER_CHUNK):
            for p in range(32):
                plane_ref[p, BLOCKS_PER_CHUNK * c + blk] = jnp.zeros((SUBLANES, QT), i32)
        return carry

    lax.fori_loop(nch, nblk // BLOCKS_PER_CHUNK, clear_body, 0)

    def pass_body(p, carry):
        remaining, took = carry
        plane = plane_ref[p]
        alive = bits_ref[0]
        hit = alive & plane
        ones = jnp.sum(lax.population_count(hit), axis=0).astype(f32)
        ones = jnp.sum(ones, axis=0, keepdims=True)
        take = ones >= remaining
        drop = jnp.where(take, 0, -1)
        bits_ref[0] = alive & (plane ^ drop)
        bits_ref[1] = bits_ref[1] | (hit & drop)
        return jnp.where(take, remaining, remaining - ones), jnp.where(take, 1.0, took)

    first_row = (PLANE_BLOCK * lax.broadcasted_iota(i32, (nblk, SUBLANES, QT), 0)
                 + lax.broadcasted_iota(i32, (nblk, SUBLANES, QT), 1))
    n_adm = jnp.clip(((qpos - first_row) >> 3) + 1, 0, 32)
    top_bits = lax.shift_left(jnp.full((nblk, SUBLANES, QT), -1, i32), 32 - jnp.maximum(n_adm, 1))
    bits_ref[0] = jnp.where(n_adm >= 1, top_bits, 0)
    bits_ref[1] = jnp.zeros((nblk, SUBLANES, QT), i32)
    need, took = lax.fori_loop(0, 32, pass_body, (jnp.full((1, QT), float(topk), f32), jnp.zeros((1, QT), f32)))
    need = jnp.where(took > 0.0, need, 0.0)
    tri = tri_ref[...]

    def unpack(words):
        return jnp.concatenate([(words >> (31 - j)) & 1 for j in range(32)], axis=0).astype(f32)

    def select_bias(blk, seen):
        tie = unpack(bits_ref[0, blk])
        rank = seen + jnp.dot(tri, _bf(tie), preferred_element_type=f32)
        sel = unpack(bits_ref[1, blk]) + jnp.where(rank <= need, tie, 0.0)
        return (sel - 1.0) * (-NEG_BIG), rank[PLANE_BLOCK - 1:PLANE_BLOCK]

    q = q_ref[...].astype(f32)
    hpg = ATT_HEADS // ATT_KV_HEADS
    q_stacks = []
    for g in range(ATT_KV_HEADS):
        parts = []
        for hh in range(hpg):
            h = g * hpg + hh
            qp = q[:, (h // 2) * LANES:(h // 2 + 1) * LANES]
            parts.append(jnp.where(left, qp, 0.0) if h % 2 == 0 else jnp.where(left, 0.0, qp))
        q_stacks.append(_bf(jnp.concatenate(parts, axis=0)))
    HB = PLANE_BLOCK
    ones_lo = _bf(jnp.where(lax.broadcasted_iota(i32, (LANES, HB), 0) < ATT_HEAD_DIM, 0.0, 1.0))
    m_ref[...] = jnp.full(m_ref.shape, NEG_BIG, f32)
    acc_ref[...] = jnp.zeros(acc_ref.shape, f32)

    def logits(c, half, seen):
        row0 = c * KC + half * HB
        row0 = row0 if isinstance(row0, int) else pl.multiple_of(row0, HB)
        bias, seen = select_bias(BLOCKS_PER_CHUNK * c + half, seen)
        bias = jnp.concatenate([bias] * hpg, axis=1)
        for g in range(ATT_KV_HEADS):
            k2 = k_ref[pl.ds(row0, HB), g * LANES:(g + 1) * LANES]
            s_ref[half, g] = lax.dot_general(k2, q_stacks[g], _NT, preferred_element_type=f32) + bias
        return seen

    def softmax_pv(c, half):
        for g in range(ATT_KV_HEADS):
            v1 = vt_ref[c, g * LANES:(g + 1) * LANES, half * HB:(half + 1) * HB] + ones_lo
            s = s_ref[half, g]
            m_prev = m_ref[g]
            m_new = jnp.maximum(m_prev, jnp.max(s, axis=0, keepdims=True))
            alpha = jnp.exp2(m_prev - m_new)
            p = jnp.exp2(s - m_new)
            acc_ref[g] = alpha * acc_ref[g] + jnp.dot(v1, _bf(p), preferred_element_type=f32)
            m_ref[g] = m_new

    def chunk_body(c, seen):
        seen = logits(c, 1, seen)
        softmax_pv(c, 0)
        seen = logits(c + 1, 0, seen)
        softmax_pv(c, 1)
        return seen

    seen = lax.fori_loop(0, nch - 1, chunk_body, logits(0, 0, jnp.zeros((1, QT), f32)))
    logits(nch - 1, 1, seen)
    softmax_pv(nch - 1, 0)
    softmax_pv(nch - 1, 1)

    for qt in range(QT // LANES):
        outs = []
        for g in range(ATT_KV_HEADS):
            heads = []
            for hh in range(hpg):
                col = hh * QT + qt * LANES
                blk = acc_ref[g, :, col:col + LANES].T
                heads.append(blk / pltpu.roll(blk, ATT_HEAD_DIM, axis=1))
            for pp in range(hpg // 2):
                outs.append(jnp.where(left, heads[2 * pp], pltpu.roll(heads[2 * pp + 1], ATT_HEAD_DIM, axis=1)))
        o_ref[qt * LANES:(qt + 1) * LANES, :] = _bf(jnp.concatenate(outs, axis=1))


def _dsa(q, qi, misc, k, vt, ki, batch, seq):
    nq = seq // Q_TILE
    nkc = seq // KEY_CHUNK
    topk = min(TOPK_MAX, seq // 4)
    hpg = ATT_HEADS // ATT_KV_HEADS
    qrow = lambda w: pl.BlockSpec((Q_TILE, w), lambda b, i: (b * nq + i, 0))
    seq_spec = lambda w: pl.BlockSpec((seq, w), lambda b, i: (b, 0))
    tri = _bf(jnp.tril(jnp.ones((TIE_BLOCK, TIE_BLOCK), jnp.float32)))
    assert vt.shape == (batch * nkc, SEG_V, KEY_CHUNK)
    assert BLOCKS_PER_CHUNK == 2
    s_bytes = 2 * ATT_KV_HEADS * hpg * Q_TILE * KEY_CHUNK * 4
    est = (2 * seq * (SEG_K + SEG_V + SEG_KI) * 2 + nkc * KEY_CHUNK * Q_TILE * 4 + s_bytes
           + ATT_KV_HEADS * (LANES + SUBLANES) * hpg * Q_TILE * 4 + 8 * hpg * Q_TILE * KEY_CHUNK * 4)
    return pl.pallas_call(
        functools.partial(_dsa_kernel, topk=topk),
        grid=(batch, nq),
        in_specs=[qrow(SEG_Q), qrow(SEG_QI), qrow(SEG_MISC), seq_spec(SEG_K),
                  pl.BlockSpec((nkc, SEG_V, KEY_CHUNK), lambda b, i: (b, 0, 0)), seq_spec(SEG_KI),
                  pl.BlockSpec((TIE_BLOCK, TIE_BLOCK), lambda b, i: (0, 0))],
        out_specs=qrow(ATT_WIDTH),
        out_shape=jax.ShapeDtypeStruct((batch * seq, ATT_WIDTH), jnp.bfloat16),
        scratch_shapes=[pltpu.VMEM((32, nkc * BLOCKS_PER_CHUNK, SUBLANES, Q_TILE), jnp.int32),
                        pltpu.VMEM((2, nkc * BLOCKS_PER_CHUNK, SUBLANES, Q_TILE), jnp.int32),
                        pltpu.VMEM((BLOCKS_PER_CHUNK, ATT_KV_HEADS, PLANE_BLOCK, hpg * Q_TILE), jnp.float32),
                        pltpu.VMEM((ATT_KV_HEADS, 1, hpg * Q_TILE), jnp.float32),
                        pltpu.VMEM((ATT_KV_HEADS, LANES, hpg * Q_TILE), jnp.float32)],
        compiler_params=_params(("parallel", "arbitrary"), est),
        name="dsa",
    )(q, qi, misc, k, vt, ki, tri)


def _merge_kernel(yssd_ref, yatt_ref, u_ref, uh_ref, g_ref, h_ref, pw_ref, ps_ref, wus_ref, wua_ref, wup_ref,
                  wo_ref, o_ref, ubuf, *, tiles_per_seq):
    f32 = jnp.float32
    tm = u_ref.shape[0]
    it = pl.program_id(0) % tiles_per_seq
    ubuf[POOL_HALO:POOL_HALO + tm, :] = u_ref[...]
    ubuf[0:POOL_HALO, :] = jnp.where(it == 0, 0.0, uh_ref[...])
    t1 = (it * tm + 1 + lax.broadcasted_iota(jnp.int32, (tm, POOL_GROUP_DIM), 0)).astype(f32)
    mixed = []
    for gi, win in enumerate(POOL_WINDOWS):
        cols = slice(gi * POOL_GROUP_DIM, (gi + 1) * POOL_GROUP_DIM)
        cur = ubuf[POOL_HALO:POOL_HALO + tm, cols]
        s = cur
        for d in range(1, win):
            s = s + ubuf[POOL_HALO - d:POOL_HALO - d + tm, cols]
        pooled = s / jnp.minimum(t1, float(win)) - cur
        mixed.append(jnp.dot(_bf(pooled), pw_ref[gi], preferred_element_type=f32))
    ypool = _bf(jnp.concatenate(mixed, axis=1) * ps_ref[...])

    g = g_ref[...].astype(f32)
    merged = (_sigmoid(g[:, :D_MODEL]) * jnp.dot(yssd_ref[...], wus_ref[...], preferred_element_type=f32)
              + _sigmoid(g[:, D_MODEL:2 * D_MODEL]) * jnp.dot(yatt_ref[...], wua_ref[...], preferred_element_type=f32)
              + _sigmoid(g[:, 2 * D_MODEL:]) * jnp.dot(ypool, wup_ref[...], preferred_element_type=f32))
    o_ref[...] = h_ref[...] + jnp.dot(_bf(merged), wo_ref[...], preferred_element_type=f32)


def _merge(y_ssd, y_att, u, gates, h, pool_w, pool_scale, w_up_ssd, w_up_attn, w_up_pool, w_out, l, seq):
    n = h.shape[0]
    tm = ROW_TILE
    tiles_per_seq = seq // tm
    row = lambda w: pl.BlockSpec((tm, w), lambda r: (r, 0))
    halo = pl.BlockSpec((POOL_HALO, POOL_WIDTH), lambda r: (jnp.maximum(r * (tm // POOL_HALO) - 1, 0), 0))
    w_bytes = (POOL_GROUPS * POOL_GROUP_DIM ** 2 + (SSD_INNER + ATT_WIDTH + POOL_WIDTH + D_MODEL) * D_MODEL) * 2
    est = (w_bytes + 2 * tm * ((SSD_INNER + ATT_WIDTH) * 2 + (POOL_WIDTH + SEG_G + 2 * D_MODEL) * 4)
           + (tm + POOL_HALO) * POOL_WIDTH * 4 + 12 * tm * D_MODEL * 4)
    return pl.pallas_call(
        functools.partial(_merge_kernel, tiles_per_seq=tiles_per_seq),
        grid=(n // tm,),
        in_specs=[row(SSD_INNER), row(ATT_WIDTH), row(POOL_WIDTH), halo, row(SEG_G), row(D_MODEL),
                  _layer_spec((POOL_GROUPS, POOL_GROUP_DIM, POOL_GROUP_DIM), l), _layer_spec((1, POOL_WIDTH), l),
                  _layer_spec((SSD_INNER, D_MODEL), l), _layer_spec((ATT_WIDTH, D_MODEL), l),
                  _layer_spec((POOL_WIDTH, D_MODEL), l), _layer_spec((D_MODEL, D_MODEL), l)],
        out_specs=row(D_MODEL),
        out_shape=jax.ShapeDtypeStruct((n, D_MODEL), jnp.float32),
        scratch_shapes=[pltpu.VMEM((tm + POOL_HALO, POOL_WIDTH), jnp.float32)],
        compiler_params=_params(("parallel",), est),
        name="merge",
    )(y_ssd, y_att, u, u, gates, h, pool_w, pool_scale, w_up_ssd, w_up_attn, w_up_pool, w_out)


def _ffn_kernel(h_ref, nw_ref, wi_ref, wo_ref, fw_ref, o_ref, *, final):
    f32 = jnp.float32
    h = h_ref[...]
    hn = _bf(_rms(h, nw_ref[...]))
    out = h
    lo = 0
    for blk in FFN_BLOCKS:
        a = jnp.dot(hn, wi_ref[:, lo:lo + blk], preferred_element_type=f32)
        b = jnp.dot(hn, wi_ref[:, FFN_HIDDEN + lo:FFN_HIDDEN + lo + blk], preferred_element_type=f32)
        out = out + jnp.dot(_bf(a * _sigmoid(a) * b), wo_ref[lo:lo + blk, :], preferred_element_type=f32)
        lo += blk
    if final:
        out = _rms(out, fw_ref[...])
    o_ref[...] = out


def _ffn(h, norm_w, w_ffn_in, w_ffn_out, final_w, l, final):
    n = h.shape[0]
    tm = ROW_TILE
    row = pl.BlockSpec((tm, D_MODEL), lambda r: (r, 0))
    est = (3 * D_MODEL * FFN_HIDDEN * 2 + 4 * tm * D_MODEL * 4 + 6 * tm * max(FFN_BLOCKS) * 4
           + 4 * tm * D_MODEL * 4)
    return pl.pallas_call(
        functools.partial(_ffn_kernel, final=final),
        grid=(n // tm,),
        in_specs=[row, _layer_spec((1, D_MODEL), l), _layer_spec((D_MODEL, 2 * FFN_HIDDEN), l),
                  _layer_spec((FFN_HIDDEN, D_MODEL), l),
                  pl.BlockSpec((1, D_MODEL), lambda r: (0, 0))],
        out_specs=row,
        out_shape=jax.ShapeDtypeStruct((n, D_MODEL), jnp.float32),
        compiler_params=_params(("parallel",), est),
        name="ffn",
    )(h, norm_w, w_ffn_in, w_ffn_out, final_w)


def _pack_w_in(w_in):
    offs, acc = [], 0
    for s in (SSD_INNER, SSD_CONV_DIM, SSD_HEADS, ATT_WIDTH, ATT_KV_HEADS * ATT_HEAD_DIM,
              ATT_KV_HEADS * ATT_HEAD_DIM, IDX_HEADS * IDX_HEAD_DIM, IDX_HEAD_DIM, IDX_HEADS, POOL_WIDTH):
        acc += s
        offs.append(acc)
    z, xbc, dt, q, k, v, qi, ki, wi, u, gates = jnp.split(w_in, offs, axis=-1)
    q = q * (ATT_SCALE * LOG2_E)
    dh = ATT_HEAD_DIM
    kv = [k[..., g * dh:(g + 1) * dh] for g in range(ATT_KV_HEADS) for _ in range(2)]
    for g in range(ATT_KV_HEADS):
        kv += [v[..., g * dh:(g + 1) * dh], jnp.zeros_like(v[..., :dh])]
    pad = jnp.zeros(w_in.shape[:-1] + (SEG_MISC - SSD_HEADS - IDX_HEADS,), w_in.dtype)
    return _bf(jnp.concatenate([z, xbc, q] + kv + [qi, ki, ki, dt, wi, pad, u, gates], axis=-1))


def _pad_lanes(a):
    depth, w = a.shape
    return jnp.pad(a, ((0, 0), (0, LANES - w))).reshape(depth, 1, LANES)


def kernel(x, norm1_w, w_in, conv_w, conv_b, dt_bias, a_log, d_skip, ssd_norm_w, pool_w, pool_scale, w_up_ssd,
           w_up_attn, w_up_pool, w_out, norm2_w, w_ffn_in, w_ffn_out, final_norm_w):
    batch, seq, d = x.shape
    assert d == D_MODEL and seq % KEY_CHUNK == 0 and seq % ROW_TILE == 0 and (batch * seq) % ROW_TILE == 0
    assert seq % Q_TILE == 0 and seq % (SSD_CHUNK * SSD_STEP_CHUNKS) == 0
    depth = w_in.shape[0]
    row3 = lambda a: a.reshape(depth, 1, a.shape[-1])

    wcat = _pack_w_in(w_in)
    norm1 = row3(norm1_w)
    conv_b3 = row3(conv_b)
    dtb = _pad_lanes(dt_bias)
    alog = _pad_lanes(a_log)
    dskip_x = row3(jnp.repeat(d_skip, SSD_HEAD_DIM, axis=-1))
    ssd_nw = row3(ssd_norm_w)
    pool_s = row3(pool_scale)
    norm2 = row3(norm2_w)
    final_w = final_norm_w.reshape(1, D_MODEL)
    pool_wb, wus, wua, wup, wo, wfi, wfo = map(_bf, (pool_w, w_up_ssd, w_up_attn, w_up_pool, w_out, w_ffn_in,
                                                     w_ffn_out))

    h = x.reshape(batch * seq, D_MODEL)
    for l in range(depth):
        z, xbc, q, k, vt, qi, ki, misc, u, gates = _inproj(h, norm1, wcat, l)
        y_ssd = _ssd(xbc, z, misc, conv_w, conv_b3, dtb, alog, dskip_x, ssd_nw, l, batch, seq)
        y_att = _dsa(q, qi, misc, k, vt, ki, batch, seq)
        h = _merge(y_ssd, y_att, u, gates, h, pool_wb, pool_s, wus, wua, wup, wo, l, seq)
        h = _ffn(h, norm2, wfi, wfo, final_w, l, final=(l == depth - 1))
    return h.reshape(batch, seq, D_MODEL)
```

```python
import functools

import jax
import jax.numpy as jnp
from jax import lax
from jax.experimental import pallas as pl
from jax.experimental.pallas import tpu as pltpu

D_MODEL = 1024
DEPTH = 4
SSD_HEADS = 16
SSD_HEAD_DIM = 64
SSD_INNER = SSD_HEADS * SSD_HEAD_DIM
SSD_GROUPS = 2
SSD_STATE = 128
SSD_CONV_DIM = SSD_INNER + 2 * SSD_GROUPS * SSD_STATE
CONV_WIDTH = 4
SSD_CHUNK = 128
ATT_HEADS = 8
ATT_KV_HEADS = 2
ATT_HEAD_DIM = 64
ATT_WIDTH = ATT_HEADS * ATT_HEAD_DIM
IDX_HEADS = 4
IDX_HEAD_DIM = 64
TOPK_MAX = 256
POOL_WINDOWS = (2, 4, 8, 16)
POOL_GROUPS = 4
POOL_GROUP_DIM = 128
POOL_WIDTH = POOL_GROUPS * POOL_GROUP_DIM
N_BRANCH = 3
FFN_HIDDEN = 2816
EPS = 1e-6
ATT_SCALE = ATT_HEAD_DIM ** -0.5
LOG2_E = 1.4426950408889634
IDX_SCALE = (IDX_HEADS ** -0.5) * (IDX_HEAD_DIM ** -0.5)

LANES = 128
SUBLANES = 8
VMEM_BUDGET_BYTES = 60000 * 1024

ROW_TILE = 512
Q_TILE = 2 * LANES
KEY_CHUNK = 512
POOL_HALO = 16
MXU_TILE = 256
FFN_BLOCKS = (6 * MXU_TILE, FFN_HIDDEN - 6 * MXU_TILE)
SSD_STEP_CHUNKS = 4
PLANE_BLOCK = 32 * SUBLANES
BLOCKS_PER_CHUNK = KEY_CHUNK // PLANE_BLOCK
TIE_BLOCK = PLANE_BLOCK

SEG_Z = SSD_INNER
SEG_XBC = SSD_CONV_DIM
SEG_Q = ATT_WIDTH
SEG_K = ATT_KV_HEADS * 2 * ATT_HEAD_DIM
SEG_V = ATT_KV_HEADS * 2 * ATT_HEAD_DIM
SEG_QI = IDX_HEADS * IDX_HEAD_DIM
SEG_KI = 2 * IDX_HEAD_DIM
SEG_MISC = LANES
SEG_U = POOL_WIDTH
SEG_G = N_BRANCH * D_MODEL
SEGS = (SEG_Z, SEG_XBC, SEG_Q, SEG_K, SEG_V, SEG_QI, SEG_KI, SEG_MISC, SEG_U, SEG_G)
V_SEG = 4
W_CAT = sum(SEGS)
MISC_WI = SSD_HEADS

NEG_BIG = -1e30

_NT = (((1,), (1,)), ((), ()))


def _bf(x):
    return x.astype(jnp.bfloat16)


def _sigmoid(x):
    return 1.0 / (1.0 + jnp.exp(-x))


def _rms(x, w):
    return x * lax.rsqrt(jnp.mean(x * x, axis=-1, keepdims=True) + EPS) * w


def _params(sem, est_bytes):
    return pltpu.CompilerParams(dimension_semantics=sem,
                                vmem_limit_bytes=int(min(VMEM_BUDGET_BYTES, est_bytes)))


def _layer_spec(shape, l):
    nd = len(shape)
    return pl.BlockSpec((None,) + tuple(shape), lambda *_: (l,) + (0,) * nd, pipeline_mode=pl.Buffered(1))


def _inproj_kernel(h_ref, nw_ref, w_ref, *refs):
    out_refs, v_buf = refs[:-1], refs[-1]
    xn = _bf(_rms(h_ref[...], nw_ref[...]))
    off = 0
    for s, (ref, width) in enumerate(zip(out_refs, SEGS)):
        r = jnp.dot(xn, w_ref[:, off:off + width], preferred_element_type=jnp.float32)
        if s == V_SEG:
            v_buf[...] = r
            r = v_buf[...].T
        ref[...] = r.astype(ref.dtype)
        off += width


def _inproj(h, norm_w, wcat, l):
    n = h.shape[0]
    tm = ROW_TILE
    dts = (jnp.float32, jnp.float32, jnp.bfloat16, jnp.bfloat16, jnp.bfloat16, jnp.bfloat16, jnp.bfloat16,
           jnp.float32, jnp.float32, jnp.bfloat16)
    out_shape = [jax.ShapeDtypeStruct((n, w), dt) for w, dt in zip(SEGS, dts)]
    out_specs = [pl.BlockSpec((tm, w), lambda i: (i, 0)) for w in SEGS]
    out_shape[V_SEG] = jax.ShapeDtypeStruct((n // tm, SEG_V, tm), dts[V_SEG])
    out_specs[V_SEG] = pl.BlockSpec((None, SEG_V, tm), lambda i: (i, 0, 0))
    out_bytes = sum(tm * w * jnp.dtype(dt).itemsize for w, dt in zip(SEGS, dts))
    est = D_MODEL * W_CAT * 2 + 2 * tm * D_MODEL * 4 + 2 * out_bytes + 3 * tm * SEG_G * 4
    return pl.pallas_call(
        _inproj_kernel,
        grid=(n // tm,),
        in_specs=[pl.BlockSpec((tm, D_MODEL), lambda i: (i, 0)),
                  _layer_spec((1, D_MODEL), l),
                  _layer_spec((D_MODEL, W_CAT), l)],
        out_specs=out_specs,
        out_shape=out_shape,
        scratch_shapes=[pltpu.VMEM((tm, SEG_V), jnp.float32)],
        compiler_params=_params(("parallel",), est),
        name="inproj",
    )(h, norm_w, wcat)


def _col(a, j, width=LANES):
    return jnp.broadcast_to(a[:, j:j + 1], (a.shape[0], width))


def _expand_heads(a, spread):
    hi = _bf(a)
    lo = _bf(a - hi.astype(jnp.float32))
    return (jnp.dot(hi, spread, preferred_element_type=jnp.float32)
            + jnp.dot(lo, spread, preferred_element_type=jnp.float32))


def _silu(x):
    h = 0.5 * x
    return h + h * jnp.tanh(h)


def _ssd_kernel(xbc_ref, z_ref, misc_ref, cw_ref, cb_ref, dtb_ref, alog_ref, dskip_ref, nw_ref, spread_ref, o_ref,
                cbuf, state):
    L = SSD_CHUNK
    f32 = jnp.float32

    @pl.when(pl.program_id(1) == 0)
    def _():
        cbuf[0:SUBLANES, :] = jnp.zeros((SUBLANES, SSD_CONV_DIM), f32)
        state[...] = jnp.zeros(state.shape, f32)

    lane = lax.broadcasted_iota(jnp.int32, (L, LANES), 1)
    row = lax.broadcasted_iota(jnp.int32, (L, LANES), 0)
    left = lane < SSD_HEAD_DIM
    causal = lane <= row
    tri = jnp.where(causal, 1.0, 0.0)
    for sub in range(SSD_STEP_CHUNKS):
        _ssd_chunk(slice(sub * L, (sub + 1) * L), left, causal, tri, xbc_ref, z_ref, misc_ref, cw_ref, cb_ref, dtb_ref,
                   alog_ref, dskip_ref, nw_ref, spread_ref, o_ref, cbuf, state)


def _ssd_chunk(rows, left, causal, tri, xbc_ref, z_ref, misc_ref, cw_ref, cb_ref, dtb_ref, alog_ref, dskip_ref, nw_ref,
               spread_ref, o_ref, cbuf, state):
    L = SSD_CHUNK
    f32 = jnp.float32
    cbuf[SUBLANES:SUBLANES + L, :] = xbc_ref[rows, :]
    conv = cb_ref[...]
    for k in range(CONV_WIDTH):
        start = SUBLANES - (CONV_WIDTH - 1) + k
        conv = conv + cw_ref[k:k + 1, :] * cbuf[start:start + L, :]
    cbuf[0:SUBLANES, :] = cbuf[L:L + SUBLANES, :]
    xbc = _silu(conv)
    xs = xbc[:, :SSD_INNER]

    dtr = misc_ref[rows, :] + dtb_ref[...]
    dt = jnp.maximum(dtr, 0.0) + jnp.log1p(jnp.exp(-jnp.abs(dtr)))
    a = dt * (-jnp.exp(alog_ref[...]))
    a_cs = jnp.dot(tri, a, preferred_element_type=f32, precision=lax.Precision.HIGHEST)
    a_cs_t = a_cs.T
    ea = jnp.exp(a_cs)
    decay = jnp.exp(a_cs[L - 1:L, :] - a_cs)

    spread = spread_ref[...]
    x_all = xs * _expand_heads(dt, spread)
    ea_x = _expand_heads(ea, spread)
    xd_all = x_all * _expand_heads(decay, spread)

    gw = SSD_INNER // SSD_GROUPS
    y_parts = []
    for g in range(SSD_GROUPS):
        b_g = xbc[:, SSD_INNER + g * SSD_STATE:SSD_INNER + (g + 1) * SSD_STATE]
        c_off = SSD_INNER + SSD_GROUPS * SSD_STATE
        c_g = xbc[:, c_off + g * SSD_STATE:c_off + (g + 1) * SSD_STATE]
        cb = lax.dot_general(_bf(c_g), _bf(b_g), _NT, preferred_element_type=f32)
        st = state[g]
        y_off = jnp.dot(_bf(c_g), _bf(st), preferred_element_type=f32)
        new_st = jnp.dot(_bf(b_g.T), _bf(xd_all[:, g * gw:(g + 1) * gw]), preferred_element_type=f32)
        state[g] = st * ea_x[L - 1:L, g * gw:(g + 1) * gw] + new_st
        pairs = []
        for mm in range(SSD_HEADS // SSD_GROUPS // 2):
            m = g * (SSD_HEADS // SSD_GROUPS // 2) + mm
            lhs = []
            for hd in (2 * m, 2 * m + 1):
                seg = _col(a_cs, hd) - a_cs_t[hd:hd + 1, :]
                lhs.append(cb * jnp.where(causal, jnp.exp(seg), 0.0))
            xp = x_all[:, m * LANES:(m + 1) * LANES]
            rhs = jnp.concatenate([jnp.where(left, xp, 0.0), jnp.where(left, 0.0, xp)], axis=0)
            pairs.append(jnp.dot(_bf(jnp.concatenate(lhs, axis=1)), _bf(rhs), preferred_element_type=f32))
        y_parts.append(jnp.concatenate(pairs, axis=1) + y_off * ea_x[:, g * gw:(g + 1) * gw])
    y = jnp.concatenate(y_parts, axis=1) + dskip_ref[...] * xs
    o_ref[rows, :] = _bf(_rms(y * _silu(z_ref[rows, :]), nw_ref[...]))


def _ssd(xbc, z, misc, conv_w, conv_b, dtb, alog, dskip_x, norm_w, l, batch, seq):
    L = SSD_CHUNK * SSD_STEP_CHUNKS
    nc = seq // L
    row_spec = lambda w: pl.BlockSpec((L, w), lambda b, c: (b * nc + c, 0))
    head_of_col = lax.broadcasted_iota(jnp.int32, (LANES, SSD_INNER), 1) // SSD_HEAD_DIM
    spread = _bf(head_of_col == lax.broadcasted_iota(jnp.int32, (LANES, SSD_INNER), 0))
    est = (2 * L * (SSD_CONV_DIM + SSD_INNER + LANES) * 4 + 2 * L * SSD_INNER * 2
           + (L + 2 * SUBLANES) * SSD_CONV_DIM * 4 + SSD_GROUPS * SSD_STATE * SSD_INNER // SSD_GROUPS * 4
           + 40 * L * SSD_INNER * 4)
    return pl.pallas_call(
        _ssd_kernel,
        grid=(batch, nc),
        in_specs=[row_spec(SSD_CONV_DIM), row_spec(SSD_INNER), row_spec(LANES),
                  _layer_spec((CONV_WIDTH, SSD_CONV_DIM), l), _layer_spec((1, SSD_CONV_DIM), l),
                  _layer_spec((1, LANES), l), _layer_spec((1, LANES), l),
                  _layer_spec((1, SSD_INNER), l), _layer_spec((1, SSD_INNER), l),
                  pl.BlockSpec((LANES, SSD_INNER), lambda b, c: (0, 0))],
        out_specs=row_spec(SSD_INNER),
        out_shape=jax.ShapeDtypeStruct((batch * seq, SSD_INNER), jnp.bfloat16),
        scratch_shapes=[pltpu.VMEM((SSD_CHUNK + 2 * SUBLANES, SSD_CONV_DIM), jnp.float32),
                        pltpu.VMEM((SSD_GROUPS, SSD_STATE, SSD_INNER // SSD_GROUPS), jnp.float32)],
        compiler_params=_params(("parallel", "arbitrary"), est),
        name="ssd",
    )(xbc, z, misc, conv_w, conv_b, dtb, alog, dskip_x, norm_w, spread)


def _bit_planes(a):
    a = list(a)
    j, m = 16, 0x0000FFFF
    while j:
        for k in range(32):
            if not k & j:
                t = (a[k] ^ (a[k + j] >> j)) & m
                a[k] = a[k] ^ t
                a[k + j] = a[k + j] ^ (t << j)
        j >>= 1
        m ^= m << j
    return a


def _dsa_kernel(q_ref, qi_ref, misc_ref, k_ref, vt_ref, ki_ref, tri_ref, o_ref, plane_ref, bits_ref, s_ref, m_ref,
                acc_ref, *, topk):
    f32 = jnp.float32
    i32 = jnp.int32
    KC = KEY_CHUNK
    i = pl.program_id(1)
    nch = (i * Q_TILE + Q_TILE + KC - 1) // KC
    QT = Q_TILE
    qpos = i * QT + lax.broadcasted_iota(i32, (1, QT), 1)
    left = lax.broadcasted_iota(i32, (1, LANES), 1) < ATT_HEAD_DIM

    misc_t = misc_ref[...].T
    qi = qi_ref[...].astype(f32)
    qi_heads, w_heads = [], []
    for h in range(IDX_HEADS):
        qp = qi[:, (h // 2) * LANES:(h // 2 + 1) * LANES]
        qi_heads.append(_bf(jnp.where(left, qp, 0.0) if h % 2 == 0 else jnp.where(left, 0.0, qp)))
        w_heads.append(misc_t[MISC_WI + h:MISC_WI + h + 1, :] * IDX_SCALE)
    qi_pairs = [jnp.concatenate(qi_heads[h:h + 2], axis=0) for h in range(0, IDX_HEADS, 2)]

    def score_body(c, carry):
        row0 = pl.multiple_of(c * KC, KC)
        ki_c = ki_ref[pl.ds(row0, KC), :]
        sc = jnp.zeros((KC, QT), f32)
        for hp, qi_pair in enumerate(qi_pairs):
            s2 = lax.dot_general(ki_c, qi_pair, _NT, preferred_element_type=f32)
            for j in range(2):
                s = s2[:, j * QT:(j + 1) * QT]
                sc = sc + jnp.maximum(s, 0.0) * w_heads[2 * hp + j]
        sc = jnp.where(sc == 0.0, 0.0, sc)
        bits = lax.bitcast_convert_type(sc, i32)
        key = bits ^ ((bits >> 31) & 0x7FFFFFFF)
        for blk in range(BLOCKS_PER_CHUNK):
            planes = _bit_planes([key[blk * PLANE_BLOCK + SUBLANES * j:blk * PLANE_BLOCK + SUBLANES * (j + 1)]
                                  for j in range(32)])
            planes[0] = planes[0] ^ -1
            for p in range(32):
                plane_ref[p, BLOCKS_PER_CHUNK * c + blk] = planes[p]
        return carry

    lax.fori_loop(0, nch, score_body, 0)

    nblk = plane_ref.shape[1]

    def clear_body(c, carry):
        for blk in range(BLOCKS_PER_CHUNK):
            for p in range(32):
                plane_ref[p, BLOCKS_PER_CHUNK * c + blk] = jnp.zeros((SUBLANES, QT), i32)
        return carry

    lax.fori_loop(nch, nblk // BLOCKS_PER_CHUNK, clear_body, 0)

    def pass_body(p, carry):
        remaining, took = carry
        plane = plane_ref[p]
        alive = bits_ref[0]
        hit = alive & plane
        ones = jnp.sum(lax.population_count(hit), axis=0).astype(f32)
        ones = jnp.sum(ones, axis=0, keepdims=True)
        take = ones >= remaining
        drop = jnp.where(take, 0, -1)
        bits_ref[0] = alive & (plane ^ drop)
        bits_ref[1] = bits_ref[1] | (hit & drop)
        return jnp.where(take, remaining, remaining - ones), jnp.where(take, 1.0, took)

    first_row = (PLANE_BLOCK * lax.broadcasted_iota(i32, (nblk, SUBLANES, QT), 0)
                 + lax.broadcasted_iota(i32, (nblk, SUBLANES, QT), 1))
    n_adm = jnp.clip(((qpos - first_row) >> 3) + 1, 0, 32)
    top_bits = lax.shift_left(jnp.full((nblk, SUBLANES, QT), -1, i32), 32 - jnp.maximum(n_adm, 1))
    bits_ref[0] = jnp.where(n_adm >= 1, top_bits, 0)
    bits_ref[1] = jnp.zeros((nblk, SUBLANES, QT), i32)
    need, took = lax.fori_loop(0, 32, pass_body, (jnp.full((1, QT), float(topk), f32), jnp.zeros((1, QT), f32)))
    need = jnp.where(took > 0.0, need, 0.0)
    tri = tri_ref[...]

    def unpack(words):
        return jnp.concatenate([(words >> (31 - j)) & 1 for j in range(32)], axis=0).astype(f32)

    def select_bias(blk, seen):
        tie = unpack(bits_ref[0, blk])
        rank = seen + jnp.dot(tri, _bf(tie), preferred_element_type=f32)
        sel = unpack(bits_ref[1, blk]) + jnp.where(rank <= need, tie, 0.0)
        return (sel - 1.0) * (-NEG_BIG), rank[PLANE_BLOCK - 1:PLANE_BLOCK]

    q = q_ref[...].astype(f32)
    hpg = ATT_HEADS // ATT_KV_HEADS
    q_stacks = []
    for g in range(ATT_KV_HEADS):
        parts = []
        for hh in range(hpg):
            h = g * hpg + hh
            qp = q[:, (h // 2) * LANES:(h // 2 + 1) * LANES]
            parts.append(jnp.where(left, qp, 0.0) if h % 2 == 0 else jnp.where(left, 0.0, qp))
        q_stacks.append(_bf(jnp.concatenate(parts, axis=0)))
    HB = PLANE_BLOCK
    ones_lo = _bf(jnp.where(lax.broadcasted_iota(i32, (LANES, HB), 0) < ATT_HEAD_DIM, 0.0, 1.0))
    m_ref[...] = jnp.full(m_ref.shape, NEG_BIG, f32)
    acc_ref[...] = jnp.zeros(acc_ref.shape, f32)

    def logits(c, half, seen):
        row0 = c * KC + half * HB
        row0 = row0 if isinstance(row0, int) else pl.multiple_of(row0, HB)
        bias, seen = select_bias(BLOCKS_PER_CHUNK * c + half, seen)
        bias = jnp.concatenate([bias] * hpg, axis=1)
        for g in range(ATT_KV_HEADS):
            k2 = k_ref[pl.ds(row0, HB), g * LANES:(g + 1) * LANES]
            s_ref[half, g] = lax.dot_general(k2, q_stacks[g], _NT, preferred_element_type=f32) + bias
        return seen

    def softmax_pv(c, half):
        for g in range(ATT_KV_HEADS):
            v1 = vt_ref[c, g * LANES:(g + 1) * LANES, half * HB:(half + 1) * HB] + ones_lo
            s = s_ref[half, g]
            m_prev = m_ref[g]
            m_new = jnp.maximum(m_prev, jnp.max(s, axis=0, keepdims=True))
            alpha = jnp.exp2(m_prev - m_new)
            p = jnp.exp2(s - m_new)
            acc_ref[g] = alpha * acc_ref[g] + jnp.dot(v1, _bf(p), preferred_element_type=f32)
            m_ref[g] = m_new

    def chunk_body(c, seen):
        seen = logits(c, 1, seen)
        softmax_pv(c, 0)
        seen = logits(c + 1, 0, seen)
        softmax_pv(c, 1)
        return seen

    seen = lax.fori_loop(0, nch - 1, chunk_body, logits(0, 0, jnp.zeros((1, QT), f32)))
    logits(nch - 1, 1, seen)
    softmax_pv(nch - 1, 0)
    softmax_pv(nch - 1, 1)

    for qt in range(QT // LANES):
        outs = []
        for g in range(ATT_KV_HEADS):
            heads = []
            for hh in range(hpg):
                col = hh * QT + qt * LANES
                blk = acc_ref[g, :, col:col + LANES].T
                heads.append(blk / pltpu.roll(blk, ATT_HEAD_DIM, axis=1))
            for pp in range(hpg // 2):
                outs.append(jnp.where(left, heads[2 * pp], pltpu.roll(heads[2 * pp + 1], ATT_HEAD_DIM, axis=1)))
        o_ref[qt * LANES:(qt + 1) * LANES, :] = _bf(jnp.concatenate(outs, axis=1))


def _dsa(q, qi, misc, k, vt, ki, batch, seq):
    nq = seq // Q_TILE
    nkc = seq // KEY_CHUNK
    topk = min(TOPK_MAX, seq // 4)
    hpg = ATT_HEADS // ATT_KV_HEADS
    qrow = lambda w: pl.BlockSpec((Q_TILE, w), lambda b, i: (b * nq + i, 0))
    seq_spec = lambda w: pl.BlockSpec((seq, w), lambda b, i: (b, 0))
    tri = _bf(jnp.tril(jnp.ones((TIE_BLOCK, TIE_BLOCK), jnp.float32)))
    assert vt.shape == (batch * nkc, SEG_V, KEY_CHUNK)
    assert BLOCKS_PER_CHUNK == 2
    s_bytes = 2 * ATT_KV_HEADS * hpg * Q_TILE * KEY_CHUNK * 4
    est = (2 * seq * (SEG_K + SEG_V + SEG_KI) * 2 + nkc * KEY_CHUNK * Q_TILE * 4 + s_bytes
           + ATT_KV_HEADS * (LANES + SUBLANES) * hpg * Q_TILE * 4 + 8 * hpg * Q_TILE * KEY_CHUNK * 4)
    return pl.pallas_call(
        functools.partial(_dsa_kernel, topk=topk),
        grid=(batch, nq),
        in_specs=[qrow(SEG_Q), qrow(SEG_QI), qrow(SEG_MISC), seq_spec(SEG_K),
                  pl.BlockSpec((nkc, SEG_V, KEY_CHUNK), lambda b, i: (b, 0, 0)), seq_spec(SEG_KI),
                  pl.BlockSpec((TIE_BLOCK, TIE_BLOCK), lambda b, i: (0, 0))],
        out_specs=qrow(ATT_WIDTH),
        out_shape=jax.ShapeDtypeStruct((batch * seq, ATT_WIDTH), jnp.bfloat16),
        scratch_shapes=[pltpu.VMEM((32, nkc * BLOCKS_PER_CHUNK, SUBLANES, Q_TILE), jnp.int32),
                        pltpu.VMEM((2, nkc * BLOCKS_PER_CHUNK, SUBLANES, Q_TILE), jnp.int32),
                        pltpu.VMEM((BLOCKS_PER_CHUNK, ATT_KV_HEADS, PLANE_BLOCK, hpg * Q_TILE), jnp.float32),
                        pltpu.VMEM((ATT_KV_HEADS, 1, hpg * Q_TILE), jnp.float32),
                        pltpu.VMEM((ATT_KV_HEADS, LANES, hpg * Q_TILE), jnp.float32)],
        compiler_params=_params(("parallel", "arbitrary"), est),
        name="dsa",
    )(q, qi, misc, k, vt, ki, tri)


def _merge_kernel(yssd_ref, yatt_ref, u_ref, uh_ref, g_ref, h_ref, pw_ref, ps_ref, wus_ref, wua_ref, wup_ref,
                  wo_ref, o_ref, ubuf, *, tiles_per_seq):
    f32 = jnp.float32
    tm = u_ref.shape[0]
    it = pl.program_id(0) % tiles_per_seq
    ubuf[POOL_HALO:POOL_HALO + tm, :] = u_ref[...]
    ubuf[0:POOL_HALO, :] = jnp.where(it == 0, 0.0, uh_ref[...])
    t1 = (it * tm + 1 + lax.broadcasted_iota(jnp.int32, (tm, POOL_GROUP_DIM), 0)).astype(f32)
    mixed = []
    for gi, win in enumerate(POOL_WINDOWS):
        cols = slice(gi * POOL_GROUP_DIM, (gi + 1) * POOL_GROUP_DIM)
        cur = ubuf[POOL_HALO:POOL_HALO + tm, cols]
        s = cur
        for d in range(1, win):
            s = s + ubuf[POOL_HALO - d:POOL_HALO - d + tm, cols]
        pooled = s / jnp.minimum(t1, float(win)) - cur
        mixed.append(jnp.dot(_bf(pooled), pw_ref[gi], preferred_element_type=f32))
    ypool = _bf(jnp.concatenate(mixed, axis=1) * ps_ref[...])

    g = g_ref[...].astype(f32)
    merged = (_sigmoid(g[:, :D_MODEL]) * jnp.dot(yssd_ref[...], wus_ref[...], preferred_element_type=f32)
              + _sigmoid(g[:, D_MODEL:2 * D_MODEL]) * jnp.dot(yatt_ref[...], wua_ref[...], preferred_element_type=f32)
              + _sigmoid(g[:, 2 * D_MODEL:]) * jnp.dot(ypool, wup_ref[...], preferred_element_type=f32))
    o_ref[...] = h_ref[...] + jnp.dot(_bf(merged), wo_ref[...], preferred_element_type=f32)


def _merge(y_ssd, y_att, u, gates, h, pool_w, pool_scale, w_up_ssd, w_up_attn, w_up_pool, w_out, l, seq):
    n = h.shape[0]
    tm = ROW_TILE
    tiles_per_seq = seq // tm
    row = lambda w: pl.BlockSpec((tm, w), lambda r: (r, 0))
    halo = pl.BlockSpec((POOL_HALO, POOL_WIDTH), lambda r: (jnp.maximum(r * (tm // POOL_HALO) - 1, 0), 0))
    w_bytes = (POOL_GROUPS * POOL_GROUP_DIM ** 2 + (SSD_INNER + ATT_WIDTH + POOL_WIDTH + D_MODEL) * D_MODEL) * 2
    est = (w_bytes + 2 * tm * ((SSD_INNER + ATT_WIDTH) * 2 + (POOL_WIDTH + SEG_G + 2 * D_MODEL) * 4)
           + (tm + POOL_HALO) * POOL_WIDTH * 4 + 12 * tm * D_MODEL * 4)
    return pl.pallas_call(
        functools.partial(_merge_kernel, tiles_per_seq=tiles_per_seq),
        grid=(n // tm,),
        in_specs=[row(SSD_INNER), row(ATT_WIDTH), row(POOL_WIDTH), halo, row(SEG_G), row(D_MODEL),
                  _layer_spec((POOL_GROUPS, POOL_GROUP_DIM, POOL_GROUP_DIM), l), _layer_spec((1, POOL_WIDTH), l),
                  _layer_spec((SSD_INNER, D_MODEL), l), _layer_spec((ATT_WIDTH, D_MODEL), l),
                  _layer_spec((POOL_WIDTH, D_MODEL), l), _layer_spec((D_MODEL, D_MODEL), l)],
        out_specs=row(D_MODEL),
        out_shape=jax.ShapeDtypeStruct((n, D_MODEL), jnp.float32),
        scratch_shapes=[pltpu.VMEM((tm + POOL_HALO, POOL_WIDTH), jnp.float32)],
        compiler_params=_params(("parallel",), est),
        name="merge",
    )(y_ssd, y_att, u, u, gates, h, pool_w, pool_scale, w_up_ssd, w_up_attn, w_up_pool, w_out)


def _ffn_kernel(h_ref, nw_ref, wi_ref, wo_ref, fw_ref, o_ref, *, final):
    f32 = jnp.float32
    h = h_ref[...]
    hn = _bf(_rms(h, nw_ref[...]))
    out = h
    lo = 0
    for blk in FFN_BLOCKS:
        a = jnp.dot(hn, wi_ref[:, lo:lo + blk], preferred_element_type=f32)
        b = jnp.dot(hn, wi_ref[:, FFN_HIDDEN + lo:FFN_HIDDEN + lo + blk], preferred_element_type=f32)
        out = out + jnp.dot(_bf(a * _sigmoid(a) * b), wo_ref[lo:lo + blk, :], preferred_element_type=f32)
        lo += blk
    if final:
        out = _rms(out, fw_ref[...])
    o_ref[...] = out


def _ffn(h, norm_w, w_ffn_in, w_ffn_out, final_w, l, final):
    n = h.shape[0]
    tm = ROW_TILE
    row = pl.BlockSpec((tm, D_MODEL), lambda r: (r, 0))
    est = (3 * D_MODEL * FFN_HIDDEN * 2 + 4 * tm * D_MODEL * 4 + 6 * tm * max(FFN_BLOCKS) * 4
           + 4 * tm * D_MODEL * 4)
    return pl.pallas_call(
        functools.partial(_ffn_kernel, final=final),
        grid=(n // tm,),
        in_specs=[row, _layer_spec((1, D_MODEL), l), _layer_spec((D_MODEL, 2 * FFN_HIDDEN), l),
                  _layer_spec((FFN_HIDDEN, D_MODEL), l),
                  pl.BlockSpec((1, D_MODEL), lambda r: (0, 0))],
        out_specs=row,
        out_shape=jax.ShapeDtypeStruct((n, D_MODEL), jnp.float32),
        compiler_params=_params(("parallel",), est),
        name="ffn",
    )(h, norm_w, w_ffn_in, w_ffn_out, final_w)


def _pack_w_in(w_in):
    offs, acc = [], 0
    for s in (SSD_INNER, SSD_CONV_DIM, SSD_HEADS, ATT_WIDTH, ATT_KV_HEADS * ATT_HEAD_DIM,
              ATT_KV_HEADS * ATT_HEAD_DIM, IDX_HEADS * IDX_HEAD_DIM, IDX_HEAD_DIM, IDX_HEADS, POOL_WIDTH):
        acc += s
        offs.append(acc)
    z, xbc, dt, q, k, v, qi, ki, wi, u, gates = jnp.split(w_in, offs, axis=-1)
    q = q * (ATT_SCALE * LOG2_E)
    dh = ATT_HEAD_DIM
    kv = [k[..., g * dh:(g + 1) * dh] for g in range(ATT_KV_HEADS) for _ in range(2)]
    for g in range(ATT_KV_HEADS):
        kv += [v[..., g * dh:(g + 1) * dh], jnp.zeros_like(v[..., :dh])]
    pad = jnp.zeros(w_in.shape[:-1] + (SEG_MISC - SSD_HEADS - IDX_HEADS,), w_in.dtype)
    return _bf(jnp.concatenate([z, xbc, q] + kv + [qi, ki, ki, dt, wi, pad, u, gates], axis=-1))


def _pad_lanes(a):
    depth, w = a.shape
    return jnp.pad(a, ((0, 0), (0, LANES - w))).reshape(depth, 1, LANES)


def kernel(x, norm1_w, w_in, conv_w, conv_b, dt_bias, a_log, d_skip, ssd_norm_w, pool_w, pool_scale, w_up_ssd,
           w_up_attn, w_up_pool, w_out, norm2_w, w_ffn_in, w_ffn_out, final_norm_w):
    batch, seq, d = x.shape
    assert d == D_MODEL and seq % KEY_CHUNK == 0 and seq % ROW_TILE == 0 and (batch * seq) % ROW_TILE == 0
    assert seq % Q_TILE == 0 and seq % (SSD_CHUNK * SSD_STEP_CHUNKS) == 0
    depth = w_in.shape[0]
    row3 = lambda a: a.reshape(depth, 1, a.shape[-1])

    wcat = _pack_w_in(w_in)
    norm1 = row3(norm1_w)
    conv_b3 = row3(conv_b)
    dtb = _pad_lanes(dt_bias)
    alog = _pad_lanes(a_log)
    dskip_x = row3(jnp.repeat(d_skip, SSD_HEAD_DIM, axis=-1))
    ssd_nw = row3(ssd_norm_w)
    pool_s = row3(pool_scale)
    norm2 = row3(norm2_w)
    final_w = final_norm_w.reshape(1, D_MODEL)
    pool_wb, wus, wua, wup, wo, wfi, wfo = map(_bf, (pool_w, w_up_ssd, w_up_attn, w_up_pool, w_out, w_ffn_in,
                                                     w_ffn_out))

    h = x.reshape(batch * seq, D_MODEL)
    for l in range(depth):
        z, xbc, q, k, vt, qi, ki, misc, u, gates = _inproj(h, norm1, wcat, l)
        y_ssd = _ssd(xbc, z, misc, conv_w, conv_b3, dtb, alog, dskip_x, ssd_nw, l, batch, seq)
        y_att = _dsa(q, qi, misc, k, vt, ki, batch, seq)
        h = _merge(y_ssd, y_att, u, gates, h, pool_wb, pool_s, wus, wua, wup, wo, l, seq)
        h = _ffn(h, norm2, wfi, wfo, final_w, l, final=(l == depth - 1))
    return h.reshape(batch, seq, D_MODEL)
```
